```python
import jax
import jax.numpy as jnp
from jax import lax
import numpy as np

D_MODEL = 1024
BATCH = 4
SEQ = 8192
DEPTH = 1
DEC_BATCH = 32
DEC_SEQ = 8
PAST_LEN = 16384
PAGE_SIZE = 128

A_WIDTH = D_MODEL // 2
A_GROUPS = 8
A_HEAD = A_WIDTH // A_GROUPS
CHUNK = 128
N_HEADS = 8
HEAD_DIM = 64
B_WIDTH = N_HEADS * HEAD_DIM
MOBA_BLOCK = 256
MOBA_TOPK = 3
Q_BLOCK = 64
ROPE_THETA = 10000.0
IN_WIDTH = 2 * A_WIDTH + 3 * B_WIDTH + 2 * D_MODEL
N_GROUPS = 4
EXPERTS_PER_GROUP = 8
N_EXPERTS = N_GROUPS * EXPERTS_PER_GROUP
TOPK_INNER = 2
EXPERT_FF = D_MODEL // 2
DISPATCH_BLOCK = 128
PLE_DIM = 256
EPS = 1e-6
NEG_INF = -1e30

kernel_name = 'hybrid_gmlp_moba_hmoe_step'


def _rmsnorm(x, g):
    xf = x.astype(jnp.float32)
    y = xf * lax.rsqrt(jnp.mean(xf * xf, axis=-1, keepdims=True) + EPS)
    return (y * g.astype(jnp.float32)).astype(x.dtype)


def _rope(x, pos):
    half = HEAD_DIM // 2
    inv = ROPE_THETA ** (-jnp.arange(half, dtype=jnp.float32) / half)
    ang = pos.astype(jnp.float32)[:, None] * inv[None, :]
    cos = jnp.cos(ang)[None, :, None, :]
    sin = jnp.sin(ang)[None, :, None, :]
    xf = x.astype(jnp.float32)
    x1, x2 = xf[..., :half], xf[..., half:]
    return jnp.concatenate([x1 * cos - x2 * sin, x2 * cos + x1 * sin], axis=-1).astype(x.dtype)


def _split_projection(z):
    offs = [A_WIDTH, 2 * A_WIDTH, 2 * A_WIDTH + B_WIDTH, 2 * A_WIDTH + 2 * B_WIDTH,
            2 * A_WIDTH + 3 * B_WIDTH, 2 * A_WIDTH + 3 * B_WIDTH + D_MODEL]
    return jnp.split(z, offs, axis=-1)


def _heads(zq, zk, zv, g_q, g_k, pos):
    n, s = zq.shape[0], zq.shape[1]
    shp = (n, s, N_HEADS, HEAD_DIM)
    q = _rope(_rmsnorm(zq.reshape(shp), g_q), pos)
    k = _rope(_rmsnorm(zk.reshape(shp), g_k), pos)
    return q, k, zv.reshape(shp)


def _spatial_gating(zu, zv, g_v, w_s, b_s):
    n, t = zv.shape[0], zv.shape[1]
    u = jax.nn.gelu(zu)
    v = _rmsnorm(jax.nn.gelu(zv), g_v)
    mask = jnp.tril(jnp.ones((t, t), dtype=bool))
    w = jnp.where(mask[None], w_s[:, :t, :t], 0)
    vg = v.reshape(n, t, A_GROUPS, A_HEAD)
    s = jnp.einsum('gts,nsgc->ntgc', w, vg) + b_s[:, :t].T[None, :, :, None]
    return u * s.reshape(n, t, A_WIDTH), v


def _select_blocks(q, kmean, own):
    s = jnp.einsum('nhqd,nhbd->nhqb', q.astype(jnp.float32), kmean)
    blk = jnp.arange(kmean.shape[2])
    allowed = blk[None, :] < own[:, None]
    s = jnp.where(allowed[None, None], s, NEG_INF)
    _, idx = lax.top_k(s, MOBA_TOPK)
    valid = idx < own[None, None, :, None]
    return idx, valid


def _moba_attend(q, k_sel, v_sel, sel_valid, k_own, v_own, own_mask):
    n, h, nq = q.shape[0], q.shape[1], q.shape[2]
    scale = HEAD_DIM ** -0.5
    s_sel = jnp.einsum('nhqd,nhqmkd->nhqmk', q, k_sel).astype(jnp.float32) * scale
    s_sel = jnp.where(sel_valid[..., None], s_sel, NEG_INF).reshape(n, h, nq, -1)
    s_own = jnp.einsum('nhqd,nhkd->nhqk', q, k_own).astype(jnp.float32) * scale
    s_own = jnp.where(own_mask[None, None], s_own, NEG_INF)
    m = s_sel.shape[-1]
    p = jax.nn.softmax(jnp.concatenate([s_sel, s_own], axis=-1), axis=-1).astype(v_own.dtype)
    p_sel = p[..., :m].reshape(k_sel.shape[:-1])
    p_own = p[..., m:]
    return (jnp.einsum('nhqmk,nhqmkd->nhqd', p_sel, v_sel)
            + jnp.einsum('nhqk,nhkd->nhqd', p_own, v_own))


def _moba_prompt(q, k, v):
    bsz, seq = q.shape[0], q.shape[1]
    nbk = max(-(-seq // MOBA_BLOCK), MOBA_TOPK)
    pad = nbk * MOBA_BLOCK - seq
    qh = q.transpose(0, 2, 1, 3)
    kb = jnp.pad(k.transpose(0, 2, 1, 3), ((0, 0), (0, 0), (0, pad), (0, 0))).reshape(
        bsz, N_HEADS, nbk, MOBA_BLOCK, HEAD_DIM)
    vb = jnp.pad(v.transpose(0, 2, 1, 3), ((0, 0), (0, 0), (0, pad), (0, 0))).reshape(
        bsz, N_HEADS, nbk, MOBA_BLOCK, HEAD_DIM)
    kmean = jnp.mean(kb.astype(jnp.float32), axis=3)
    b_idx = jnp.arange(bsz)[:, None, None, None]
    h_idx = jnp.arange(N_HEADS)[None, :, None, None]

    def one_block(i):
        q0 = i * Q_BLOCK
        qb = lax.dynamic_slice_in_dim(qh, q0, Q_BLOCK, axis=2)
        qpos = q0 + jnp.arange(Q_BLOCK)
        own = q0 // MOBA_BLOCK
        idx, valid = _select_blocks(qb, kmean, qpos // MOBA_BLOCK)
        k_sel = kb[b_idx, h_idx, idx]
        v_sel = vb[b_idx, h_idx, idx]
        k_own = lax.dynamic_index_in_dim(kb, own, axis=2, keepdims=False)
        v_own = lax.dynamic_index_in_dim(vb, own, axis=2, keepdims=False)
        kpos = own * MOBA_BLOCK + jnp.arange(MOBA_BLOCK)
        own_mask = kpos[None, :] <= qpos[:, None]
        return _moba_attend(qb, k_sel, v_sel, valid, k_own, v_own, own_mask)

    out = lax.map(one_block, jnp.arange(seq // Q_BLOCK))
    return out.transpose(1, 0, 3, 2, 4).reshape(bsz, seq, B_WIDTH)


def _moba_sample(q, k_new, v_new, cache_k, cache_v, page_table):
    dbsz, dseq = q.shape[0], q.shape[1]
    nbp = PAST_LEN // MOBA_BLOCK
    own = nbp
    r = PAST_LEN - own * MOBA_BLOCK
    nbk = max(nbp, MOBA_TOPK)
    k_past = cache_k[page_table].reshape(dbsz, PAST_LEN, N_HEADS, HEAD_DIM)
    kmean = jnp.mean(k_past[:, :nbp * MOBA_BLOCK].reshape(
        dbsz, nbp, MOBA_BLOCK, N_HEADS, HEAD_DIM).astype(jnp.float32), axis=2).transpose(0, 2, 1, 3)
    kmean = jnp.pad(kmean, ((0, 0), (0, 0), (0, nbk - nbp), (0, 0)))
    qh = q.transpose(0, 2, 1, 3)
    qpos = PAST_LEN + jnp.arange(dseq)
    idx, valid = _select_blocks(qh, kmean, qpos // MOBA_BLOCK)
    pos = jnp.minimum(idx[..., None] * MOBA_BLOCK + jnp.arange(MOBA_BLOCK), PAST_LEN - 1)
    b_idx = jnp.arange(dbsz)[:, None, None, None, None]
    h_idx = jnp.arange(N_HEADS)[None, :, None, None, None]
    k_sel = k_past[b_idx, pos, h_idx]
    v_sel = cache_v[page_table[b_idx, pos // PAGE_SIZE], pos % PAGE_SIZE, h_idx]
    pos_op = own * MOBA_BLOCK + jnp.arange(r)
    k_op = k_past[:, pos_op]
    v_op = cache_v[page_table[:, pos_op // PAGE_SIZE], (pos_op % PAGE_SIZE)[None, :]]
    k_own = jnp.concatenate([k_op, k_new], axis=1).transpose(0, 2, 1, 3)
    v_own = jnp.concatenate([v_op, v_new], axis=1).transpose(0, 2, 1, 3)
    kpos = jnp.concatenate([pos_op, qpos])
    own_mask = kpos[None, :] <= qpos[:, None]
    out = _moba_attend(qh, k_sel, v_sel, valid, k_own, v_own, own_mask)
    return out.transpose(0, 2, 1, 3).reshape(dbsz, dseq, B_WIDTH)


def _hier_moe(h, w_rg, b_rg, w_re, b_re, w1, w3, w2):
    n = h.shape[0]
    glog = (h @ w_rg).astype(jnp.float32) + b_rg.astype(jnp.float32)
    gprob = jax.nn.softmax(glog, axis=-1)
    grp = jnp.argmax(glog, axis=-1)
    p_grp = jnp.take_along_axis(gprob, grp[:, None], axis=1)[:, 0]
    elog = ((h @ w_re).astype(jnp.float32) + b_re.astype(jnp.float32)).reshape(n, N_GROUPS, EXPERTS_PER_GROUP)
    elog = jnp.take_along_axis(elog, grp[:, None, None], axis=1)[:, 0]
    top_v, top_j = lax.top_k(elog, TOPK_INNER)
    p_exp = jax.nn.softmax(top_v, axis=-1)
    weight = (p_grp[:, None] * p_exp).reshape(-1)
    eid = (grp[:, None] * EXPERTS_PER_GROUP + top_j).reshape(-1)
    tok = jnp.repeat(jnp.arange(n), TOPK_INNER)
    n_assign = n * TOPK_INNER
    onehot = (eid[:, None] == jnp.arange(N_EXPERTS)[None, :]).astype(jnp.int32)
    counts = jnp.sum(onehot, axis=0)
    rank = jnp.take_along_axis(jnp.cumsum(onehot, axis=0), eid[:, None], axis=1)[:, 0] - 1
    pcounts = ((counts + DISPATCH_BLOCK - 1) // DISPATCH_BLOCK) * DISPATCH_BLOCK
    pend = jnp.cumsum(pcounts)
    pstart = pend - pcounts
    dest = pstart[eid] + rank
    n_blocks = -(-(n_assign + N_EXPERTS * (DISPATCH_BLOCK - 1)) // DISPATCH_BLOCK)
    xd = jnp.zeros((n_blocks * DISPATCH_BLOCK, D_MODEL), h.dtype).at[dest].set(h[tok])
    blk_start = jnp.arange(n_blocks) * DISPATCH_BLOCK
    blk_e = jnp.minimum(jnp.sum(pend[None, :] <= blk_start[:, None], axis=1), N_EXPERTS - 1)

    def expert_block(args):
        xb, e = args
        return (jax.nn.silu(xb @ w1[e]) * (xb @ w3[e])) @ w2[e]

    yd = lax.map(expert_block, (xd.reshape(n_blocks, DISPATCH_BLOCK, D_MODEL), blk_e)).reshape(-1, D_MODEL)
    return jax.ops.segment_sum(yd[dest] * weight[:, None].astype(yd.dtype), tok, num_segments=n)


def _finish_layer(x, out_a, out_b, ga, gb, p, w_a, w_b, w_o, g_ffn, w_rg, b_rg, w_re, b_re,
                  w1, w3, w2, g_ple, w_ple_gate, w_ple):
    merged = jax.nn.sigmoid(ga) * (out_a @ w_a) + jax.nn.sigmoid(gb) * (out_b @ w_b)
    x = x + merged @ w_o
    hf = _rmsnorm(x, g_ffn)
    x = x + _hier_moe(hf.reshape(-1, D_MODEL), w_rg, b_rg, w_re, b_re, w1, w3, w2).reshape(x.shape)
    gate = jax.nn.sigmoid(_rmsnorm(x, g_ple) @ w_ple_gate)
    return x + gate * (p @ w_ple)


def setup_inputs(seed: int = 0) -> dict:
    key = jax.random.key(seed)
    ks = jax.random.split(key, 32)

    def nrm(k, shape, scale):
        return jax.random.normal(k, shape, jnp.float32) * scale

    def gain(k, shape):
        return 1.0 + 0.1 * jax.random.normal(k, shape, jnp.float32)

    n_pages = PAST_LEN // PAGE_SIZE
    n_used = DEC_BATCH * n_pages
    n_pool = (n_used * 5) // 4
    page_table = jax.random.permutation(ks[4], n_pool)[:n_used].astype(jnp.int32).reshape(DEC_BATCH, n_pages)
    return {
        'x_prompt': nrm(ks[0], (BATCH, SEQ, D_MODEL), 1.0),
        'x_sample': nrm(ks[1], (DEC_BATCH, DEC_SEQ, D_MODEL), 1.0),
        'cache_k': nrm(ks[2], (DEPTH, n_pool, PAGE_SIZE, N_HEADS, HEAD_DIM), 1.0),
        'cache_v': nrm(ks[3], (DEPTH, n_pool, PAGE_SIZE, N_HEADS, HEAD_DIM), 1.0),
        'page_table': page_table,
        'p_prompt': nrm(ks[5], (DEPTH, BATCH, SEQ, PLE_DIM), 1.0),
        'p_sample': nrm(ks[6], (DEPTH, DEC_BATCH, DEC_SEQ, PLE_DIM), 1.0),
        'g_mix': gain(ks[7], (DEPTH, D_MODEL)),
        'w_in': nrm(ks[8], (DEPTH, D_MODEL, IN_WIDTH), D_MODEL ** -0.5),
        'g_v': gain(ks[9], (DEPTH, A_WIDTH)),
        'w_s': nrm(ks[10], (DEPTH, A_GROUPS, CHUNK, CHUNK), CHUNK ** -0.5),
        'b_s': 1.0 + 0.1 * jax.random.normal(ks[11], (DEPTH, A_GROUPS, CHUNK), jnp.float32),
        'g_q': gain(ks[12], (DEPTH, HEAD_DIM)),
        'g_k': gain(ks[13], (DEPTH, HEAD_DIM)),
        'w_a': nrm(ks[14], (DEPTH, A_WIDTH, D_MODEL), A_WIDTH ** -0.5),
        'w_b': nrm(ks[15], (DEPTH, B_WIDTH, D_MODEL), B_WIDTH ** -0.5),
        'w_o': nrm(ks[16], (DEPTH, D_MODEL, D_MODEL), D_MODEL ** -0.5),
        'g_ffn': gain(ks[17], (DEPTH, D_MODEL)),
        'w_router_group': nrm(ks[18], (DEPTH, D_MODEL, N_GROUPS), D_MODEL ** -0.5),
        'b_router_group': nrm(ks[19], (DEPTH, N_GROUPS), 0.01),
        'w_router_expert': nrm(ks[20], (DEPTH, D_MODEL, N_EXPERTS), D_MODEL ** -0.5),
        'b_router_expert': nrm(ks[21], (DEPTH, N_EXPERTS), 0.01),
        'w1': nrm(ks[22], (DEPTH, N_EXPERTS, D_MODEL, EXPERT_FF), D_MODEL ** -0.5),
        'w3': nrm(ks[23], (DEPTH, N_EXPERTS, D_MODEL, EXPERT_FF), D_MODEL ** -0.5),
        'w2': nrm(ks[24], (DEPTH, N_EXPERTS, EXPERT_FF, D_MODEL), EXPERT_FF ** -0.5),
        'g_ple': gain(ks[25], (DEPTH, D_MODEL)),
        'w_ple_gate': nrm(ks[26], (DEPTH, D_MODEL, D_MODEL), D_MODEL ** -0.5),
        'w_ple': nrm(ks[27], (DEPTH, PLE_DIM, D_MODEL), PLE_DIM ** -0.5),
    }


def reference(x_prompt, x_sample, cache_k, cache_v, page_table, p_prompt, p_sample, g_mix, w_in, g_v,
              w_s, b_s, g_q, g_k, w_a, w_b, w_o, g_ffn, w_router_group, b_router_group,
              w_router_expert, b_router_expert, w1, w3, w2, g_ple, w_ple_gate, w_ple):
    bsz, seq = x_prompt.shape[0], x_prompt.shape[1]
    dseq = x_sample.shape[1]
    pos_p = jnp.arange(seq, dtype=jnp.int32)
    pos_s = PAST_LEN + jnp.arange(dseq, dtype=jnp.int32)
    xp, xs = x_prompt, x_sample
    kp_rows, vp_rows, ks_rows, vs_rows, chunk_rows = [], [], [], [], []
    for l in range(DEPTH):
        tail = (w_a[l], w_b[l], w_o[l], g_ffn[l], w_router_group[l], b_router_group[l],
                w_router_expert[l], b_router_expert[l], w1[l], w3[l], w2[l],
                g_ple[l], w_ple_gate[l], w_ple[l])
        zu, zv, zq, zk, zva, ga, gb = _split_projection(_rmsnorm(xp, g_mix[l]) @ w_in[l])
        out_a, _ = _spatial_gating(zu.reshape(-1, CHUNK, A_WIDTH), zv.reshape(-1, CHUNK, A_WIDTH),
                                   g_v[l], w_s[l], b_s[l])
        q, k, v = _heads(zq, zk, zva, g_q[l], g_k[l], pos_p)
        out_b = _moba_prompt(q, k, v)
        xp = _finish_layer(xp, out_a.reshape(bsz, seq, A_WIDTH), out_b, ga, gb, p_prompt[l], *tail)
        kp_rows.append(k)
        vp_rows.append(v)
        zu, zv, zq, zk, zva, ga, gb = _split_projection(_rmsnorm(xs, g_mix[l]) @ w_in[l])
        out_a, v_chunk = _spatial_gating(zu, zv, g_v[l], w_s[l], b_s[l])
        q, k, v = _heads(zq, zk, zva, g_q[l], g_k[l], pos_s)
        out_b = _moba_sample(q, k, v, cache_k[l], cache_v[l], page_table)
        xs = _finish_layer(xs, out_a, out_b, ga, gb, p_sample[l], *tail)
        ks_rows.append(k)
        vs_rows.append(v)
        chunk_rows.append(v_chunk)
    return (xp, xs, jnp.stack(kp_rows), jnp.stack(vp_rows), jnp.stack(ks_rows), jnp.stack(vs_rows),
            jnp.stack(chunk_rows))
```

```python
import functools

import jax
import jax.numpy as jnp
from jax import lax
from jax.experimental import pallas as pl
from jax.experimental.pallas import tpu as pltpu

F32 = jnp.float32
BF16 = jnp.bfloat16

EPS = 1e-6
NEG_INF = -1e30
A_GROUPS = 8
CHUNK = 128
HEAD_DIM = 64
MOBA_BLOCK = 256
MOBA_TOPK = 3
ROPE_THETA = 10000.0
N_GROUPS = 4
EXPERTS_PER_GROUP = 8
PAGE_SIZE = 128
PLE_LANES = 128
ROW_TILE = 256
MOE_BLOCK = 256
PAGES_PER_STEP = 8
VMEM_LIMIT = 56 * 1024 * 1024

_NT = (((1,), (1,)), ((), ()))


def _rms(x, g):
    return x * lax.rsqrt(jnp.mean(x * x, axis=-1, keepdims=True) + EPS) * g


def _dot(a, b):
    return jnp.dot(a, b, preferred_element_type=F32)


def _dot_nt(a, b):
    return lax.dot_general(a, b, _NT, preferred_element_type=F32)


def _const_spec(shape):
    return pl.BlockSpec(shape, lambda *_: (0,) * len(shape))


def _inproj_kernel(x_ref, gmix_ref, win_ref, gv_ref, wmix_ref, bs_ref, hind_ref, gq_ref, gk_ref,
                   cos_ref, sin_ref, wa_ref,
                   aterm_ref, sgb_ref, q_ref, k_ref, v_ref, kb_ref, vb_ref, *extra,
                   a_width, b_width, d_model, emit_kmean, emit_vchunk):
    tm = x_ref.shape[0]
    h = _rms(x_ref[...], gmix_ref[...]).astype(BF16)
    offs = [0]

    def proj(width):
        o = offs[0]
        offs[0] = o + width
        return _dot(h, win_ref[:, o:o + width])

    zu = proj(a_width)
    zv = proj(a_width)
    zq = proj(b_width)
    zk = proj(b_width)
    zva = proj(b_width)

    u = jax.nn.gelu(zu)
    vn = _rms(jax.nn.gelu(zv), gv_ref[...])
    vb16 = vn.astype(BF16)
    lane_grp = lax.broadcasted_iota(jnp.int32, (CHUNK, a_width), 1) // (a_width // A_GROUPS)
    parts = []
    for c in range(tm // CHUNK):
        vc = vb16[c * CHUNK:(c + 1) * CHUNK, :]
        rhs = jnp.concatenate(
            [jnp.where(lane_grp == g, vc, jnp.zeros_like(vc)) for g in range(A_GROUPS)], axis=0)
        parts.append(_dot(wmix_ref[...], rhs) + bs_ref[...])
    s = parts[0] if len(parts) == 1 else jnp.concatenate(parts, axis=0)
    out_a = (u * s).astype(BF16)
    ga = proj(d_model)
    aterm_ref[...] = jax.nn.sigmoid(ga) * _dot(out_a, wa_ref[...])
    gb = proj(d_model)
    sgb_ref[...] = jax.nn.sigmoid(gb)

    lane = lax.broadcasted_iota(jnp.int32, (tm, b_width), 1)
    first_half = (lane % HEAD_DIM) < (HEAD_DIM // 2)
    cos = cos_ref[...]
    sin = sin_ref[...]

    def headnorm_rope(z, g):
        ms = _dot((z * z).astype(BF16), hind_ref[...])
        y = z * lax.rsqrt(ms + EPS) * g
        swapped = jnp.where(first_half,
                            pltpu.roll(y, b_width - HEAD_DIM // 2, 1),
                            pltpu.roll(y, HEAD_DIM // 2, 1))
        return y * cos + swapped * sin

    q = headnorm_rope(zq, gq_ref[...])
    k = headnorm_rope(zk, gk_ref[...])
    q_ref[...] = q.astype(BF16)
    k_ref[...] = k
    kb_ref[...] = k.astype(BF16)
    v_ref[...] = zva
    vb_ref[...] = zva.astype(BF16)
    idx = 0
    if emit_kmean:
        km_ref = extra[idx]
        idx += 1
        for bi in range(tm // MOBA_BLOCK):
            km_ref[bi] = jnp.mean(k[bi * MOBA_BLOCK:(bi + 1) * MOBA_BLOCK, :], axis=0, keepdims=True)
    if emit_vchunk:
        extra[idx][...] = vn


def _inproj(x, pos_tables, wts, *, emit_kmean, emit_vchunk, pos_blocks):
    n, d_model = x.shape
    a_width = wts['g_v'].shape[-1]
    b_width = wts['hind'].shape[0]
    tm = ROW_TILE
    cos_t, sin_t = pos_tables
    row = lambda i: (i, 0)
    posrow = lambda i: (i % pos_blocks, 0)
    in_specs = [
        pl.BlockSpec((tm, d_model), row),
        _const_spec((1, d_model)),
        _const_spec(wts['w_in'].shape),
        _const_spec((1, a_width)),
        _const_spec(wts['wmix'].shape),
        _const_spec(wts['bs_tab'].shape),
        _const_spec(wts['hind'].shape),
        _const_spec((1, b_width)),
        _const_spec((1, b_width)),
        pl.BlockSpec((tm, b_width), posrow),
        pl.BlockSpec((tm, b_width), posrow),
        _const_spec(wts['w_a'].shape),
    ]
    out_shape = [
        jax.ShapeDtypeStruct((n, d_model), F32),
        jax.ShapeDtypeStruct((n, d_model), F32),
        jax.ShapeDtypeStruct((n, b_width), BF16),
        jax.ShapeDtypeStruct((n, b_width), F32),
        jax.ShapeDtypeStruct((n, b_width), F32),
        jax.ShapeDtypeStruct((n, b_width), BF16),
        jax.ShapeDtypeStruct((n, b_width), BF16),
    ]
    out_specs = [
        pl.BlockSpec((tm, d_model), row), pl.BlockSpec((tm, d_model), row),
        pl.BlockSpec((tm, b_width), row), pl.BlockSpec((tm, b_width), row),
        pl.BlockSpec((tm, b_width), row), pl.BlockSpec((tm, b_width), row),
        pl.BlockSpec((tm, b_width), row),
    ]
    if emit_kmean:
        nbt = tm // MOBA_BLOCK
        out_shape.append(jax.ShapeDtypeStruct((n // MOBA_BLOCK, 1, b_width), F32))
        out_specs.append(pl.BlockSpec((nbt, 1, b_width), lambda i: (i, 0, 0)))
    if emit_vchunk:
        out_shape.append(jax.ShapeDtypeStruct((n, a_width), F32))
        out_specs.append(pl.BlockSpec((tm, a_width), row))
    kern = functools.partial(_inproj_kernel, a_width=a_width, b_width=b_width, d_model=d_model,
                             emit_kmean=emit_kmean, emit_vchunk=emit_vchunk)
    return pl.pallas_call(
        kern, grid=(n // tm,), in_specs=in_specs, out_specs=out_specs, out_shape=out_shape,
        name='inproj',
        compiler_params=pltpu.CompilerParams(dimension_semantics=('arbitrary',),
                                             vmem_limit_bytes=VMEM_LIMIT),
    )(x, wts['g_mix'], wts['w_in'], wts['g_v'], wts['wmix'], wts['bs_tab'], wts['hind'],
      wts['g_q'], wts['g_k'], cos_t, sin_t, wts['w_a'])


def _select_topk(scores, allowed, blk_f, nb):
    sel = jnp.zeros(scores.shape, jnp.bool_)
    for _ in range(MOBA_TOPK):
        cand = jnp.logical_and(allowed, jnp.logical_not(sel))
        scm = jnp.where(cand, scores, -jnp.inf)
        mx = jnp.max(scm, axis=-1, keepdims=True)
        is_max = jnp.logical_and(cand, scm == mx)
        first = jnp.min(jnp.where(is_max, blk_f, float(nb)), axis=-1, keepdims=True)
        sel = jnp.logical_or(sel, jnp.logical_and(is_max, blk_f == first))
    return sel


def _moba_prompt_kernel(q_ref, kb_ref, vb_ref, km_ref, o_ref, *, n_heads, nb):
    tq = q_ref.shape[0]
    i = pl.program_id(1)
    scale = HEAD_DIM ** -0.5
    km = km_ref[0].astype(BF16)
    blk_f = lax.broadcasted_iota(jnp.int32, (tq, nb), 1).astype(F32)
    allowed = blk_f < i.astype(F32)
    rows = lax.broadcasted_iota(jnp.int32, (tq, MOBA_BLOCK), 0)
    cols = lax.broadcasted_iota(jnp.int32, (tq, MOBA_BLOCK), 1)
    causal = cols <= rows
    oh_rows = lax.broadcasted_iota(jnp.int32, (nb, 128), 0)
    own = pl.multiple_of(i * MOBA_BLOCK, MOBA_BLOCK)
    for h in range(n_heads):
        sl = slice(h * HEAD_DIM, (h + 1) * HEAD_DIM)
        qh = q_ref[:, sl]
        sel = _select_topk(_dot_nt(qh, km[:, sl]), allowed, blk_f, nb)
        selw = jnp.where(sel, 1.0, 0.0).astype(BF16)
        qs = (qh.astype(F32) * scale).astype(BF16)

        s = jnp.where(causal, _dot_nt(qs, kb_ref[pl.ds(own, MOBA_BLOCK), sl]), NEG_INF)
        m = jnp.max(s, axis=-1, keepdims=True)
        p = jnp.exp(s - m)
        l = jnp.sum(p, axis=-1, keepdims=True)
        acc = _dot(p.astype(BF16), vb_ref[pl.ds(own, MOBA_BLOCK), sl])

        def body(j, carry):
            m, l, acc = carry
            st = pl.multiple_of(j * MOBA_BLOCK, MOBA_BLOCK)
            s = _dot_nt(qs, kb_ref[pl.ds(st, MOBA_BLOCK), sl])
            picked = _dot(selw, jnp.where(oh_rows == j, 1.0, 0.0).astype(BF16))
            bias = (picked - 1.0) * (-NEG_INF)
            s = s + jnp.concatenate([bias] * (MOBA_BLOCK // 128), axis=1)
            m_new = jnp.maximum(m, jnp.max(s, axis=-1, keepdims=True))
            alpha = jnp.exp(m - m_new)
            p = jnp.exp(s - m_new)
            l = alpha * l + jnp.sum(p, axis=-1, keepdims=True)
            acc = alpha * acc + _dot(p.astype(BF16), vb_ref[pl.ds(st, MOBA_BLOCK), sl])
            return m_new, l, acc

        m, l, acc = lax.fori_loop(0, i, body, (m, l, acc))
        o_ref[:, sl] = (acc / l).astype(o_ref.dtype)


def _moba_prompt(q, kb, vb, kmean, bsz, seq):
    n, width = q.shape
    n_heads = width // HEAD_DIM
    nb = seq // MOBA_BLOCK
    tq = MOBA_BLOCK
    steps = seq // tq
    kern = functools.partial(_moba_prompt_kernel, n_heads=n_heads, nb=nb)
    return pl.pallas_call(
        kern, grid=(bsz, steps),
        in_specs=[
            pl.BlockSpec((tq, width), lambda b, i: (b * steps + i, 0)),
            pl.BlockSpec((seq, width), lambda b, i: (b, 0)),
            pl.BlockSpec((seq, width), lambda b, i: (b, 0)),
            pl.BlockSpec((1, nb, width), lambda b, i: (b, 0, 0)),
        ],
        out_specs=pl.BlockSpec((tq, width), lambda b, i: (b * steps + i, 0)),
        out_shape=jax.ShapeDtypeStruct((n, width), BF16),
        name='moba_prompt',
        compiler_params=pltpu.CompilerParams(dimension_semantics=('arbitrary', 'arbitrary'),
                                             vmem_limit_bytes=VMEM_LIMIT),
    )(q, kb, vb, kmean)


def _moba_sample_kernel(pt_ref, q_ref, kn_ref, vn_ref, *rest, pg, n_heads, nb):
    del pt_ref
    k_refs = rest[:pg]
    v_refs = rest[pg:2 * pg]
    o_ref = rest[2 * pg]
    sc_ref, m_ref, l_ref, acc_ref = rest[2 * pg + 1:]
    ds, width = q_ref.shape
    r = n_heads * ds
    step = pl.program_id(1)
    scale = HEAD_DIM ** -0.5
    row_h = lax.broadcasted_iota(jnp.int32, (r, width), 0) // ds
    lane_h = lax.broadcasted_iota(jnp.int32, (r, width), 1) // HEAD_DIM
    hmask = row_h == lane_h
    q = q_ref[...]
    qt = jnp.concatenate([q] * n_heads, axis=0)
    qbd = jnp.where(hmask, qt, jnp.zeros_like(qt))
    lane_b = lax.broadcasted_iota(jnp.int32, (r, nb), 1)

    @pl.when(step == 0)
    def _():
        sc_ref[...] = jnp.zeros_like(sc_ref)
        m_ref[...] = jnp.zeros_like(m_ref)
        l_ref[...] = jnp.zeros_like(l_ref)

    pages_per_block = MOBA_BLOCK // PAGE_SIZE
    for c in range(pg // pages_per_block):
        kblk = jnp.concatenate([k_refs[pages_per_block * c + t][...] for t in range(pages_per_block)],
                               axis=0).astype(BF16)
        vblk = jnp.concatenate([v_refs[pages_per_block * c + t][...] for t in range(pages_per_block)],
                               axis=0).astype(BF16)
        sraw = _dot_nt(qbd, kblk)
        bscore = jnp.mean(sraw, axis=-1, keepdims=True)
        s = sraw * scale
        mj = jnp.max(s, axis=-1, keepdims=True)
        p = jnp.exp(s - mj)
        lj = jnp.sum(p, axis=-1, keepdims=True)
        jb = step * (pg // pages_per_block) + c
        col = lane_b == jb
        sc_ref[...] = jnp.where(col, bscore, sc_ref[...])
        m_ref[...] = jnp.where(col, mj, m_ref[...])
        l_ref[...] = jnp.where(col, lj, l_ref[...])
        acc_ref[jb] = _dot(p.astype(BF16), vblk)

    @pl.when(step == pl.num_programs(1) - 1)
    def _():
        blk_f = lane_b.astype(F32)
        sel = _select_topk(sc_ref[...], jnp.ones((r, nb), jnp.bool_), blk_f, nb)
        s_own = _dot_nt(qbd, kn_ref[...]) * scale
        key_t = lax.broadcasted_iota(jnp.int32, (r, ds), 1)
        qry_t = lax.broadcasted_iota(jnp.int32, (r, ds), 0) % ds
        s_own = jnp.where(key_t <= qry_t, s_own, NEG_INF)
        m_all = m_ref[...]
        m_tot = jnp.maximum(jnp.max(jnp.where(sel, m_all, NEG_INF), axis=-1, keepdims=True),
                            jnp.max(s_own, axis=-1, keepdims=True))
        w = jnp.where(sel, jnp.exp(m_all - m_tot), 0.0)
        p_own = jnp.exp(s_own - m_tot)
        den = jnp.sum(w * l_ref[...], axis=-1, keepdims=True) + jnp.sum(p_own, axis=-1, keepdims=True)
        num = _dot(p_own.astype(BF16), vn_ref[...])

        def body(j, num):
            wj = jnp.sum(jnp.where(lane_b == j, w, 0.0), axis=-1, keepdims=True)
            return num + wj * acc_ref[j]

        num = lax.fori_loop(0, nb, body, num)
        out = jnp.where(hmask, num / den, 0.0)
        o = out[0:ds, :]
        for h in range(1, n_heads):
            o = o + out[h * ds:(h + 1) * ds, :]
        o_ref[...] = o.astype(o_ref.dtype)


def _moba_sample(q, kb, vb, cache_k, cache_v, page_table, ds):
    n, width = q.shape
    dbsz, n_pages = page_table.shape
    n_heads = width // HEAD_DIM
    assert (n_pages * PAGE_SIZE) % MOBA_BLOCK == 0, "cached length must fill whole MoBA blocks"
    nb = n_pages * PAGE_SIZE // MOBA_BLOCK
    assert nb >= MOBA_TOPK
    pg = PAGES_PER_STEP
    assert n_pages % pg == 0 and ds % 8 == 0
    r = n_heads * ds
    ck = cache_k.reshape(cache_k.shape[0], PAGE_SIZE, width)
    cv = cache_v.reshape(cache_v.shape[0], PAGE_SIZE, width)

    def page_spec(t):
        return pl.BlockSpec((None, PAGE_SIZE, width), lambda b, s, pt: (pt[b, s * pg + t], 0, 0))

    tok = pl.BlockSpec((ds, width), lambda b, s, pt: (b, 0))
    grid_spec = pltpu.PrefetchScalarGridSpec(
        num_scalar_prefetch=1, grid=(dbsz, n_pages // pg),
        in_specs=[tok, tok, tok] + [page_spec(t) for t in range(pg)] + [page_spec(t) for t in range(pg)],
        out_specs=tok,
        scratch_shapes=[pltpu.VMEM((r, nb), F32), pltpu.VMEM((r, nb), F32), pltpu.VMEM((r, nb), F32),
                        pltpu.VMEM((nb, r, width), F32)],
    )
    kern = functools.partial(_moba_sample_kernel, pg=pg, n_heads=n_heads, nb=nb)
    return pl.pallas_call(
        kern, grid_spec=grid_spec, out_shape=jax.ShapeDtypeStruct((n, width), BF16),
        name='moba_sample',
        compiler_params=pltpu.CompilerParams(dimension_semantics=('arbitrary', 'arbitrary'),
                                             vmem_limit_bytes=VMEM_LIMIT),
    )(page_table, q, kb, vb, *([ck] * pg), *([cv] * pg))


def _pack_bf16_pairs(x):
    w = x.shape[1] // 2
    xb = x.astype(BF16).astype(F32)
    hi = lax.bitcast_convert_type(xb[:, :w], jnp.uint32) & jnp.uint32(0xFFFF0000)
    lo = lax.bitcast_convert_type(xb[:, w:], jnp.uint32) >> 16
    return hi | lo


def _unpack_bf16_pairs(u):
    hi = lax.bitcast_convert_type(u & jnp.uint32(0xFFFF0000), F32)
    lo = lax.bitcast_convert_type(u << 16, F32)
    return jnp.concatenate([hi, lo], axis=1).astype(BF16)


def _post_attn_kernel(x_ref, aterm_ref, sgb_ref, ob_ref, wb_ref, wo_ref, gffn_ref, wr_ref, br_ref,
                      x1_ref, hf_ref, route_ref):
    merged = aterm_ref[...] + sgb_ref[...] * _dot(ob_ref[...], wb_ref[...])
    x1 = x_ref[...] + _dot(merged.astype(BF16), wo_ref[...])
    x1_ref[...] = x1
    hf = _rms(x1, gffn_ref[...])
    hf_ref[...] = _pack_bf16_pairs(hf)
    logits = _dot(hf.astype(BF16), wr_ref[...]) + br_ref[...]
    tm, lanes = logits.shape
    lane = lax.broadcasted_iota(jnp.int32, (tm, lanes), 1).astype(F32)
    n_exp = N_GROUPS * EXPERTS_PER_GROUP
    gmask = lane < N_GROUPS
    gl = jnp.where(gmask, logits, -jnp.inf)
    gmax = jnp.max(gl, axis=-1, keepdims=True)
    grp = jnp.min(jnp.where(gl == gmax, lane, float(lanes)), axis=-1, keepdims=True)
    p_grp = 1.0 / jnp.sum(jnp.where(gmask, jnp.exp(gl - gmax), 0.0), axis=-1, keepdims=True)
    lo = N_GROUPS + grp * EXPERTS_PER_GROUP
    emask = jnp.logical_and(jnp.logical_and(lane >= lo, lane < lo + EXPERTS_PER_GROUP),
                            lane < N_GROUPS + n_exp)
    e1 = jnp.where(emask, logits, -jnp.inf)
    v1 = jnp.max(e1, axis=-1, keepdims=True)
    j1 = jnp.min(jnp.where(e1 == v1, lane, float(lanes)), axis=-1, keepdims=True)
    e2 = jnp.where(lane == j1, -jnp.inf, e1)
    v2 = jnp.max(e2, axis=-1, keepdims=True)
    j2 = jnp.min(jnp.where(e2 == v2, lane, float(lanes)), axis=-1, keepdims=True)
    t = jnp.exp(v2 - v1)
    p1 = 1.0 / (1.0 + t)
    p2 = t / (1.0 + t)
    rec = jnp.where(lane == 0, j1 - N_GROUPS, 0.0)
    rec = jnp.where(lane == 1, j2 - N_GROUPS, rec)
    rec = jnp.where(lane == 2, p_grp * p1, rec)
    rec = jnp.where(lane == 3, p_grp * p2, rec)
    route_ref[...] = rec


def _post_attn(x, aterm, sgb, out_b, wts):
    n, d_model = x.shape
    tm = ROW_TILE
    row = lambda i: (i, 0)
    return pl.pallas_call(
        _post_attn_kernel, grid=(n // tm,),
        in_specs=[pl.BlockSpec((tm, d_model), row), pl.BlockSpec((tm, d_model), row),
                  pl.BlockSpec((tm, d_model), row), pl.BlockSpec((tm, out_b.shape[1]), row),
                  _const_spec(wts['w_b'].shape), _const_spec(wts['w_o'].shape),
                  _const_spec((1, d_model)), _const_spec(wts['w_r'].shape), _const_spec((1, PLE_LANES))],
        out_specs=[pl.BlockSpec((tm, d_model), row), pl.BlockSpec((tm, d_model // 2), row),
                   pl.BlockSpec((tm, PLE_LANES), row)],
        out_shape=[jax.ShapeDtypeStruct((n, d_model), F32),
                   jax.ShapeDtypeStruct((n, d_model // 2), jnp.uint32),
                   jax.ShapeDtypeStruct((n, PLE_LANES), F32)],
        name='post_attn',
        compiler_params=pltpu.CompilerParams(dimension_semantics=('arbitrary',),
                                             vmem_limit_bytes=VMEM_LIMIT),
    )(x, aterm, sgb, out_b, wts['w_b'], wts['w_o'], wts['g_ffn'], wts['w_r'], wts['b_r'])


def _row_copy(src_ref, src_row, dst_ref, dst_row, sem):
    return pltpu.make_async_copy(src_ref.at[pl.ds(src_row, 1)], dst_ref.at[pl.ds(dst_row, 1)], sem)


def _dispatch_kernel(dest_ref, hf_ref, xd_in_ref, xd_ref, sem, *, tm, fanout):
    del xd_in_ref
    base = pl.program_id(0) * tm

    def issue(rr, carry):
        for kk in range(fanout):
            _row_copy(hf_ref, base + rr, xd_ref, dest_ref[0, 0, rr * fanout + kk], sem).start()
        return carry

    lax.fori_loop(0, tm, issue, 0)

    def drain(rr, carry):
        for kk in range(fanout):
            _row_copy(hf_ref, 0, xd_ref, 0, sem).wait()
        return carry

    lax.fori_loop(0, tm, drain, 0)


def _dispatch(hf, dest, n_rows):
    n, w = hf.shape
    fanout = dest.shape[0] // n
    tm = min(n, 512)
    steps = n // tm
    dest3 = dest.reshape(steps, 1, tm * fanout)
    xd0 = jnp.zeros((n_rows, w), hf.dtype)
    kern = functools.partial(_dispatch_kernel, tm=tm, fanout=fanout)
    return pl.pallas_call(
        kern, grid=(steps,),
        in_specs=[pl.BlockSpec((1, 1, tm * fanout), lambda i: (i, 0, 0), memory_space=pltpu.SMEM),
                  pl.BlockSpec(memory_space=pl.ANY), pl.BlockSpec(memory_space=pl.ANY)],
        out_specs=pl.BlockSpec(memory_space=pl.ANY),
        out_shape=jax.ShapeDtypeStruct((n_rows, w), hf.dtype),
        scratch_shapes=[pltpu.SemaphoreType.DMA(())],
        input_output_aliases={2: 0},
        name='dispatch',
        compiler_params=pltpu.CompilerParams(dimension_semantics=('arbitrary',)),
    )(dest3, hf, xd0)


def _moe_ffn_kernel(be_ref, nreal_ref, xd_ref, w1_ref, w3_ref, w2_ref, yd_ref):
    del be_ref
    i = pl.program_id(0)

    @pl.when(i < nreal_ref[0])
    def _():
        xb = _unpack_bf16_pairs(xd_ref[...])
        a = _dot(xb, w1_ref[...])
        b = _dot(xb, w3_ref[...])
        yd_ref[...] = _dot((jax.nn.silu(a) * b).astype(BF16), w2_ref[...])

    @pl.when(i >= nreal_ref[0])
    def _():
        yd_ref[...] = jnp.zeros_like(yd_ref)


def _moe_ffn(xd, blk_e, n_real, w1, w3, w2):
    n_rows, half = xd.shape
    d_model = 2 * half
    ff = w1.shape[-1]
    blk = MOE_BLOCK
    grid_spec = pltpu.PrefetchScalarGridSpec(
        num_scalar_prefetch=2, grid=(n_rows // blk,),
        in_specs=[pl.BlockSpec((blk, half), lambda i, be, nr: (i, 0)),
                  pl.BlockSpec((None, d_model, ff), lambda i, be, nr: (be[i], 0, 0)),
                  pl.BlockSpec((None, d_model, ff), lambda i, be, nr: (be[i], 0, 0)),
                  pl.BlockSpec((None, ff, d_model), lambda i, be, nr: (be[i], 0, 0))],
        out_specs=pl.BlockSpec((blk, d_model), lambda i, be, nr: (i, 0)),
    )
    return pl.pallas_call(
        _moe_ffn_kernel, grid_spec=grid_spec,
        out_shape=jax.ShapeDtypeStruct((n_rows, d_model), F32),
        name='moe_ffn',
        compiler_params=pltpu.CompilerParams(dimension_semantics=('arbitrary',),
                                             vmem_limit_bytes=VMEM_LIMIT),
    )(blk_e, n_real, xd, w1, w3, w2)


def _final_kernel(dest_ref, yd_ref, x1_ref, route_ref, p_ref, gple_ref, wpg_ref, wple_ref,
                  y_ref, ybuf, sem, *, fanout):
    tm = x1_ref.shape[0]

    def issue(rr, carry):
        for kk in range(fanout):
            pltpu.make_async_copy(yd_ref.at[pl.ds(dest_ref[0, 0, rr * fanout + kk], 1)],
                                  ybuf.at[kk, pl.ds(rr, 1)], sem).start()
        return carry

    lax.fori_loop(0, tm, issue, 0)

    def drain(rr, carry):
        for kk in range(fanout):
            pltpu.make_async_copy(yd_ref.at[pl.ds(0, 1)], ybuf.at[kk, pl.ds(0, 1)], sem).wait()
        return carry

    lax.fori_loop(0, tm, drain, 0)

    route = route_ref[...]
    moe = route[:, 2:3] * ybuf[0]
    for kk in range(1, fanout):
        moe = moe + route[:, 2 + kk:3 + kk] * ybuf[kk]
    x2 = x1_ref[...] + moe
    gate = jax.nn.sigmoid(_dot(_rms(x2, gple_ref[...]).astype(BF16), wpg_ref[...]))
    y_ref[...] = x2 + gate * _dot(p_ref[...].astype(BF16), wple_ref[...])


def _final(x1, yd, dest, route, p, wts):
    n, d_model = x1.shape
    fanout = dest.shape[0] // n
    tm = ROW_TILE
    steps = n // tm
    dest3 = dest.reshape(steps, 1, tm * fanout)
    row = lambda i: (i, 0)
    kern = functools.partial(_final_kernel, fanout=fanout)
    return pl.pallas_call(
        kern, grid=(steps,),
        in_specs=[pl.BlockSpec((1, 1, tm * fanout), lambda i: (i, 0, 0), memory_space=pltpu.SMEM),
                  pl.BlockSpec(memory_space=pl.ANY),
                  pl.BlockSpec((tm, d_model), row), pl.BlockSpec((tm, PLE_LANES), row),
                  pl.BlockSpec((tm, p.shape[1]), row),
                  _const_spec((1, d_model)), _const_spec(wts['w_pg'].shape), _const_spec(wts['w_ple'].shape)],
        out_specs=pl.BlockSpec((tm, d_model), row),
        out_shape=jax.ShapeDtypeStruct((n, d_model), F32),
        scratch_shapes=[pltpu.VMEM((fanout, tm, d_model), F32), pltpu.SemaphoreType.DMA(())],
        name='final',
        compiler_params=pltpu.CompilerParams(dimension_semantics=('arbitrary',),
                                             vmem_limit_bytes=VMEM_LIMIT),
    )(dest3, yd, x1, route, p, wts['g_ple'], wts['w_pg'], wts['w_ple'])


def _routing_plan(eid, blk):
    n_exp = N_GROUPS * EXPERTS_PER_GROUP
    e = eid.reshape(-1)
    n_assign = e.shape[0]
    onehot = (e[:, None] == jnp.arange(n_exp, dtype=jnp.int32)[None, :]).astype(jnp.int32)
    counts = jnp.sum(onehot, axis=0)
    rank = jnp.take_along_axis(jnp.cumsum(onehot, axis=0), e[:, None], axis=1)[:, 0] - 1
    pcounts = ((counts + blk - 1) // blk) * blk
    pend = jnp.cumsum(pcounts)
    pstart = pend - pcounts
    dest = (pstart[e] + rank).astype(jnp.int32)
    n_blocks = -(-(n_assign + n_exp * (blk - 1)) // blk)
    blk_start = jnp.arange(n_blocks, dtype=jnp.int32) * blk
    blk_e = jnp.minimum(jnp.sum(pend[None, :] <= blk_start[:, None], axis=1), n_exp - 1).astype(jnp.int32)
    n_real = (pend[-1:] // blk).astype(jnp.int32)
    return dest, blk_e, n_real, n_blocks * blk


def _rope_tables(pos, n_heads):
    half = HEAD_DIM // 2
    inv = ROPE_THETA ** (-jnp.arange(half, dtype=F32) / half)
    ang = pos.astype(F32)[:, None] * inv[None, :]
    cos = jnp.cos(ang)
    sin = jnp.sin(ang)
    return (jnp.tile(jnp.concatenate([cos, cos], axis=-1), (1, n_heads)),
            jnp.tile(jnp.concatenate([-sin, sin], axis=-1), (1, n_heads)))


def _mix_tables(w_s_l, b_s_l, t_mix, a_width):
    reps = CHUNK // t_mix
    tri = jnp.tril(jnp.ones((t_mix, t_mix), F32))
    wt = w_s_l[:, :t_mix, :t_mix] * tri[None]
    eye = jnp.eye(reps, dtype=F32)
    wbig = jnp.einsum('ab,gts->gatbs', eye, wt).reshape(A_GROUPS, CHUNK, CHUNK)
    wmix = jnp.transpose(wbig, (1, 0, 2)).reshape(CHUNK, A_GROUPS * CHUNK)
    bs = jnp.tile(b_s_l[:, :t_mix], (1, reps))
    bs_tab = jnp.repeat(bs.T, a_width // A_GROUPS, axis=1)
    return dict(wmix=wmix.astype(BF16), bs_tab=bs_tab)


def _layer_weights(l, g_mix, w_in, g_v, g_q, g_k, w_a, w_b, w_o, g_ffn, w_rg, b_rg, w_re, b_re,
                   w1, w3, w2, g_ple, w_pg, w_ple):
    b_width = w_b.shape[1]
    n_heads = b_width // HEAD_DIM
    d_model = w_o.shape[-1]
    hid = jnp.arange(b_width) // HEAD_DIM
    hind = jnp.where(hid[:, None] == hid[None, :], 1.0 / HEAD_DIM, 0.0)
    n_exp = N_GROUPS * EXPERTS_PER_GROUP
    w_r = jnp.zeros((d_model, PLE_LANES), F32)
    w_r = w_r.at[:, :N_GROUPS].set(w_rg[l]).at[:, N_GROUPS:N_GROUPS + n_exp].set(w_re[l])
    b_r = jnp.zeros((1, PLE_LANES), F32)
    b_r = b_r.at[0, :N_GROUPS].set(b_rg[l]).at[0, N_GROUPS:N_GROUPS + n_exp].set(b_re[l])
    return dict(
        g_mix=g_mix[l][None], w_in=w_in[l].astype(BF16), g_v=g_v[l][None], hind=hind.astype(BF16),
        g_q=jnp.tile(g_q[l], n_heads)[None], g_k=jnp.tile(g_k[l], n_heads)[None],
        w_a=w_a[l].astype(BF16), w_b=w_b[l].astype(BF16), w_o=w_o[l].astype(BF16),
        g_ffn=g_ffn[l][None], w_r=w_r.astype(BF16), b_r=b_r,
        w1=w1[l].astype(BF16), w3=w3[l].astype(BF16), w2=w2[l].astype(BF16),
        g_ple=g_ple[l][None], w_pg=w_pg[l].astype(BF16), w_ple=w_ple[l].astype(BF16),
    )


def _finish(x, aterm, sgb, out_b, p, wts):
    x1, hf, route = _post_attn(x, aterm, sgb, out_b, wts)
    eid = route[:, :2].astype(jnp.int32)
    dest, blk_e, n_real, n_rows = _routing_plan(eid, MOE_BLOCK)
    xd = _dispatch(hf, dest, n_rows)
    yd = _moe_ffn(xd, blk_e, n_real, wts['w1'], wts['w3'], wts['w2'])
    return _final(x1, yd, dest, route, p, wts)


def kernel(x_prompt, x_sample, cache_k, cache_v, page_table, p_prompt, p_sample, g_mix, w_in, g_v, w_s, b_s, g_q, g_k, w_a, w_b, w_o, g_ffn, w_router_group, b_router_group, w_router_expert, b_router_expert, w1, w3, w2, g_ple, w_ple_gate, w_ple):
    bsz, seq, d_model = x_prompt.shape
    dbsz, dseq, _ = x_sample.shape
    depth = g_mix.shape[0]
    b_width = w_b.shape[1]
    n_heads = b_width // HEAD_DIM
    past_len = page_table.shape[1] * PAGE_SIZE
    assert seq % MOBA_BLOCK == 0 and (bsz * seq) % ROW_TILE == 0 and (dbsz * dseq) % ROW_TILE == 0
    assert CHUNK % dseq == 0 and ROW_TILE % CHUNK == 0 and ROW_TILE % MOBA_BLOCK == 0
    assert seq // MOBA_BLOCK >= MOBA_TOPK
    params = (g_mix, w_in, g_v, g_q, g_k, w_a, w_b, w_o, g_ffn, w_router_group, b_router_group,
              w_router_expert, b_router_expert, w1, w3, w2, g_ple, w_ple_gate, w_ple)
    a_width = g_v.shape[-1]
    tab_p = _rope_tables(jnp.arange(seq, dtype=jnp.int32), n_heads)
    pos_s = past_len + (jnp.arange(ROW_TILE, dtype=jnp.int32) % dseq)
    tab_s = _rope_tables(pos_s, n_heads)
    xp = x_prompt.reshape(bsz * seq, d_model)
    xs = x_sample.reshape(dbsz * dseq, d_model)
    kp_rows, vp_rows, ks_rows, vs_rows, chunk_rows = [], [], [], [], []
    for l in range(depth):
        wts = _layer_weights(l, *params)
        wts_p = dict(wts, **_mix_tables(w_s[l], b_s[l], CHUNK, a_width))
        wts_s = dict(wts, **_mix_tables(w_s[l], b_s[l], dseq, a_width))
        aterm, sgb, q, k, v, kb, vb, kmean = _inproj(
            xp, tab_p, wts_p, emit_kmean=True, emit_vchunk=False, pos_blocks=seq // ROW_TILE)
        out_b = _moba_prompt(q, kb, vb, kmean.reshape(bsz, seq // MOBA_BLOCK, b_width), bsz, seq)
        xp = _finish(xp, aterm, sgb, out_b, p_prompt[l].reshape(bsz * seq, -1), wts_p)
        kp_rows.append(k.reshape(bsz, seq, n_heads, HEAD_DIM))
        vp_rows.append(v.reshape(bsz, seq, n_heads, HEAD_DIM))
        aterm, sgb, q, k, v, kb, vb, vchunk = _inproj(
            xs, tab_s, wts_s, emit_kmean=False, emit_vchunk=True, pos_blocks=1)
        out_b = _moba_sample(q, kb, vb, cache_k[l], cache_v[l], page_table, dseq)
        xs = _finish(xs, aterm, sgb, out_b, p_sample[l].reshape(dbsz * dseq, -1), wts_s)
        ks_rows.append(k.reshape(dbsz, dseq, n_heads, HEAD_DIM))
        vs_rows.append(v.reshape(dbsz, dseq, n_heads, HEAD_DIM))
        chunk_rows.append(vchunk.reshape(dbsz, dseq, -1))
    return (xp.reshape(bsz, seq, d_model), xs.reshape(dbsz, dseq, d_model),
            jnp.stack(kp_rows), jnp.stack(vp_rows), jnp.stack(ks_rows), jnp.stack(vs_rows),
            jnp.stack(chunk_rows))
```

```python
import functools

import jax
import jax.numpy as jnp
from jax import lax
from jax.experimental import pallas as pl
from jax.experimental.pallas import tpu as pltpu

F32 = jnp.float32
BF16 = jnp.bfloat16

EPS = 1e-6
NEG_INF = -1e30
A_GROUPS = 8
CHUNK = 128
HEAD_DIM = 64
MOBA_BLOCK = 256
MOBA_TOPK = 3
ROPE_THETA = 10000.0
N_GROUPS = 4
EXPERTS_PER_GROUP = 8
PAGE_SIZE = 128
PLE_LANES = 128
ROW_TILE = 256
MOE_BLOCK = 256
PAGES_PER_STEP = 8
VMEM_LIMIT = 56 * 1024 * 1024
LOG2_E = 1.4426950408889634
ONES_ROWS = 16
DMA_ISSUE_UNROLL = 8

_NT = (((1,), (1,)), ((), ()))


def _rms(x, g):
    return x * lax.rsqrt(jnp.mean(x * x, axis=-1, keepdims=True) + EPS) * g


def _dot(a, b):
    return jnp.dot(a, b, preferred_element_type=F32)


def _dot_nt(a, b):
    return lax.dot_general(a, b, _NT, preferred_element_type=F32)


def _const_spec(shape):
    return pl.BlockSpec(shape, lambda *_: (0,) * len(shape))


def _inproj_kernel(x_ref, gmix_ref, win_ref, gv_ref, wmix_ref, bs_ref, hind_ref, gq_ref, gk_ref,
                   cos_ref, sin_ref, wa_ref,
                   aterm_ref, sgb_ref, q_ref, k_ref, v_ref, kb_ref, vb_ref, *extra,
                   a_width, b_width, d_model, emit_kmean, emit_vchunk, transpose_qv):
    tm = x_ref.shape[0]
    h = _rms(x_ref[...], gmix_ref[...]).astype(BF16)
    offs = [0]

    def proj(width):
        o = offs[0]
        offs[0] = o + width
        return _dot(h, win_ref[:, o:o + width])

    zu = proj(a_width)
    zv = proj(a_width)
    zq = proj(b_width)
    zk = proj(b_width)
    zva = proj(b_width)

    u = jax.nn.gelu(zu)
    vn = _rms(jax.nn.gelu(zv), gv_ref[...])
    vb16 = vn.astype(BF16)
    lane_grp = lax.broadcasted_iota(jnp.int32, (CHUNK, a_width), 1) // (a_width // A_GROUPS)
    parts = []
    for c in range(tm // CHUNK):
        vc = vb16[c * CHUNK:(c + 1) * CHUNK, :]
        rhs = jnp.concatenate(
            [jnp.where(lane_grp == g, vc, jnp.zeros_like(vc)) for g in range(A_GROUPS)], axis=0)
        parts.append(_dot(wmix_ref[...], rhs) + bs_ref[...])
    s = parts[0] if len(parts) == 1 else jnp.concatenate(parts, axis=0)
    out_a = (u * s).astype(BF16)
    ga = proj(d_model)
    aterm_ref[...] = jax.nn.sigmoid(ga) * _dot(out_a, wa_ref[...])
    gb = proj(d_model)
    sgb_ref[...] = jax.nn.sigmoid(gb)

    lane = lax.broadcasted_iota(jnp.int32, (tm, b_width), 1)
    first_half = (lane % HEAD_DIM) < (HEAD_DIM // 2)
    cos = cos_ref[...]
    sin = sin_ref[...]

    def headnorm_rope(z, g):
        ms = _dot((z * z).astype(BF16), hind_ref[...])
        y = z * lax.rsqrt(ms + EPS) * g
        swapped = jnp.where(first_half,
                            pltpu.roll(y, b_width - HEAD_DIM // 2, 1),
                            pltpu.roll(y, HEAD_DIM // 2, 1))
        return y * cos + swapped * sin

    q = headnorm_rope(zq, gq_ref[...])
    k = headnorm_rope(zk, gk_ref[...])
    k_ref[...] = k
    kb_ref[...] = k.astype(BF16)
    v_ref[...] = zva
    if transpose_qv:
        q_ref[0] = q.T.astype(BF16)
        vb_ref[0] = zva.T.astype(BF16)
    else:
        q_ref[...] = q.astype(BF16)
        vb_ref[...] = zva.astype(BF16)
    idx = 0
    if emit_kmean:
        km_ref = extra[idx]
        idx += 1
        for bi in range(tm // MOBA_BLOCK):
            km_ref[bi] = jnp.mean(k[bi * MOBA_BLOCK:(bi + 1) * MOBA_BLOCK, :], axis=0, keepdims=True)
    if emit_vchunk:
        extra[idx][...] = vn


def _inproj(x, pos_tables, wts, *, emit_kmean, emit_vchunk, pos_blocks):
    n, d_model = x.shape
    a_width = wts['g_v'].shape[-1]
    b_width = wts['hind'].shape[0]
    tm = ROW_TILE
    cos_t, sin_t = pos_tables
    row = lambda i: (i, 0)
    posrow = lambda i: (i % pos_blocks, 0)
    in_specs = [
        pl.BlockSpec((tm, d_model), row),
        _const_spec((1, d_model)),
        _const_spec(wts['w_in'].shape),
        _const_spec((1, a_width)),
        _const_spec(wts['wmix'].shape),
        _const_spec(wts['bs_tab'].shape),
        _const_spec(wts['hind'].shape),
        _const_spec((1, b_width)),
        _const_spec((1, b_width)),
        pl.BlockSpec((tm, b_width), posrow),
        pl.BlockSpec((tm, b_width), posrow),
        _const_spec(wts['w_a'].shape),
    ]
    transpose_qv = emit_kmean
    if transpose_qv:
        assert tm == MOBA_BLOCK
        qv_shape = jax.ShapeDtypeStruct((n // tm, b_width, tm), BF16)
        qv_spec = pl.BlockSpec((1, b_width, tm), lambda i: (i, 0, 0))
    else:
        qv_shape = jax.ShapeDtypeStruct((n, b_width), BF16)
        qv_spec = pl.BlockSpec((tm, b_width), row)
    out_shape = [
        jax.ShapeDtypeStruct((n, d_model), F32),
        jax.ShapeDtypeStruct((n, d_model), F32),
        qv_shape,
        jax.ShapeDtypeStruct((n, b_width), F32),
        jax.ShapeDtypeStruct((n, b_width), F32),
        jax.ShapeDtypeStruct((n, b_width), BF16),
        qv_shape,
    ]
    out_specs = [
        pl.BlockSpec((tm, d_model), row), pl.BlockSpec((tm, d_model), row),
        qv_spec, pl.BlockSpec((tm, b_width), row),
        pl.BlockSpec((tm, b_width), row), pl.BlockSpec((tm, b_width), row),
        qv_spec,
    ]
    if emit_kmean:
        nbt = tm // MOBA_BLOCK
        out_shape.append(jax.ShapeDtypeStruct((n // MOBA_BLOCK, 1, b_width), F32))
        out_specs.append(pl.BlockSpec((nbt, 1, b_width), lambda i: (i, 0, 0)))
    if emit_vchunk:
        out_shape.append(jax.ShapeDtypeStruct((n, a_width), F32))
        out_specs.append(pl.BlockSpec((tm, a_width), row))
    kern = functools.partial(_inproj_kernel, a_width=a_width, b_width=b_width, d_model=d_model,
                             emit_kmean=emit_kmean, emit_vchunk=emit_vchunk, transpose_qv=transpose_qv)
    return pl.pallas_call(
        kern, grid=(n // tm,), in_specs=in_specs, out_specs=out_specs, out_shape=out_shape,
        name='inproj',
        compiler_params=pltpu.CompilerParams(dimension_semantics=('arbitrary',),
                                             vmem_limit_bytes=VMEM_LIMIT),
    )(x, wts['g_mix'], wts['w_in'], wts['g_v'], wts['wmix'], wts['bs_tab'], wts['hind'],
      wts['g_q'], wts['g_k'], cos_t, sin_t, wts['w_a'])


def _select_topk(scores, allowed, blk_f, nb):
    sel = jnp.zeros(scores.shape, jnp.bool_)
    for _ in range(MOBA_TOPK):
        cand = jnp.logical_and(allowed, jnp.logical_not(sel))
        scm = jnp.where(cand, scores, -jnp.inf)
        mx = jnp.max(scm, axis=-1, keepdims=True)
        is_max = jnp.logical_and(cand, scm == mx)
        first = jnp.min(jnp.where(is_max, blk_f, float(nb)), axis=-1, keepdims=True)
        sel = jnp.logical_or(sel, jnp.logical_and(is_max, blk_f == first))
    return sel


def _select_topk_rows(scores, allowed, blk_f, nb):
    sel = jnp.zeros(scores.shape, jnp.bool_)
    for _ in range(MOBA_TOPK):
        cand = jnp.logical_and(allowed, jnp.logical_not(sel))
        scm = jnp.where(cand, scores, -jnp.inf)
        mx = jnp.max(scm, axis=0, keepdims=True)
        is_max = jnp.logical_and(cand, scm == mx)
        first = jnp.min(jnp.where(is_max, blk_f, float(nb)), axis=0, keepdims=True)
        sel = jnp.logical_or(sel, jnp.logical_and(is_max, blk_f == first))
    return sel


def _moba_prompt_kernel(qt_ref, kb_ref, vt_ref, km_ref, o_ref, w_ref, bias_ref, m_ref, l_ref, acc_ref,
                        s_ref, *, n_heads, nb):
    tq = qt_ref.shape[2]
    blk = MOBA_BLOCK
    pair = 2 * HEAD_DIM
    i = pl.program_id(1)
    scale = HEAD_DIM ** -0.5 * LOG2_E
    km = km_ref[0].astype(BF16)
    blk_f = lax.broadcasted_iota(jnp.int32, (nb, tq), 0).astype(F32)
    allowed = blk_f < i.astype(F32)
    key_t = lax.broadcasted_iota(jnp.int32, (blk, tq), 0)
    qry_t = lax.broadcasted_iota(jnp.int32, (blk, tq), 1)
    causal = key_t <= qry_t
    zeros = jnp.zeros((HEAD_DIM, tq), BF16)
    ones = jnp.ones((ONES_ROWS, blk), BF16)

    for h in range(n_heads):
        qth = qt_ref[0, h * HEAD_DIM:(h + 1) * HEAD_DIM, :]
        sel = _select_topk_rows(_dot(km[:, h * HEAD_DIM:(h + 1) * HEAD_DIM], qth), allowed, blk_f, nb)
        bias_ref[h] = jnp.where(sel, 0.0, NEG_INF)
        qs = (qth.astype(F32) * scale).astype(BF16)
        col = jnp.concatenate([qs, zeros] if h % 2 == 0 else [zeros, qs], axis=0)
        w_ref[h // 2, :, (h % 2) * tq:(h % 2 + 1) * tq] = col

    def scores(j, hp):
        return _dot(kb_ref[j, :, hp * pair:(hp + 1) * pair], w_ref[hp])

    def weighted_values(j, h, p):
        vt1 = jnp.concatenate([vt_ref[j, h * HEAD_DIM:(h + 1) * HEAD_DIM, :], ones], axis=0)
        pv = _dot(vt1, p.astype(BF16))
        return pv[:HEAD_DIM, :], pv[HEAD_DIM:HEAD_DIM + 1, :]

    for hp in range(n_heads // 2):
        s2 = scores(i, hp)
        for h in (2 * hp, 2 * hp + 1):
            s = jnp.where(causal, s2[:, (h % 2) * tq:(h % 2 + 1) * tq], NEG_INF)
            m = jnp.max(s, axis=0, keepdims=True)
            pv, psum = weighted_values(i, h, jnp.exp2(s - m))
            m_ref[h:h + 1, :] = m
            l_ref[h:h + 1, :] = psum
            acc_ref[h * HEAD_DIM:(h + 1) * HEAD_DIM, :] = pv

    def stage_scores(j, slot):
        jc = jnp.minimum(j, nb - 1)
        for hp in range(n_heads // 2):
            s_ref[slot, hp] = scores(jc, hp)

    def consume(j, slot):
        jc = jnp.minimum(j, nb - 1)
        for h in range(n_heads):
            s = s_ref[slot, h // 2, :, (h % 2) * tq:(h % 2 + 1) * tq]
            bias = bias_ref[h, pl.ds(jc, 1), :]
            m = m_ref[h:h + 1, :]
            m_new = jnp.maximum(m, jnp.max(s, axis=0, keepdims=True) + bias)
            alpha = jnp.exp2(m - m_new)
            pv, psum = weighted_values(jc, h, jnp.exp2(s + (bias - m_new)))
            m_ref[h:h + 1, :] = m_new
            l_ref[h:h + 1, :] = alpha * l_ref[h:h + 1, :] + psum
            rows = slice(h * HEAD_DIM, (h + 1) * HEAD_DIM)
            acc_ref[rows, :] = alpha * acc_ref[rows, :] + pv

    @pl.when(i > 0)
    def _():
        stage_scores(0, 0)

        def body(t, carry):
            j = 2 * t
            stage_scores(j + 1, 1)
            consume(j, 0)
            stage_scores(j + 2, 0)
            consume(j + 1, 1)
            return carry

        lax.fori_loop(0, (i + 1) // 2, body, 0)
    outs = [acc_ref[h * HEAD_DIM:(h + 1) * HEAD_DIM, :] / l_ref[h:h + 1, :] for h in range(n_heads)]
    o_ref[...] = jnp.concatenate(outs, axis=0).T.astype(o_ref.dtype)


def _moba_prompt(qt, kb, vt, kmean, bsz, seq):
    _, width, tq = qt.shape
    n_heads = width // HEAD_DIM
    nb = seq // MOBA_BLOCK
    assert tq == MOBA_BLOCK and n_heads % 2 == 0
    kern = functools.partial(_moba_prompt_kernel, n_heads=n_heads, nb=nb)
    return pl.pallas_call(
        kern, grid=(bsz, nb),
        in_specs=[
            pl.BlockSpec((1, width, tq), lambda b, i: (b * nb + i, 0, 0)),
            pl.BlockSpec((nb, MOBA_BLOCK, width), lambda b, i: (b, 0, 0)),
            pl.BlockSpec((nb, width, MOBA_BLOCK), lambda b, i: (b, 0, 0)),
            pl.BlockSpec((1, nb, width), lambda b, i: (b, 0, 0)),
        ],
        out_specs=pl.BlockSpec((tq, width), lambda b, i: (b * nb + i, 0)),
        out_shape=jax.ShapeDtypeStruct((bsz * seq, width), BF16),
        scratch_shapes=[pltpu.VMEM((n_heads // 2, 2 * HEAD_DIM, 2 * tq), BF16),
                        pltpu.VMEM((n_heads, nb, tq), F32),
                        pltpu.VMEM((n_heads, tq), F32), pltpu.VMEM((n_heads, tq), F32),
                        pltpu.VMEM((width, tq), F32),
                        pltpu.VMEM((2, n_heads // 2, MOBA_BLOCK, 2 * tq), F32)],
        name='moba_prompt',
        compiler_params=pltpu.CompilerParams(dimension_semantics=('arbitrary', 'arbitrary'),
                                             vmem_limit_bytes=VMEM_LIMIT),
    )(qt, kb, vt, kmean)


def _moba_sample_kernel(pt_ref, q_ref, kn_ref, vn_ref, *rest, pg, n_heads, nb):
    del pt_ref
    k_refs = rest[:pg]
    v_refs = rest[pg:2 * pg]
    o_ref = rest[2 * pg]
    sc_ref, m_ref, l_ref, acc_ref = rest[2 * pg + 1:]
    ds, width = q_ref.shape
    r = n_heads * ds
    step = pl.program_id(1)
    scale = HEAD_DIM ** -0.5
    row_h = lax.broadcasted_iota(jnp.int32, (r, width), 0) // ds
    lane_h = lax.broadcasted_iota(jnp.int32, (r, width), 1) // HEAD_DIM
    hmask = row_h == lane_h
    q = q_ref[...]
    qt = jnp.concatenate([q] * n_heads, axis=0)
    qbd = jnp.where(hmask, qt, jnp.zeros_like(qt))
    lane_b = lax.broadcasted_iota(jnp.int32, (r, nb), 1)

    @pl.when(step == 0)
    def _():
        sc_ref[...] = jnp.zeros_like(sc_ref)
        m_ref[...] = jnp.zeros_like(m_ref)
        l_ref[...] = jnp.zeros_like(l_ref)

    pages_per_block = MOBA_BLOCK // PAGE_SIZE
    for c in range(pg // pages_per_block):
        kblk = jnp.concatenate([k_refs[pages_per_block * c + t][...] for t in range(pages_per_block)],
                               axis=0).astype(BF16)
        vblk = jnp.concatenate([v_refs[pages_per_block * c + t][...] for t in range(pages_per_block)],
                               axis=0).astype(BF16)
        sraw = _dot_nt(qbd, kblk)
        bscore = jnp.mean(sraw, axis=-1, keepdims=True)
        s = sraw * scale
        mj = jnp.max(s, axis=-1, keepdims=True)
        p = jnp.exp(s - mj)
        lj = jnp.sum(p, axis=-1, keepdims=True)
        jb = step * (pg // pages_per_block) + c
        col = lane_b == jb
        sc_ref[...] = jnp.where(col, bscore, sc_ref[...])
        m_ref[...] = jnp.where(col, mj, m_ref[...])
        l_ref[...] = jnp.where(col, lj, l_ref[...])
        acc_ref[jb] = _dot(p.astype(BF16), vblk)

    @pl.when(step == pl.num_programs(1) - 1)
    def _():
        blk_f = lane_b.astype(F32)
        sel = _select_topk(sc_ref[...], jnp.ones((r, nb), jnp.bool_), blk_f, nb)
        s_own = _dot_nt(qbd, kn_ref[...]) * scale
        key_t = lax.broadcasted_iota(jnp.int32, (r, ds), 1)
        qry_t = lax.broadcasted_iota(jnp.int32, (r, ds), 0) % ds
        s_own = jnp.where(key_t <= qry_t, s_own, NEG_INF)
        m_all = m_ref[...]
        m_tot = jnp.maximum(jnp.max(jnp.where(sel, m_all, NEG_INF), axis=-1, keepdims=True),
                            jnp.max(s_own, axis=-1, keepdims=True))
        w = jnp.where(sel, jnp.exp(m_all - m_tot), 0.0)
        p_own = jnp.exp(s_own - m_tot)
        den = jnp.sum(w * l_ref[...], axis=-1, keepdims=True) + jnp.sum(p_own, axis=-1, keepdims=True)
        num = _dot(p_own.astype(BF16), vn_ref[...])

        def body(j, num):
            wj = jnp.sum(jnp.where(lane_b == j, w, 0.0), axis=-1, keepdims=True)
            return num + wj * acc_ref[j]

        num = lax.fori_loop(0, nb, body, num)
        out = jnp.where(hmask, num / den, 0.0)
        o = out[0:ds, :]
        for h in range(1, n_heads):
            o = o + out[h * ds:(h + 1) * ds, :]
        o_ref[...] = o.astype(o_ref.dtype)


def _moba_sample(q, kb, vb, cache_k, cache_v, page_table, ds):
    n, width = q.shape
    dbsz, n_pages = page_table.shape
    n_heads = width // HEAD_DIM
    assert (n_pages * PAGE_SIZE) % MOBA_BLOCK == 0, "cached length must fill whole MoBA blocks"
    nb = n_pages * PAGE_SIZE // MOBA_BLOCK
    assert nb >= MOBA_TOPK
    pg = PAGES_PER_STEP
    assert n_pages % pg == 0 and ds % 8 == 0
    r = n_heads * ds
    ck = cache_k.reshape(cache_k.shape[0], PAGE_SIZE, width)
    cv = cache_v.reshape(cache_v.shape[0], PAGE_SIZE, width)

    def page_spec(t):
        return pl.BlockSpec((None, PAGE_SIZE, width), lambda b, s, pt: (pt[b, s * pg + t], 0, 0))

    tok = pl.BlockSpec((ds, width), lambda b, s, pt: (b, 0))
    grid_spec = pltpu.PrefetchScalarGridSpec(
        num_scalar_prefetch=1, grid=(dbsz, n_pages // pg),
        in_specs=[tok, tok, tok] + [page_spec(t) for t in range(pg)] + [page_spec(t) for t in range(pg)],
        out_specs=tok,
        scratch_shapes=[pltpu.VMEM((r, nb), F32), pltpu.VMEM((r, nb), F32), pltpu.VMEM((r, nb), F32),
                        pltpu.VMEM((nb, r, width), F32)],
    )
    kern = functools.partial(_moba_sample_kernel, pg=pg, n_heads=n_heads, nb=nb)
    return pl.pallas_call(
        kern, grid_spec=grid_spec, out_shape=jax.ShapeDtypeStruct((n, width), BF16),
        name='moba_sample',
        compiler_params=pltpu.CompilerParams(dimension_semantics=('arbitrary', 'arbitrary'),
                                             vmem_limit_bytes=VMEM_LIMIT),
    )(page_table, q, kb, vb, *([ck] * pg), *([cv] * pg))


def _pack_bf16_pairs(x):
    w = x.shape[1] // 2
    xb = x.astype(BF16).astype(F32)
    hi = lax.bitcast_convert_type(xb[:, :w], jnp.uint32) & jnp.uint32(0xFFFF0000)
    lo = lax.bitcast_convert_type(xb[:, w:], jnp.uint32) >> 16
    return hi | lo


def _unpack_bf16_pairs(u):
    hi = lax.bitcast_convert_type(u & jnp.uint32(0xFFFF0000), F32)
    lo = lax.bitcast_convert_type(u << 16, F32)
    return jnp.concatenate([hi, lo], axis=1).astype(BF16)


def _post_attn_kernel(x_ref, aterm_ref, sgb_ref, ob_ref, wb_ref, wo_ref, gffn_ref, wr_ref, br_ref,
                      x1_ref, hf_ref, route_ref):
    merged = aterm_ref[...] + sgb_ref[...] * _dot(ob_ref[...], wb_ref[...])
    x1 = x_ref[...] + _dot(merged.astype(BF16), wo_ref[...])
    x1_ref[...] = x1
    hf = _rms(x1, gffn_ref[...])
    hf_ref[...] = _pack_bf16_pairs(hf)
    logits = _dot(hf.astype(BF16), wr_ref[...]) + br_ref[...]
    tm, lanes = logits.shape
    lane = lax.broadcasted_iota(jnp.int32, (tm, lanes), 1).astype(F32)
    n_exp = N_GROUPS * EXPERTS_PER_GROUP
    gmask = lane < N_GROUPS
    gl = jnp.where(gmask, logits, -jnp.inf)
    gmax = jnp.max(gl, axis=-1, keepdims=True)
    grp = jnp.min(jnp.where(gl == gmax, lane, float(lanes)), axis=-1, keepdims=True)
    p_grp = 1.0 / jnp.sum(jnp.where(gmask, jnp.exp(gl - gmax), 0.0), axis=-1, keepdims=True)
    lo = N_GROUPS + grp * EXPERTS_PER_GROUP
    emask = jnp.logical_and(jnp.logical_and(lane >= lo, lane < lo + EXPERTS_PER_GROUP),
                            lane < N_GROUPS + n_exp)
    e1 = jnp.where(emask, logits, -jnp.inf)
    v1 = jnp.max(e1, axis=-1, keepdims=True)
    j1 = jnp.min(jnp.where(e1 == v1, lane, float(lanes)), axis=-1, keepdims=True)
    e2 = jnp.where(lane == j1, -jnp.inf, e1)
    v2 = jnp.max(e2, axis=-1, keepdims=True)
    j2 = jnp.min(jnp.where(e2 == v2, lane, float(lanes)), axis=-1, keepdims=True)
    t = jnp.exp(v2 - v1)
    p1 = 1.0 / (1.0 + t)
    p2 = t / (1.0 + t)
    rec = jnp.where(lane == 0, j1 - N_GROUPS, 0.0)
    rec = jnp.where(lane == 1, j2 - N_GROUPS, rec)
    rec = jnp.where(lane == 2, p_grp * p1, rec)
    rec = jnp.where(lane == 3, p_grp * p2, rec)
    route_ref[...] = rec


def _post_attn(x, aterm, sgb, out_b, wts):
    n, d_model = x.shape
    tm = ROW_TILE
    row = lambda i: (i, 0)
    return pl.pallas_call(
        _post_attn_kernel, grid=(n // tm,),
        in_specs=[pl.BlockSpec((tm, d_model), row), pl.BlockSpec((tm, d_model), row),
                  pl.BlockSpec((tm, d_model), row), pl.BlockSpec((tm, out_b.shape[1]), row),
                  _const_spec(wts['w_b'].shape), _const_spec(wts['w_o'].shape),
                  _const_spec((1, d_model)), _const_spec(wts['w_r'].shape), _const_spec((1, PLE_LANES))],
        out_specs=[pl.BlockSpec((tm, d_model), row), pl.BlockSpec((tm, d_model // 2), row),
                   pl.BlockSpec((tm, PLE_LANES), row)],
        out_shape=[jax.ShapeDtypeStruct((n, d_model), F32),
                   jax.ShapeDtypeStruct((n, d_model // 2), jnp.uint32),
                   jax.ShapeDtypeStruct((n, PLE_LANES), F32)],
        name='post_attn',
        compiler_params=pltpu.CompilerParams(dimension_semantics=('arbitrary',),
                                             vmem_limit_bytes=VMEM_LIMIT),
    )(x, aterm, sgb, out_b, wts['w_b'], wts['w_o'], wts['g_ffn'], wts['w_r'], wts['b_r'])


def _dispatch_kernel(dest_ref, hf_ref, xd_in_ref, xd_ref, sem, *, fanout):
    del xd_in_ref
    tm = hf_ref.shape[0]

    def issue(rr, carry):
        for kk in range(fanout):
            pltpu.make_async_copy(hf_ref.at[pl.ds(rr, 1)],
                                  xd_ref.at[pl.ds(dest_ref[0, 0, rr * fanout + kk], 1)], sem).start()
        return carry

    lax.fori_loop(0, tm, issue, 0, unroll=DMA_ISSUE_UNROLL)
    for kk in range(fanout):
        pltpu.make_async_copy(hf_ref, xd_ref.at[pl.ds(0, tm)], sem).wait()


def _dispatch(hf, dest, n_rows):
    n, w = hf.shape
    fanout = dest.shape[0] // n
    tm = min(n, 512)
    steps = n // tm
    dest3 = dest.reshape(steps, 1, tm * fanout)
    xd0 = jnp.zeros((n_rows, w), hf.dtype)
    kern = functools.partial(_dispatch_kernel, fanout=fanout)
    return pl.pallas_call(
        kern, grid=(steps,),
        in_specs=[pl.BlockSpec((1, 1, tm * fanout), lambda i: (i, 0, 0), memory_space=pltpu.SMEM),
                  pl.BlockSpec((tm, w), lambda i: (i, 0)), pl.BlockSpec(memory_space=pl.ANY)],
        out_specs=pl.BlockSpec(memory_space=pl.ANY),
        out_shape=jax.ShapeDtypeStruct((n_rows, w), hf.dtype),
        scratch_shapes=[pltpu.SemaphoreType.DMA(())],
        input_output_aliases={2: 0},
        name='dispatch',
        compiler_params=pltpu.CompilerParams(dimension_semantics=('arbitrary',)),
    )(dest3, hf, xd0)


def _moe_ffn_kernel(be_ref, nreal_ref, xd_ref, w1_ref, w3_ref, w2_ref, yd_ref):
    del be_ref
    i = pl.program_id(0)

    @pl.when(i < nreal_ref[0])
    def _():
        xb = _unpack_bf16_pairs(xd_ref[...])
        a = _dot(xb, w1_ref[...])
        b = _dot(xb, w3_ref[...])
        yd_ref[...] = _dot((jax.nn.silu(a) * b).astype(BF16), w2_ref[...])

    @pl.when(i >= nreal_ref[0])
    def _():
        yd_ref[...] = jnp.zeros_like(yd_ref)


def _moe_ffn(xd, blk_e, n_real, w1, w3, w2):
    n_rows, half = xd.shape
    d_model = 2 * half
    ff = w1.shape[-1]
    blk = MOE_BLOCK
    grid_spec = pltpu.PrefetchScalarGridSpec(
        num_scalar_prefetch=2, grid=(n_rows // blk,),
        in_specs=[pl.BlockSpec((blk, half), lambda i, be, nr: (i, 0)),
                  pl.BlockSpec((None, d_model, ff), lambda i, be, nr: (be[i], 0, 0)),
                  pl.BlockSpec((None, d_model, ff), lambda i, be, nr: (be[i], 0, 0)),
                  pl.BlockSpec((None, ff, d_model), lambda i, be, nr: (be[i], 0, 0))],
        out_specs=pl.BlockSpec((blk, d_model), lambda i, be, nr: (i, 0)),
    )
    return pl.pallas_call(
        _moe_ffn_kernel, grid_spec=grid_spec,
        out_shape=jax.ShapeDtypeStruct((n_rows, d_model), F32),
        name='moe_ffn',
        compiler_params=pltpu.CompilerParams(dimension_semantics=('arbitrary',),
                                             vmem_limit_bytes=VMEM_LIMIT),
    )(blk_e, n_real, xd, w1, w3, w2)


def _final_kernel(dest_ref, yd_ref, x1_ref, route_ref, p_ref, gple_ref, wpg_ref, wple_ref,
                  y_ref, ybuf, sem, *, fanout):
    tm = x1_ref.shape[0]

    def issue(rr, carry):
        for kk in range(fanout):
            pltpu.make_async_copy(yd_ref.at[pl.ds(dest_ref[0, 0, rr * fanout + kk], 1)],
                                  ybuf.at[kk, pl.ds(rr, 1)], sem).start()
        return carry

    lax.fori_loop(0, tm, issue, 0, unroll=DMA_ISSUE_UNROLL)
    for kk in range(fanout):
        pltpu.make_async_copy(yd_ref.at[pl.ds(0, tm)], ybuf.at[kk], sem).wait()

    route = route_ref[...]
    moe = route[:, 2:3] * ybuf[0]
    for kk in range(1, fanout):
        moe = moe + route[:, 2 + kk:3 + kk] * ybuf[kk]
    x2 = x1_ref[...] + moe
    gate = jax.nn.sigmoid(_dot(_rms(x2, gple_ref[...]).astype(BF16), wpg_ref[...]))
    y_ref[...] = x2 + gate * _dot(p_ref[...].astype(BF16), wple_ref[...])


def _final(x1, yd, dest, route, p, wts):
    n, d_model = x1.shape
    fanout = dest.shape[0] // n
    tm = ROW_TILE
    steps = n // tm
    dest3 = dest.reshape(steps, 1, tm * fanout)
    row = lambda i: (i, 0)
    kern = functools.partial(_final_kernel, fanout=fanout)
    return pl.pallas_call(
        kern, grid=(steps,),
        in_specs=[pl.BlockSpec((1, 1, tm * fanout), lambda i: (i, 0, 0), memory_space=pltpu.SMEM),
                  pl.BlockSpec(memory_space=pl.ANY),
                  pl.BlockSpec((tm, d_model), row), pl.BlockSpec((tm, PLE_LANES), row),
                  pl.BlockSpec((tm, p.shape[1]), row),
                  _const_spec((1, d_model)), _const_spec(wts['w_pg'].shape), _const_spec(wts['w_ple'].shape)],
        out_specs=pl.BlockSpec((tm, d_model), row),
        out_shape=jax.ShapeDtypeStruct((n, d_model), F32),
        scratch_shapes=[pltpu.VMEM((fanout, tm, d_model), F32), pltpu.SemaphoreType.DMA(())],
        name='final',
        compiler_params=pltpu.CompilerParams(dimension_semantics=('arbitrary',),
                                             vmem_limit_bytes=VMEM_LIMIT),
    )(dest3, yd, x1, route, p, wts['g_ple'], wts['w_pg'], wts['w_ple'])


def _routing_plan(eid, blk):
    n_exp = N_GROUPS * EXPERTS_PER_GROUP
    e = eid.reshape(-1)
    n_assign = e.shape[0]
    onehot = (e[:, None] == jnp.arange(n_exp, dtype=jnp.int32)[None, :]).astype(jnp.int32)
    counts = jnp.sum(onehot, axis=0)
    rank = jnp.take_along_axis(jnp.cumsum(onehot, axis=0), e[:, None], axis=1)[:, 0] - 1
    pcounts = ((counts + blk - 1) // blk) * blk
    pend = jnp.cumsum(pcounts)
    pstart = pend - pcounts
    dest = (pstart[e] + rank).astype(jnp.int32)
    n_blocks = -(-(n_assign + n_exp * (blk - 1)) // blk)
    blk_start = jnp.arange(n_blocks, dtype=jnp.int32) * blk
    blk_e = jnp.minimum(jnp.sum(pend[None, :] <= blk_start[:, None], axis=1), n_exp - 1).astype(jnp.int32)
    n_real = (pend[-1:] // blk).astype(jnp.int32)
    return dest, blk_e, n_real, n_blocks * blk


def _rope_tables(pos, n_heads):
    half = HEAD_DIM // 2
    inv = ROPE_THETA ** (-jnp.arange(half, dtype=F32) / half)
    ang = pos.astype(F32)[:, None] * inv[None, :]
    cos = jnp.cos(ang)
    sin = jnp.sin(ang)
    return (jnp.tile(jnp.concatenate([cos, cos], axis=-1), (1, n_heads)),
            jnp.tile(jnp.concatenate([-sin, sin], axis=-1), (1, n_heads)))


def _mix_tables(w_s_l, b_s_l, t_mix, a_width):
    reps = CHUNK // t_mix
    tri = jnp.tril(jnp.ones((t_mix, t_mix), F32))
    wt = w_s_l[:, :t_mix, :t_mix] * tri[None]
    eye = jnp.eye(reps, dtype=F32)
    wbig = jnp.einsum('ab,gts->gatbs', eye, wt).reshape(A_GROUPS, CHUNK, CHUNK)
    wmix = jnp.transpose(wbig, (1, 0, 2)).reshape(CHUNK, A_GROUPS * CHUNK)
    bs = jnp.tile(b_s_l[:, :t_mix], (1, reps))
    bs_tab = jnp.repeat(bs.T, a_width // A_GROUPS, axis=1)
    return dict(wmix=wmix.astype(BF16), bs_tab=bs_tab)


def _layer_weights(l, g_mix, w_in, g_v, g_q, g_k, w_a, w_b, w_o, g_ffn, w_rg, b_rg, w_re, b_re,
                   w1, w3, w2, g_ple, w_pg, w_ple):
    b_width = w_b.shape[1]
    n_heads = b_width // HEAD_DIM
    d_model = w_o.shape[-1]
    hid = jnp.arange(b_width) // HEAD_DIM
    hind = jnp.where(hid[:, None] == hid[None, :], 1.0 / HEAD_DIM, 0.0)
    n_exp = N_GROUPS * EXPERTS_PER_GROUP
    w_r = jnp.zeros((d_model, PLE_LANES), F32)
    w_r = w_r.at[:, :N_GROUPS].set(w_rg[l]).at[:, N_GROUPS:N_GROUPS + n_exp].set(w_re[l])
    b_r = jnp.zeros((1, PLE_LANES), F32)
    b_r = b_r.at[0, :N_GROUPS].set(b_rg[l]).at[0, N_GROUPS:N_GROUPS + n_exp].set(b_re[l])
    return dict(
        g_mix=g_mix[l][None], w_in=w_in[l].astype(BF16), g_v=g_v[l][None], hind=hind.astype(BF16),
        g_q=jnp.tile(g_q[l], n_heads)[None], g_k=jnp.tile(g_k[l], n_heads)[None],
        w_a=w_a[l].astype(BF16), w_b=w_b[l].astype(BF16), w_o=w_o[l].astype(BF16),
        g_ffn=g_ffn[l][None], w_r=w_r.astype(BF16), b_r=b_r,
        w1=w1[l].astype(BF16), w3=w3[l].astype(BF16), w2=w2[l].astype(BF16),
        g_ple=g_ple[l][None], w_pg=w_pg[l].astype(BF16), w_ple=w_ple[l].astype(BF16),
    )


def _finish(x, aterm, sgb, out_b, p, wts):
    x1, hf, route = _post_attn(x, aterm, sgb, out_b, wts)
    eid = route[:, :2].astype(jnp.int32)
    dest, blk_e, n_real, n_rows = _routing_plan(eid, MOE_BLOCK)
    xd = _dispatch(hf, dest, n_rows)
    yd = _moe_ffn(xd, blk_e, n_real, wts['w1'], wts['w3'], wts['w2'])
    return _final(x1, yd, dest, route, p, wts)


def kernel(x_prompt, x_sample, cache_k, cache_v, page_table, p_prompt, p_sample, g_mix, w_in, g_v, w_s, b_s, g_q, g_k, w_a, w_b, w_o, g_ffn, w_router_group, b_router_group, w_router_expert, b_router_expert, w1, w3, w2, g_ple, w_ple_gate, w_ple):
    bsz, seq, d_model = x_prompt.shape
    dbsz, dseq, _ = x_sample.shape
    depth = g_mix.shape[0]
    b_width = w_b.shape[1]
    n_heads = b_width // HEAD_DIM
    past_len = page_table.shape[1] * PAGE_SIZE
    assert seq % MOBA_BLOCK == 0 and (bsz * seq) % ROW_TILE == 0 and (dbsz * dseq) % ROW_TILE == 0
    assert CHUNK % dseq == 0 and ROW_TILE % CHUNK == 0 and ROW_TILE % MOBA_BLOCK == 0
    assert seq // MOBA_BLOCK >= MOBA_TOPK
    params = (g_mix, w_in, g_v, g_q, g_k, w_a, w_b, w_o, g_ffn, w_router_group, b_router_group,
              w_router_expert, b_router_expert, w1, w3, w2, g_ple, w_ple_gate, w_ple)
    a_width = g_v.shape[-1]
    tab_p = _rope_tables(jnp.arange(seq, dtype=jnp.int32), n_heads)
    pos_s = past_len + (jnp.arange(ROW_TILE, dtype=jnp.int32) % dseq)
    tab_s = _rope_tables(pos_s, n_heads)
    xp = x_prompt.reshape(bsz * seq, d_model)
    xs = x_sample.reshape(dbsz * dseq, d_model)
    kp_rows, vp_rows, ks_rows, vs_rows, chunk_rows = [], [], [], [], []
    for l in range(depth):
        wts = _layer_weights(l, *params)
        wts_p = dict(wts, **_mix_tables(w_s[l], b_s[l], CHUNK, a_width))
        wts_s = dict(wts, **_mix_tables(w_s[l], b_s[l], dseq, a_width))
        aterm, sgb, qt, k, v, kb, vt, kmean = _inproj(
            xp, tab_p, wts_p, emit_kmean=True, emit_vchunk=False, pos_blocks=seq // ROW_TILE)
        nbt = bsz * seq // MOBA_BLOCK
        out_b = _moba_prompt(qt, kb.reshape(nbt, MOBA_BLOCK, b_width), vt,
                             kmean.reshape(bsz, seq // MOBA_BLOCK, b_width), bsz, seq)
        xp = _finish(xp, aterm, sgb, out_b, p_prompt[l].reshape(bsz * seq, -1), wts_p)
        kp_rows.append(k.reshape(bsz, seq, n_heads, HEAD_DIM))
        vp_rows.append(v.reshape(bsz, seq, n_heads, HEAD_DIM))
        aterm, sgb, q, k, v, kb, vb, vchunk = _inproj(
            xs, tab_s, wts_s, emit_kmean=False, emit_vchunk=True, pos_blocks=1)
        out_b = _moba_sample(q, kb, vb, cache_k[l], cache_v[l], page_table, dseq)
        xs = _finish(xs, aterm, sgb, out_b, p_sample[l].reshape(dbsz * dseq, -1), wts_s)
        ks_rows.append(k.reshape(dbsz, dseq, n_heads, HEAD_DIM))
        vs_rows.append(v.reshape(dbsz, dseq, n_heads, HEAD_DIM))
        chunk_rows.append(vchunk.reshape(dbsz, dseq, -1))
    return (xp.reshape(bsz, seq, d_model), xs.reshape(dbsz, dseq, d_model),
            jnp.stack(kp_rows), jnp.stack(vp_rows), jnp.stack(ks_rows), jnp.stack(vs_rows),
            jnp.stack(chunk_rows))
```

```python
import functools

import jax
import jax.numpy as jnp
from jax import lax
from jax.experimental import pallas as pl
from jax.experimental.pallas import tpu as pltpu

F32 = jnp.float32
BF16 = jnp.bfloat16

EPS = 1e-6
NEG_INF = -1e30
A_GROUPS = 8
CHUNK = 128
HEAD_DIM = 64
MOBA_BLOCK = 256
MOBA_TOPK = 3
ROPE_THETA = 10000.0
N_GROUPS = 4
EXPERTS_PER_GROUP = 8
PAGE_SIZE = 128
PLE_LANES = 128
ROW_TILE = 256
MOE_BLOCK = 256
PAGES_PER_STEP = 8
VMEM_LIMIT = 56 * 1024 * 1024
LOG2_E = 1.4426950408889634
ONES_ROWS = 16
DMA_ISSUE_UNROLL = 8

_NT = (((1,), (1,)), ((), ()))


def _rms(x, g):
    return x * lax.rsqrt(jnp.mean(x * x, axis=-1, keepdims=True) + EPS) * g


def _dot(a, b):
    return jnp.dot(a, b, preferred_element_type=F32)


def _dot_nt(a, b):
    return lax.dot_general(a, b, _NT, preferred_element_type=F32)


def _dot_tn(a, b):
    return lax.dot_general(a, b, (((0,), (0,)), ((), ())), preferred_element_type=F32)


def _const_spec(shape):
    return pl.BlockSpec(shape, lambda *_: (0,) * len(shape))


def _inproj_kernel(x_ref, gmix_ref, win_ref, gv_ref, wmix_ref, bs_ref, hind_ref, gq_ref, gk_ref,
                   cos_ref, sin_ref, wa_ref,
                   aterm_ref, sgb_ref, q_ref, k_ref, v_ref, kb_ref, vb_ref, *extra,
                   a_width, b_width, d_model, emit_kmean, emit_vchunk, transpose_qv):
    tm = x_ref.shape[0]
    h = _rms(x_ref[...], gmix_ref[...]).astype(BF16)
    offs = [0]

    def proj(width):
        o = offs[0]
        offs[0] = o + width
        return _dot(h, win_ref[:, o:o + width])

    zu = proj(a_width)
    zv = proj(a_width)
    zq = proj(b_width)
    zk = proj(b_width)
    zva = proj(b_width)

    u = jax.nn.gelu(zu)
    vn = _rms(jax.nn.gelu(zv), gv_ref[...])
    vb16 = vn.astype(BF16)
    lane_grp = lax.broadcasted_iota(jnp.int32, (CHUNK, a_width), 1) // (a_width // A_GROUPS)
    parts = []
    for c in range(tm // CHUNK):
        vc = vb16[c * CHUNK:(c + 1) * CHUNK, :]
        rhs = jnp.concatenate(
            [jnp.where(lane_grp == g, vc, jnp.zeros_like(vc)) for g in range(A_GROUPS)], axis=0)
        parts.append(_dot(wmix_ref[...], rhs) + bs_ref[...])
    s = parts[0] if len(parts) == 1 else jnp.concatenate(parts, axis=0)
    out_a = (u * s).astype(BF16)
    ga = proj(d_model)
    aterm_ref[...] = jax.nn.sigmoid(ga) * _dot(out_a, wa_ref[...])
    gb = proj(d_model)
    sgb_ref[...] = jax.nn.sigmoid(gb)

    lane = lax.broadcasted_iota(jnp.int32, (tm, b_width), 1)
    first_half = (lane % HEAD_DIM) < (HEAD_DIM // 2)
    cos = cos_ref[...]
    sin = sin_ref[...]

    def headnorm_rope(z, g):
        ms = _dot((z * z).astype(BF16), hind_ref[...])
        y = z * lax.rsqrt(ms + EPS) * g
        swapped = jnp.where(first_half,
                            pltpu.roll(y, b_width - HEAD_DIM // 2, 1),
                            pltpu.roll(y, HEAD_DIM // 2, 1))
        return y * cos + swapped * sin

    q = headnorm_rope(zq, gq_ref[...])
    k = headnorm_rope(zk, gk_ref[...])
    k_ref[...] = k
    kb_ref[...] = k.astype(BF16)
    v_ref[...] = zva
    if transpose_qv:
        q_ref[0] = q.T.astype(BF16)
        vb_ref[0] = zva.T.astype(BF16)
    else:
        q_ref[...] = q.astype(BF16)
        vb_ref[...] = zva.astype(BF16)
    idx = 0
    if emit_kmean:
        km_ref = extra[idx]
        idx += 1
        for bi in range(tm // MOBA_BLOCK):
            km_ref[bi] = jnp.mean(k[bi * MOBA_BLOCK:(bi + 1) * MOBA_BLOCK, :], axis=0, keepdims=True)
    if emit_vchunk:
        extra[idx][...] = vn


def _inproj(x, pos_tables, wts, *, emit_kmean, emit_vchunk, pos_blocks):
    n, d_model = x.shape
    a_width = wts['g_v'].shape[-1]
    b_width = wts['hind'].shape[0]
    tm = ROW_TILE
    cos_t, sin_t = pos_tables
    row = lambda i: (i, 0)
    posrow = lambda i: (i % pos_blocks, 0)
    in_specs = [
        pl.BlockSpec((tm, d_model), row),
        _const_spec((1, d_model)),
        _const_spec(wts['w_in'].shape),
        _const_spec((1, a_width)),
        _const_spec(wts['wmix'].shape),
        _const_spec(wts['bs_tab'].shape),
        _const_spec(wts['hind'].shape),
        _const_spec((1, b_width)),
        _const_spec((1, b_width)),
        pl.BlockSpec((tm, b_width), posrow),
        pl.BlockSpec((tm, b_width), posrow),
        _const_spec(wts['w_a'].shape),
    ]
    transpose_qv = emit_kmean
    if transpose_qv:
        assert tm == MOBA_BLOCK
        qv_shape = jax.ShapeDtypeStruct((n // tm, b_width, tm), BF16)
        qv_spec = pl.BlockSpec((1, b_width, tm), lambda i: (i, 0, 0))
    else:
        qv_shape = jax.ShapeDtypeStruct((n, b_width), BF16)
        qv_spec = pl.BlockSpec((tm, b_width), row)
    out_shape = [
        jax.ShapeDtypeStruct((n, d_model), F32),
        jax.ShapeDtypeStruct((n, d_model), F32),
        qv_shape,
        jax.ShapeDtypeStruct((n, b_width), F32),
        jax.ShapeDtypeStruct((n, b_width), F32),
        jax.ShapeDtypeStruct((n, b_width), BF16),
        qv_shape,
    ]
    out_specs = [
        pl.BlockSpec((tm, d_model), row), pl.BlockSpec((tm, d_model), row),
        qv_spec, pl.BlockSpec((tm, b_width), row),
        pl.BlockSpec((tm, b_width), row), pl.BlockSpec((tm, b_width), row),
        qv_spec,
    ]
    if emit_kmean:
        nbt = tm // MOBA_BLOCK
        out_shape.append(jax.ShapeDtypeStruct((n // MOBA_BLOCK, 1, b_width), F32))
        out_specs.append(pl.BlockSpec((nbt, 1, b_width), lambda i: (i, 0, 0)))
    if emit_vchunk:
        out_shape.append(jax.ShapeDtypeStruct((n, a_width), F32))
        out_specs.append(pl.BlockSpec((tm, a_width), row))
    kern = functools.partial(_inproj_kernel, a_width=a_width, b_width=b_width, d_model=d_model,
                             emit_kmean=emit_kmean, emit_vchunk=emit_vchunk, transpose_qv=transpose_qv)
    return pl.pallas_call(
        kern, grid=(n // tm,), in_specs=in_specs, out_specs=out_specs, out_shape=out_shape,
        name='inproj',
        compiler_params=pltpu.CompilerParams(dimension_semantics=('arbitrary',),
                                             vmem_limit_bytes=VMEM_LIMIT),
    )(x, wts['g_mix'], wts['w_in'], wts['g_v'], wts['wmix'], wts['bs_tab'], wts['hind'],
      wts['g_q'], wts['g_k'], cos_t, sin_t, wts['w_a'])


def _select_topk(scores, allowed, blk_f, nb):
    sel = jnp.zeros(scores.shape, jnp.bool_)
    for _ in range(MOBA_TOPK):
        cand = jnp.logical_and(allowed, jnp.logical_not(sel))
        scm = jnp.where(cand, scores, -jnp.inf)
        mx = jnp.max(scm, axis=-1, keepdims=True)
        is_max = jnp.logical_and(cand, scm == mx)
        first = jnp.min(jnp.where(is_max, blk_f, float(nb)), axis=-1, keepdims=True)
        sel = jnp.logical_or(sel, jnp.logical_and(is_max, blk_f == first))
    return sel


def _select_topk_rows(scores, allowed, blk_f, nb):
    sel = jnp.zeros(scores.shape, jnp.bool_)
    for _ in range(MOBA_TOPK):
        cand = jnp.logical_and(allowed, jnp.logical_not(sel))
        scm = jnp.where(cand, scores, -jnp.inf)
        mx = jnp.max(scm, axis=0, keepdims=True)
        is_max = jnp.logical_and(cand, scm == mx)
        first = jnp.min(jnp.where(is_max, blk_f, float(nb)), axis=0, keepdims=True)
        sel = jnp.logical_or(sel, jnp.logical_and(is_max, blk_f == first))
    return sel


def _moba_prompt_kernel(qt_ref, kb_ref, vt_ref, km_ref, o_ref, w_ref, bias_ref, m_ref, l_ref, acc_ref,
                        s_ref, *, n_heads, nb):
    tq = qt_ref.shape[2]
    blk = MOBA_BLOCK
    pair = 2 * HEAD_DIM
    i = pl.program_id(1)
    scale = HEAD_DIM ** -0.5 * LOG2_E
    km = km_ref[0].astype(BF16)
    blk_f = lax.broadcasted_iota(jnp.int32, (nb, tq), 0).astype(F32)
    allowed = blk_f < i.astype(F32)
    key_t = lax.broadcasted_iota(jnp.int32, (blk, tq), 0)
    qry_t = lax.broadcasted_iota(jnp.int32, (blk, tq), 1)
    causal = key_t <= qry_t
    zeros = jnp.zeros((HEAD_DIM, tq), BF16)
    ones = jnp.ones((ONES_ROWS, blk), BF16)

    for h in range(n_heads):
        qth = qt_ref[0, h * HEAD_DIM:(h + 1) * HEAD_DIM, :]
        sel = _select_topk_rows(_dot(km[:, h * HEAD_DIM:(h + 1) * HEAD_DIM], qth), allowed, blk_f, nb)
        bias_ref[h] = jnp.where(sel, 0.0, NEG_INF)
        qs = (qth.astype(F32) * scale).astype(BF16)
        col = jnp.concatenate([qs, zeros] if h % 2 == 0 else [zeros, qs], axis=0)
        w_ref[h // 2, :, (h % 2) * tq:(h % 2 + 1) * tq] = col

    def scores(j, hp):
        return _dot(kb_ref[j, :, hp * pair:(hp + 1) * pair], w_ref[hp])

    def weighted_values(j, h, p):
        vt1 = jnp.concatenate([vt_ref[j, h * HEAD_DIM:(h + 1) * HEAD_DIM, :], ones], axis=0)
        pv = _dot(vt1, p.astype(BF16))
        return pv[:HEAD_DIM, :], pv[HEAD_DIM:HEAD_DIM + 1, :]

    for hp in range(n_heads // 2):
        s2 = scores(i, hp)
        for h in (2 * hp, 2 * hp + 1):
            s = jnp.where(causal, s2[:, (h % 2) * tq:(h % 2 + 1) * tq], NEG_INF)
            m = jnp.max(s, axis=0, keepdims=True)
            pv, psum = weighted_values(i, h, jnp.exp2(s - m))
            m_ref[h:h + 1, :] = m
            l_ref[h:h + 1, :] = psum
            acc_ref[h * HEAD_DIM:(h + 1) * HEAD_DIM, :] = pv

    def stage_scores(j, slot):
        jc = jnp.minimum(j, nb - 1)
        for hp in range(n_heads // 2):
            s_ref[slot, hp] = scores(jc, hp)

    def consume(j, slot):
        jc = jnp.minimum(j, nb - 1)
        for h in range(n_heads):
            s = s_ref[slot, h // 2, :, (h % 2) * tq:(h % 2 + 1) * tq]
            bias = bias_ref[h, pl.ds(jc, 1), :]
            m = m_ref[h:h + 1, :]
            m_new = jnp.maximum(m, jnp.max(s, axis=0, keepdims=True) + bias)
            alpha = jnp.exp2(m - m_new)
            pv, psum = weighted_values(jc, h, jnp.exp2(s + (bias - m_new)))
            m_ref[h:h + 1, :] = m_new
            l_ref[h:h + 1, :] = alpha * l_ref[h:h + 1, :] + psum
            rows = slice(h * HEAD_DIM, (h + 1) * HEAD_DIM)
            acc_ref[rows, :] = alpha * acc_ref[rows, :] + pv

    @pl.when(i > 0)
    def _():
        stage_scores(0, 0)

        def body(t, carry):
            j = 2 * t
            stage_scores(j + 1, 1)
            consume(j, 0)
            stage_scores(j + 2, 0)
            consume(j + 1, 1)
            return carry

        lax.fori_loop(0, (i + 1) // 2, body, 0)
    outs = [acc_ref[h * HEAD_DIM:(h + 1) * HEAD_DIM, :] / l_ref[h:h + 1, :] for h in range(n_heads)]
    o_ref[...] = jnp.concatenate(outs, axis=0).T.astype(o_ref.dtype)


def _moba_prompt(qt, kb, vt, kmean, bsz, seq):
    _, width, tq = qt.shape
    n_heads = width // HEAD_DIM
    nb = seq // MOBA_BLOCK
    assert tq == MOBA_BLOCK and n_heads % 2 == 0
    kern = functools.partial(_moba_prompt_kernel, n_heads=n_heads, nb=nb)
    return pl.pallas_call(
        kern, grid=(bsz, nb),
        in_specs=[
            pl.BlockSpec((1, width, tq), lambda b, i: (b * nb + i, 0, 0)),
            pl.BlockSpec((nb, MOBA_BLOCK, width), lambda b, i: (b, 0, 0)),
            pl.BlockSpec((nb, width, MOBA_BLOCK), lambda b, i: (b, 0, 0)),
            pl.BlockSpec((1, nb, width), lambda b, i: (b, 0, 0)),
        ],
        out_specs=pl.BlockSpec((tq, width), lambda b, i: (b * nb + i, 0)),
        out_shape=jax.ShapeDtypeStruct((bsz * seq, width), BF16),
        scratch_shapes=[pltpu.VMEM((n_heads // 2, 2 * HEAD_DIM, 2 * tq), BF16),
                        pltpu.VMEM((n_heads, nb, tq), F32),
                        pltpu.VMEM((n_heads, tq), F32), pltpu.VMEM((n_heads, tq), F32),
                        pltpu.VMEM((width, tq), F32),
                        pltpu.VMEM((2, n_heads // 2, MOBA_BLOCK, 2 * tq), F32)],
        name='moba_prompt',
        compiler_params=pltpu.CompilerParams(dimension_semantics=('arbitrary', 'arbitrary'),
                                             vmem_limit_bytes=VMEM_LIMIT),
    )(qt, kb, vt, kmean)


def _moba_sample_kernel(pt_ref, qa_ref, kn_ref, vn_ref, *rest, pg, n_heads, ds, nb):
    del pt_ref
    k_refs = rest[:pg]
    v_refs = rest[pg:2 * pg]
    o_ref = rest[2 * pg]
    bs_ref, m_ref, l_ref, acc_ref = rest[2 * pg + 1:]
    step = pl.program_id(1)
    r = n_heads * ds
    c2 = HEAD_DIM ** -0.5 * LOG2_E
    qa = qa_ref[0]
    keys = PAGE_SIZE * n_heads
    ppb = MOBA_BLOCK // PAGE_SIZE

    def head_match(n_keys):
        row_h = lax.broadcasted_iota(jnp.int32, (r, n_keys), 0) // ds
        key_h = lax.broadcasted_iota(jnp.int32, (r, n_keys), 1) % n_heads
        return row_h == key_h

    match = head_match(keys)
    bias = jnp.where(match, 0.0, NEG_INF)
    keep = jnp.where(match, 1.0, 0.0)
    lane_b = lax.broadcasted_iota(jnp.int32, (r, nb), 1)

    @pl.when(step == 0)
    def _():
        bs_ref[...] = jnp.zeros_like(bs_ref)
        m_ref[...] = jnp.zeros_like(m_ref)
        l_ref[...] = jnp.zeros_like(l_ref)

    for t in range(pg):
        xk = k_refs[t][...].reshape(keys, HEAD_DIM).astype(BF16)
        xv = v_refs[t][...].reshape(keys, HEAD_DIM).astype(BF16)
        st = _dot_nt(qa, xk)
        s = st * c2 + bias
        m = jnp.max(s, axis=-1, keepdims=True)
        p = jnp.exp2(s - m)
        jb = step * (pg // ppb) + t // ppb
        par = t % ppb
        col = lane_b == jb
        bs_ref[par] = jnp.where(col, jnp.sum(st * keep, axis=-1, keepdims=True), bs_ref[par])
        m_ref[par] = jnp.where(col, m, m_ref[par])
        l_ref[par] = jnp.where(col, jnp.sum(p, axis=-1, keepdims=True), l_ref[par])
        acc_ref[par, jb] = _dot(p.astype(BF16), xv)

    @pl.when(step == pl.num_programs(1) - 1)
    def _():
        n_own = ds * n_heads
        bscore = bs_ref[0]
        for par in range(1, ppb):
            bscore = bscore + bs_ref[par]
        sel = _select_topk(bscore * (1.0 / MOBA_BLOCK), jnp.ones((r, nb), jnp.bool_),
                           lane_b.astype(F32), nb)
        key_t = lax.broadcasted_iota(jnp.int32, (r, n_own), 1) // n_heads
        qry_t = lax.broadcasted_iota(jnp.int32, (r, n_own), 0) % ds
        ok = jnp.logical_and(head_match(n_own), key_t <= qry_t)
        s_own = jnp.where(ok, _dot_nt(qa, kn_ref[0]) * c2, NEG_INF)
        m_tot = jnp.max(s_own, axis=-1, keepdims=True)
        for par in range(ppb):
            m_tot = jnp.maximum(m_tot, jnp.max(jnp.where(sel, m_ref[par], NEG_INF), axis=-1, keepdims=True))
        p_own = jnp.exp2(s_own - m_tot)
        den = jnp.sum(p_own, axis=-1, keepdims=True)
        num = _dot(p_own.astype(BF16), vn_ref[0])
        ws = []
        for par in range(ppb):
            w = jnp.where(sel, jnp.exp2(m_ref[par] - m_tot), 0.0)
            den = den + jnp.sum(w * l_ref[par], axis=-1, keepdims=True)
            ws.append(w)

        def body(j, num):
            for par in range(ppb):
                wj = jnp.sum(jnp.where(lane_b == j, ws[par], 0.0), axis=-1, keepdims=True)
                num = num + wj * acc_ref[par, j]
            return num

        out = lax.fori_loop(0, nb, body, num) / den
        o_ref[...] = jnp.concatenate([out[h * ds:(h + 1) * ds, :] for h in range(n_heads)],
                                     axis=1).astype(o_ref.dtype)


def _moba_sample(q, kb, vb, cache_k, cache_v, page_table, ds):
    n, width = q.shape
    dbsz, n_pages = page_table.shape
    n_heads = width // HEAD_DIM
    assert (n_pages * PAGE_SIZE) % MOBA_BLOCK == 0, "cached length must fill whole MoBA blocks"
    nb = n_pages * PAGE_SIZE // MOBA_BLOCK
    assert nb >= MOBA_TOPK
    pg = PAGES_PER_STEP
    ppb = MOBA_BLOCK // PAGE_SIZE
    assert n_pages % pg == 0 and pg % ppb == 0 and ds % 8 == 0
    assert cache_k.shape[1:] == (PAGE_SIZE, n_heads, HEAD_DIM)
    r = n_heads * ds
    qa = q.reshape(dbsz, ds, n_heads, HEAD_DIM).transpose(0, 2, 1, 3).reshape(dbsz, r, HEAD_DIM)
    kn = kb.reshape(dbsz, ds * n_heads, HEAD_DIM)
    vn = vb.reshape(dbsz, ds * n_heads, HEAD_DIM)

    def page_spec(t):
        return pl.BlockSpec((None, PAGE_SIZE, n_heads, HEAD_DIM),
                            lambda b, s, pt: (pt[b, s * pg + t], 0, 0, 0))

    new_tok = pl.BlockSpec((1, ds * n_heads, HEAD_DIM), lambda b, s, pt: (b, 0, 0))
    grid_spec = pltpu.PrefetchScalarGridSpec(
        num_scalar_prefetch=1, grid=(dbsz, n_pages // pg),
        in_specs=[pl.BlockSpec((1, r, HEAD_DIM), lambda b, s, pt: (b, 0, 0)), new_tok, new_tok]
        + [page_spec(t) for t in range(pg)] + [page_spec(t) for t in range(pg)],
        out_specs=pl.BlockSpec((ds, width), lambda b, s, pt: (b, 0)),
        scratch_shapes=[pltpu.VMEM((ppb, r, nb), F32), pltpu.VMEM((ppb, r, nb), F32),
                        pltpu.VMEM((ppb, r, nb), F32), pltpu.VMEM((ppb, nb, r, HEAD_DIM), F32)],
    )
    kern = functools.partial(_moba_sample_kernel, pg=pg, n_heads=n_heads, ds=ds, nb=nb)
    return pl.pallas_call(
        kern, grid_spec=grid_spec, out_shape=jax.ShapeDtypeStruct((n, width), BF16),
        name='moba_sample',
        compiler_params=pltpu.CompilerParams(dimension_semantics=('arbitrary', 'arbitrary'),
                                             vmem_limit_bytes=VMEM_LIMIT),
    )(page_table, qa, kn, vn, *([cache_k] * pg), *([cache_v] * pg))


def _pack_bf16_pairs(x):
    w = x.shape[1] // 2
    xb = x.astype(BF16).astype(F32)
    hi = lax.bitcast_convert_type(xb[:, :w], jnp.uint32) & jnp.uint32(0xFFFF0000)
    lo = lax.bitcast_convert_type(xb[:, w:], jnp.uint32) >> 16
    return hi | lo


def _unpack_bf16_pairs(u):
    hi = lax.bitcast_convert_type(u & jnp.uint32(0xFFFF0000), F32)
    lo = lax.bitcast_convert_type(u << 16, F32)
    return jnp.concatenate([hi, lo], axis=1).astype(BF16)


def _post_attn_kernel(x_ref, aterm_ref, sgb_ref, ob_ref, wb_ref, wo_ref, gffn_ref, wr_ref, br_ref,
                      x1_ref, hf_ref, route_ref):
    merged = aterm_ref[...] + sgb_ref[...] * _dot(ob_ref[...], wb_ref[...])
    x1 = x_ref[...] + _dot(merged.astype(BF16), wo_ref[...])
    x1_ref[...] = x1
    hf = _rms(x1, gffn_ref[...])
    hf_ref[...] = _pack_bf16_pairs(hf)
    logits = _dot(hf.astype(BF16), wr_ref[...]) + br_ref[...]
    tm, lanes = logits.shape
    lane = lax.broadcasted_iota(jnp.int32, (tm, lanes), 1).astype(F32)
    n_exp = N_GROUPS * EXPERTS_PER_GROUP
    gmask = lane < N_GROUPS
    gl = jnp.where(gmask, logits, -jnp.inf)
    gmax = jnp.max(gl, axis=-1, keepdims=True)
    grp = jnp.min(jnp.where(gl == gmax, lane, float(lanes)), axis=-1, keepdims=True)
    p_grp = 1.0 / jnp.sum(jnp.where(gmask, jnp.exp(gl - gmax), 0.0), axis=-1, keepdims=True)
    lo = N_GROUPS + grp * EXPERTS_PER_GROUP
    emask = jnp.logical_and(jnp.logical_and(lane >= lo, lane < lo + EXPERTS_PER_GROUP),
                            lane < N_GROUPS + n_exp)
    e1 = jnp.where(emask, logits, -jnp.inf)
    v1 = jnp.max(e1, axis=-1, keepdims=True)
    j1 = jnp.min(jnp.where(e1 == v1, lane, float(lanes)), axis=-1, keepdims=True)
    e2 = jnp.where(lane == j1, -jnp.inf, e1)
    v2 = jnp.max(e2, axis=-1, keepdims=True)
    j2 = jnp.min(jnp.where(e2 == v2, lane, float(lanes)), axis=-1, keepdims=True)
    t = jnp.exp(v2 - v1)
    p1 = 1.0 / (1.0 + t)
    p2 = t / (1.0 + t)
    rec = jnp.where(lane == 0, j1 - N_GROUPS, 0.0)
    rec = jnp.where(lane == 1, j2 - N_GROUPS, rec)
    rec = jnp.where(lane == 2, p_grp * p1, rec)
    rec = jnp.where(lane == 3, p_grp * p2, rec)
    route_ref[...] = rec


def _post_attn(x, aterm, sgb, out_b, wts):
    n, d_model = x.shape
    tm = ROW_TILE
    row = lambda i: (i, 0)
    return pl.pallas_call(
        _post_attn_kernel, grid=(n // tm,),
        in_specs=[pl.BlockSpec((tm, d_model), row), pl.BlockSpec((tm, d_model), row),
                  pl.BlockSpec((tm, d_model), row), pl.BlockSpec((tm, out_b.shape[1]), row),
                  _const_spec(wts['w_b'].shape), _const_spec(wts['w_o'].shape),
                  _const_spec((1, d_model)), _const_spec(wts['w_r'].shape), _const_spec((1, PLE_LANES))],
        out_specs=[pl.BlockSpec((tm, d_model), row), pl.BlockSpec((tm, d_model // 2), row),
                   pl.BlockSpec((tm, PLE_LANES), row)],
        out_shape=[jax.ShapeDtypeStruct((n, d_model), F32),
                   jax.ShapeDtypeStruct((n, d_model // 2), jnp.uint32),
                   jax.ShapeDtypeStruct((n, PLE_LANES), F32)],
        name='post_attn',
        compiler_params=pltpu.CompilerParams(dimension_semantics=('arbitrary',),
                                             vmem_limit_bytes=VMEM_LIMIT),
    )(x, aterm, sgb, out_b, wts['w_b'], wts['w_o'], wts['g_ffn'], wts['w_r'], wts['b_r'])


def _dispatch_kernel(dest_ref, hf_ref, xd_in_ref, xd_ref, sem, *, fanout):
    del xd_in_ref
    tm = hf_ref.shape[0]

    def issue(rr, carry):
        for kk in range(fanout):
            pltpu.make_async_copy(hf_ref.at[pl.ds(rr, 1)],
                                  xd_ref.at[pl.ds(dest_ref[0, 0, rr * fanout + kk], 1)], sem).start()
        return carry

    lax.fori_loop(0, tm, issue, 0, unroll=DMA_ISSUE_UNROLL)
    for kk in range(fanout):
        pltpu.make_async_copy(hf_ref, xd_ref.at[pl.ds(0, tm)], sem).wait()


def _dispatch(hf, dest, n_rows):
    n, w = hf.shape
    fanout = dest.shape[0] // n
    tm = min(n, 512)
    steps = n // tm
    dest3 = dest.reshape(steps, 1, tm * fanout)
    xd0 = jnp.zeros((n_rows, w), hf.dtype)
    kern = functools.partial(_dispatch_kernel, fanout=fanout)
    return pl.pallas_call(
        kern, grid=(steps,),
        in_specs=[pl.BlockSpec((1, 1, tm * fanout), lambda i: (i, 0, 0), memory_space=pltpu.SMEM),
                  pl.BlockSpec((tm, w), lambda i: (i, 0)), pl.BlockSpec(memory_space=pl.ANY)],
        out_specs=pl.BlockSpec(memory_space=pl.ANY),
        out_shape=jax.ShapeDtypeStruct((n_rows, w), hf.dtype),
        scratch_shapes=[pltpu.SemaphoreType.DMA(())],
        input_output_aliases={2: 0},
        name='dispatch',
        compiler_params=pltpu.CompilerParams(dimension_semantics=('arbitrary',)),
    )(dest3, hf, xd0)


def _moe_ffn_kernel(be_ref, nreal_ref, xd_ref, w1_ref, w3_ref, w2_ref, yd_ref):
    del be_ref
    i = pl.program_id(0)

    @pl.when(i < nreal_ref[0])
    def _():
        xb = _unpack_bf16_pairs(xd_ref[...])
        a = _dot(xb, w1_ref[...])
        b = _dot(xb, w3_ref[...])
        yd_ref[...] = _dot((jax.nn.silu(a) * b).astype(BF16), w2_ref[...])

    @pl.when(i >= nreal_ref[0])
    def _():
        yd_ref[...] = jnp.zeros_like(yd_ref)


def _moe_ffn(xd, blk_e, n_real, w1, w3, w2):
    n_rows, half = xd.shape
    d_model = 2 * half
    ff = w1.shape[-1]
    blk = MOE_BLOCK
    grid_spec = pltpu.PrefetchScalarGridSpec(
        num_scalar_prefetch=2, grid=(n_rows // blk,),
        in_specs=[pl.BlockSpec((blk, half), lambda i, be, nr: (i, 0)),
                  pl.BlockSpec((None, d_model, ff), lambda i, be, nr: (be[i], 0, 0)),
                  pl.BlockSpec((None, d_model, ff), lambda i, be, nr: (be[i], 0, 0)),
                  pl.BlockSpec((None, ff, d_model), lambda i, be, nr: (be[i], 0, 0))],
        out_specs=pl.BlockSpec((blk, d_model), lambda i, be, nr: (i, 0)),
    )
    return pl.pallas_call(
        _moe_ffn_kernel, grid_spec=grid_spec,
        out_shape=jax.ShapeDtypeStruct((n_rows, d_model), F32),
        name='moe_ffn',
        compiler_params=pltpu.CompilerParams(dimension_semantics=('arbitrary',),
                                             vmem_limit_bytes=VMEM_LIMIT),
    )(blk_e, n_real, xd, w1, w3, w2)


def _final_kernel(dest_ref, yd_ref, x1_ref, route_ref, p_ref, gple_ref, wpg_ref, wple_ref,
                  y_ref, ybuf, sem, *, fanout):
    tm = x1_ref.shape[0]

    def issue(rr, carry):
        for kk in range(fanout):
            pltpu.make_async_copy(yd_ref.at[pl.ds(dest_ref[0, 0, rr * fanout + kk], 1)],
                                  ybuf.at[kk, pl.ds(rr, 1)], sem).start()
        return carry

    lax.fori_loop(0, tm, issue, 0, unroll=DMA_ISSUE_UNROLL)
    for kk in range(fanout):
        pltpu.make_async_copy(yd_ref.at[pl.ds(0, tm)], ybuf.at[kk], sem).wait()

    route = route_ref[...]
    moe = route[:, 2:3] * ybuf[0]
    for kk in range(1, fanout):
        moe = moe + route[:, 2 + kk:3 + kk] * ybuf[kk]
    x2 = x1_ref[...] + moe
    gate = jax.nn.sigmoid(_dot(_rms(x2, gple_ref[...]).astype(BF16), wpg_ref[...]))
    y_ref[...] = x2 + gate * _dot(p_ref[...].astype(BF16), wple_ref[...])


def _final(x1, yd, dest, route, p, wts):
    n, d_model = x1.shape
    fanout = dest.shape[0] // n
    tm = ROW_TILE
    steps = n // tm
    dest3 = dest.reshape(steps, 1, tm * fanout)
    row = lambda i: (i, 0)
    kern = functools.partial(_final_kernel, fanout=fanout)
    return pl.pallas_call(
        kern, grid=(steps,),
        in_specs=[pl.BlockSpec((1, 1, tm * fanout), lambda i: (i, 0, 0), memory_space=pltpu.SMEM),
                  pl.BlockSpec(memory_space=pl.ANY),
                  pl.BlockSpec((tm, d_model), row), pl.BlockSpec((tm, PLE_LANES), row),
                  pl.BlockSpec((tm, p.shape[1]), row),
                  _const_spec((1, d_model)), _const_spec(wts['w_pg'].shape), _const_spec(wts['w_ple'].shape)],
        out_specs=pl.BlockSpec((tm, d_model), row),
        out_shape=jax.ShapeDtypeStruct((n, d_model), F32),
        scratch_shapes=[pltpu.VMEM((fanout, tm, d_model), F32), pltpu.SemaphoreType.DMA(())],
        name='final',
        compiler_params=pltpu.CompilerParams(dimension_semantics=('arbitrary',),
                                             vmem_limit_bytes=VMEM_LIMIT),
    )(dest3, yd, x1, route, p, wts['g_ple'], wts['w_pg'], wts['w_ple'])


def _routing_plan(eid, blk):
    n_exp = N_GROUPS * EXPERTS_PER_GROUP
    e = eid.reshape(-1)
    n_assign = e.shape[0]
    onehot = (e[:, None] == jnp.arange(n_exp, dtype=jnp.int32)[None, :]).astype(jnp.int32)
    counts = jnp.sum(onehot, axis=0)
    rank = jnp.take_along_axis(jnp.cumsum(onehot, axis=0), e[:, None], axis=1)[:, 0] - 1
    pcounts = ((counts + blk - 1) // blk) * blk
    pend = jnp.cumsum(pcounts)
    pstart = pend - pcounts
    dest = (pstart[e] + rank).astype(jnp.int32)
    n_blocks = -(-(n_assign + n_exp * (blk - 1)) // blk)
    blk_start = jnp.arange(n_blocks, dtype=jnp.int32) * blk
    blk_e = jnp.minimum(jnp.sum(pend[None, :] <= blk_start[:, None], axis=1), n_exp - 1).astype(jnp.int32)
    n_real = (pend[-1:] // blk).astype(jnp.int32)
    return dest, blk_e, n_real, n_blocks * blk


def _rope_tables(pos, n_heads):
    half = HEAD_DIM // 2
    inv = ROPE_THETA ** (-jnp.arange(half, dtype=F32) / half)
    ang = pos.astype(F32)[:, None] * inv[None, :]
    cos = jnp.cos(ang)
    sin = jnp.sin(ang)
    return (jnp.tile(jnp.concatenate([cos, cos], axis=-1), (1, n_heads)),
            jnp.tile(jnp.concatenate([-sin, sin], axis=-1), (1, n_heads)))


def _mix_tables(w_s_l, b_s_l, t_mix, a_width):
    reps = CHUNK // t_mix
    tri = jnp.tril(jnp.ones((t_mix, t_mix), F32))
    wt = w_s_l[:, :t_mix, :t_mix] * tri[None]
    eye = jnp.eye(reps, dtype=F32)
    wbig = jnp.einsum('ab,gts->gatbs', eye, wt).reshape(A_GROUPS, CHUNK, CHUNK)
    wmix = jnp.transpose(wbig, (1, 0, 2)).reshape(CHUNK, A_GROUPS * CHUNK)
    bs = jnp.tile(b_s_l[:, :t_mix], (1, reps))
    bs_tab = jnp.repeat(bs.T, a_width // A_GROUPS, axis=1)
    return dict(wmix=wmix.astype(BF16), bs_tab=bs_tab)


def _layer_weights(l, g_mix, w_in, g_v, g_q, g_k, w_a, w_b, w_o, g_ffn, w_rg, b_rg, w_re, b_re,
                   w1, w3, w2, g_ple, w_pg, w_ple):
    b_width = w_b.shape[1]
    n_heads = b_width // HEAD_DIM
    d_model = w_o.shape[-1]
    hid = jnp.arange(b_width) // HEAD_DIM
    hind = jnp.where(hid[:, None] == hid[None, :], 1.0 / HEAD_DIM, 0.0)
    n_exp = N_GROUPS * EXPERTS_PER_GROUP
    w_r = jnp.zeros((d_model, PLE_LANES), F32)
    w_r = w_r.at[:, :N_GROUPS].set(w_rg[l]).at[:, N_GROUPS:N_GROUPS + n_exp].set(w_re[l])
    b_r = jnp.zeros((1, PLE_LANES), F32)
    b_r = b_r.at[0, :N_GROUPS].set(b_rg[l]).at[0, N_GROUPS:N_GROUPS + n_exp].set(b_re[l])
    return dict(
        g_mix=g_mix[l][None], w_in=w_in[l].astype(BF16), g_v=g_v[l][None], hind=hind.astype(BF16),
        g_q=jnp.tile(g_q[l], n_heads)[None], g_k=jnp.tile(g_k[l], n_heads)[None],
        w_a=w_a[l].astype(BF16), w_b=w_b[l].astype(BF16), w_o=w_o[l].astype(BF16),
        g_ffn=g_ffn[l][None], w_r=w_r.astype(BF16), b_r=b_r,
        w1=w1[l].astype(BF16), w3=w3[l].astype(BF16), w2=w2[l].astype(BF16),
        g_ple=g_ple[l][None], w_pg=w_pg[l].astype(BF16), w_ple=w_ple[l].astype(BF16),
    )


def _finish(x, aterm, sgb, out_b, p, wts):
    x1, hf, route = _post_attn(x, aterm, sgb, out_b, wts)
    eid = route[:, :2].astype(jnp.int32)
    dest, blk_e, n_real, n_rows = _routing_plan(eid, MOE_BLOCK)
    xd = _dispatch(hf, dest, n_rows)
    yd = _moe_ffn(xd, blk_e, n_real, wts['w1'], wts['w3'], wts['w2'])
    return _final(x1, yd, dest, route, p, wts)


def kernel(x_prompt, x_sample, cache_k, cache_v, page_table, p_prompt, p_sample, g_mix, w_in, g_v, w_s, b_s, g_q, g_k, w_a, w_b, w_o, g_ffn, w_router_group, b_router_group, w_router_expert, b_router_expert, w1, w3, w2, g_ple, w_ple_gate, w_ple):
    bsz, seq, d_model = x_prompt.shape
    dbsz, dseq, _ = x_sample.shape
    depth = g_mix.shape[0]
    b_width = w_b.shape[1]
    n_heads = b_width // HEAD_DIM
    past_len = page_table.shape[1] * PAGE_SIZE
    assert seq % MOBA_BLOCK == 0 and (bsz * seq) % ROW_TILE == 0 and (dbsz * dseq) % ROW_TILE == 0
    assert CHUNK % dseq == 0 and ROW_TILE % CHUNK == 0 and ROW_TILE % MOBA_BLOCK == 0
    assert seq // MOBA_BLOCK >= MOBA_TOPK
    params = (g_mix, w_in, g_v, g_q, g_k, w_a, w_b, w_o, g_ffn, w_router_group, b_router_group,
              w_router_expert, b_router_expert, w1, w3, w2, g_ple, w_ple_gate, w_ple)
    a_width = g_v.shape[-1]
    tab_p = _rope_tables(jnp.arange(seq, dtype=jnp.int32), n_heads)
    pos_s = past_len + (jnp.arange(ROW_TILE, dtype=jnp.int32) % dseq)
    tab_s = _rope_tables(pos_s, n_heads)
    xp = x_prompt.reshape(bsz * seq, d_model)
    xs = x_sample.reshape(dbsz * dseq, d_model)
    kp_rows, vp_rows, ks_rows, vs_rows, chunk_rows = [], [], [], [], []
    for l in range(depth):
        wts = _layer_weights(l, *params)
        wts_p = dict(wts, **_mix_tables(w_s[l], b_s[l], CHUNK, a_width))
        wts_s = dict(wts, **_mix_tables(w_s[l], b_s[l], dseq, a_width))
        aterm, sgb, qt, k, v, kb, vt, kmean = _inproj(
            xp, tab_p, wts_p, emit_kmean=True, emit_vchunk=False, pos_blocks=seq // ROW_TILE)
        nbt = bsz * seq // MOBA_BLOCK
        out_b = _moba_prompt(qt, kb.reshape(nbt, MOBA_BLOCK, b_width), vt,
                             kmean.reshape(bsz, seq // MOBA_BLOCK, b_width), bsz, seq)
        xp = _finish(xp, aterm, sgb, out_b, p_prompt[l].reshape(bsz * seq, -1), wts_p)
        kp_rows.append(k.reshape(bsz, seq, n_heads, HEAD_DIM))
        vp_rows.append(v.reshape(bsz, seq, n_heads, HEAD_DIM))
        aterm, sgb, q, k, v, kb, vb, vchunk = _inproj(
            xs, tab_s, wts_s, emit_kmean=False, emit_vchunk=True, pos_blocks=1)
        out_b = _moba_sample(q, kb, vb, cache_k[l], cache_v[l], page_table, dseq)
        xs = _finish(xs, aterm, sgb, out_b, p_sample[l].reshape(dbsz * dseq, -1), wts_s)
        ks_rows.append(k.reshape(dbsz, dseq, n_heads, HEAD_DIM))
        vs_rows.append(v.reshape(dbsz, dseq, n_heads, HEAD_DIM))
        chunk_rows.append(vchunk.reshape(dbsz, dseq, -1))
    return (xp.reshape(bsz, seq, d_model), xs.reshape(dbsz, dseq, d_model),
            jnp.stack(kp_rows), jnp.stack(vp_rows), jnp.stack(ks_rows), jnp.stack(vs_rows),
            jnp.stack(chunk_rows))
```

```python
import functools

import jax
import jax.numpy as jnp
from jax import lax
from jax.experimental import pallas as pl
from jax.experimental.pallas import tpu as pltpu

F32 = jnp.float32
BF16 = jnp.bfloat16

EPS = 1e-6
NEG_INF = -1e30
A_GROUPS = 8
CHUNK = 128
HEAD_DIM = 64
MOBA_BLOCK = 256
MOBA_TOPK = 3
ROPE_THETA = 10000.0
N_GROUPS = 4
EXPERTS_PER_GROUP = 8
PAGE_SIZE = 128
PLE_LANES = 128
ROW_TILE = 256
MOE_BLOCK = 256
PAGES_PER_STEP = 8
VMEM_LIMIT = 56 * 1024 * 1024
LOG2_E = 1.4426950408889634
ONES_ROWS = 16
DMA_ISSUE_UNROLL = 8

_NT = (((1,), (1,)), ((), ()))


def _rms(x, g):
    return x * lax.rsqrt(jnp.mean(x * x, axis=-1, keepdims=True) + EPS) * g


def _dot(a, b):
    return jnp.dot(a, b, preferred_element_type=F32)


def _dot_nt(a, b):
    return lax.dot_general(a, b, _NT, preferred_element_type=F32)


def _dot_tn(a, b):
    return lax.dot_general(a, b, (((0,), (0,)), ((), ())), preferred_element_type=F32)


def _const_spec(shape):
    return pl.BlockSpec(shape, lambda *_: (0,) * len(shape))


def _inproj_kernel(x_ref, gmix_ref, win_ref, gv_ref, wmix_ref, bs_ref, hind_ref, gq_ref, gk_ref,
                   cos_ref, sin_ref, wa_ref,
                   aterm_ref, sgb_ref, q_ref, k_ref, v_ref, kb_ref, vb_ref, *extra,
                   a_width, b_width, d_model, emit_kmean, emit_vchunk, transpose_qv):
    tm = x_ref.shape[0]
    h = _rms(x_ref[...], gmix_ref[...]).astype(BF16)
    offs = [0]

    def proj(width):
        o = offs[0]
        offs[0] = o + width
        return _dot(h, win_ref[:, o:o + width])

    zu = proj(a_width)
    zv = proj(a_width)
    zq = proj(b_width)
    zk = proj(b_width)
    zva = proj(b_width)

    u = jax.nn.gelu(zu)
    vn = _rms(jax.nn.gelu(zv), gv_ref[...])
    vb16 = vn.astype(BF16)
    lane_grp = lax.broadcasted_iota(jnp.int32, (CHUNK, a_width), 1) // (a_width // A_GROUPS)
    parts = []
    for c in range(tm // CHUNK):
        vc = vb16[c * CHUNK:(c + 1) * CHUNK, :]
        rhs = jnp.concatenate(
            [jnp.where(lane_grp == g, vc, jnp.zeros_like(vc)) for g in range(A_GROUPS)], axis=0)
        parts.append(_dot(wmix_ref[...], rhs) + bs_ref[...])
    s = parts[0] if len(parts) == 1 else jnp.concatenate(parts, axis=0)
    out_a = (u * s).astype(BF16)
    ga = proj(d_model)
    aterm_ref[...] = jax.nn.sigmoid(ga) * _dot(out_a, wa_ref[...])
    gb = proj(d_model)
    sgb_ref[...] = jax.nn.sigmoid(gb)

    lane = lax.broadcasted_iota(jnp.int32, (tm, b_width), 1)
    first_half = (lane % HEAD_DIM) < (HEAD_DIM // 2)
    cos = cos_ref[...]
    sin = sin_ref[...]

    def headnorm_rope(z, g):
        ms = _dot((z * z).astype(BF16), hind_ref[...])
        y = z * lax.rsqrt(ms + EPS) * g
        swapped = jnp.where(first_half,
                            pltpu.roll(y, b_width - HEAD_DIM // 2, 1),
                            pltpu.roll(y, HEAD_DIM // 2, 1))
        return y * cos + swapped * sin

    q = headnorm_rope(zq, gq_ref[...])
    k = headnorm_rope(zk, gk_ref[...])
    kb_ref[...] = k.astype(BF16)
    if transpose_qv:
        vt = zva.T
        q_ref[0] = q.T.astype(BF16)
        k_ref[0] = k.T
        v_ref[0] = vt
        vb_ref[0] = vt.astype(BF16)
    else:
        q_ref[...] = q.astype(BF16)
        k_ref[...] = k
        v_ref[...] = zva
        vb_ref[...] = zva.astype(BF16)
    idx = 0
    if emit_kmean:
        km_ref = extra[idx]
        idx += 1
        for bi in range(tm // MOBA_BLOCK):
            km_ref[bi] = jnp.mean(k[bi * MOBA_BLOCK:(bi + 1) * MOBA_BLOCK, :], axis=0, keepdims=True)
    if emit_vchunk:
        extra[idx][...] = vn


def _inproj(x, pos_tables, wts, *, emit_kmean, emit_vchunk, pos_blocks):
    n, d_model = x.shape
    a_width = wts['g_v'].shape[-1]
    b_width = wts['hind'].shape[0]
    tm = ROW_TILE
    cos_t, sin_t = pos_tables
    row = lambda i: (i, 0)
    posrow = lambda i: (i % pos_blocks, 0)
    in_specs = [
        pl.BlockSpec((tm, d_model), row),
        _const_spec((1, d_model)),
        _const_spec(wts['w_in'].shape),
        _const_spec((1, a_width)),
        _const_spec(wts['wmix'].shape),
        _const_spec(wts['bs_tab'].shape),
        _const_spec(wts['hind'].shape),
        _const_spec((1, b_width)),
        _const_spec((1, b_width)),
        pl.BlockSpec((tm, b_width), posrow),
        pl.BlockSpec((tm, b_width), posrow),
        _const_spec(wts['w_a'].shape),
    ]
    transpose_qv = emit_kmean
    if transpose_qv:
        assert tm == MOBA_BLOCK
        qv_shape = jax.ShapeDtypeStruct((n // tm, b_width, tm), BF16)
        qv_spec = pl.BlockSpec((1, b_width, tm), lambda i: (i, 0, 0))
        kv_shape = jax.ShapeDtypeStruct((n // (tm * pos_blocks), b_width, tm * pos_blocks), F32)
        kv_spec = pl.BlockSpec((1, b_width, tm), lambda i: (i // pos_blocks, 0, i % pos_blocks))
    else:
        qv_shape = jax.ShapeDtypeStruct((n, b_width), BF16)
        qv_spec = pl.BlockSpec((tm, b_width), row)
        kv_shape = jax.ShapeDtypeStruct((n, b_width), F32)
        kv_spec = pl.BlockSpec((tm, b_width), row)
    out_shape = [
        jax.ShapeDtypeStruct((n, d_model), F32),
        jax.ShapeDtypeStruct((n, d_model), F32),
        qv_shape,
        kv_shape,
        kv_shape,
        jax.ShapeDtypeStruct((n, b_width), BF16),
        qv_shape,
    ]
    out_specs = [
        pl.BlockSpec((tm, d_model), row), pl.BlockSpec((tm, d_model), row),
        qv_spec, kv_spec, kv_spec, pl.BlockSpec((tm, b_width), row),
        qv_spec,
    ]
    if emit_kmean:
        nbt = tm // MOBA_BLOCK
        out_shape.append(jax.ShapeDtypeStruct((n // MOBA_BLOCK, 1, b_width), F32))
        out_specs.append(pl.BlockSpec((nbt, 1, b_width), lambda i: (i, 0, 0)))
    if emit_vchunk:
        out_shape.append(jax.ShapeDtypeStruct((n, a_width), F32))
        out_specs.append(pl.BlockSpec((tm, a_width), row))
    kern = functools.partial(_inproj_kernel, a_width=a_width, b_width=b_width, d_model=d_model,
                             emit_kmean=emit_kmean, emit_vchunk=emit_vchunk, transpose_qv=transpose_qv)
    return pl.pallas_call(
        kern, grid=(n // tm,), in_specs=in_specs, out_specs=out_specs, out_shape=out_shape,
        name='inproj',
        compiler_params=pltpu.CompilerParams(dimension_semantics=('arbitrary',),
                                             vmem_limit_bytes=VMEM_LIMIT),
    )(x, wts['g_mix'], wts['w_in'], wts['g_v'], wts['wmix'], wts['bs_tab'], wts['hind'],
      wts['g_q'], wts['g_k'], cos_t, sin_t, wts['w_a'])


def _select_topk(scores, allowed, blk_f, nb):
    sel = jnp.zeros(scores.shape, jnp.bool_)
    for _ in range(MOBA_TOPK):
        cand = jnp.logical_and(allowed, jnp.logical_not(sel))
        scm = jnp.where(cand, scores, -jnp.inf)
        mx = jnp.max(scm, axis=-1, keepdims=True)
        is_max = jnp.logical_and(cand, scm == mx)
        first = jnp.min(jnp.where(is_max, blk_f, float(nb)), axis=-1, keepdims=True)
        sel = jnp.logical_or(sel, jnp.logical_and(is_max, blk_f == first))
    return sel


def _select_topk_rows(scores, allowed, blk_f, nb):
    sel = jnp.zeros(scores.shape, jnp.bool_)
    for _ in range(MOBA_TOPK):
        cand = jnp.logical_and(allowed, jnp.logical_not(sel))
        scm = jnp.where(cand, scores, -jnp.inf)
        mx = jnp.max(scm, axis=0, keepdims=True)
        is_max = jnp.logical_and(cand, scm == mx)
        first = jnp.min(jnp.where(is_max, blk_f, float(nb)), axis=0, keepdims=True)
        sel = jnp.logical_or(sel, jnp.logical_and(is_max, blk_f == first))
    return sel


def _moba_prompt_kernel(qt_ref, kb_ref, vt_ref, km_ref, o_ref, w_ref, bias_ref, m_ref, l_ref, acc_ref,
                        s_ref, *, n_heads, nb):
    tq = qt_ref.shape[2]
    blk = MOBA_BLOCK
    pair = 2 * HEAD_DIM
    i = pl.program_id(1)
    scale = HEAD_DIM ** -0.5 * LOG2_E
    km = km_ref[0].astype(BF16)
    blk_f = lax.broadcasted_iota(jnp.int32, (nb, tq), 0).astype(F32)
    allowed = blk_f < i.astype(F32)
    key_t = lax.broadcasted_iota(jnp.int32, (blk, tq), 0)
    qry_t = lax.broadcasted_iota(jnp.int32, (blk, tq), 1)
    causal = key_t <= qry_t
    zeros = jnp.zeros((HEAD_DIM, tq), BF16)
    ones = jnp.ones((ONES_ROWS, blk), BF16)

    for h in range(n_heads):
        qth = qt_ref[0, h * HEAD_DIM:(h + 1) * HEAD_DIM, :]
        sel = _select_topk_rows(_dot(km[:, h * HEAD_DIM:(h + 1) * HEAD_DIM], qth), allowed, blk_f, nb)
        bias_ref[h] = jnp.where(sel, 0.0, NEG_INF)
        qs = (qth.astype(F32) * scale).astype(BF16)
        col = jnp.concatenate([qs, zeros] if h % 2 == 0 else [zeros, qs], axis=0)
        w_ref[h // 2, :, (h % 2) * tq:(h % 2 + 1) * tq] = col

    def scores(j, hp):
        return _dot(kb_ref[j, :, hp * pair:(hp + 1) * pair], w_ref[hp])

    def weighted_values(j, h, p):
        vt1 = jnp.concatenate([vt_ref[j, h * HEAD_DIM:(h + 1) * HEAD_DIM, :], ones], axis=0)
        pv = _dot(vt1, p.astype(BF16))
        return pv[:HEAD_DIM, :], pv[HEAD_DIM:HEAD_DIM + 1, :]

    for hp in range(n_heads // 2):
        s2 = scores(i, hp)
        for h in (2 * hp, 2 * hp + 1):
            s = jnp.where(causal, s2[:, (h % 2) * tq:(h % 2 + 1) * tq], NEG_INF)
            m = jnp.max(s, axis=0, keepdims=True)
            pv, psum = weighted_values(i, h, jnp.exp2(s - m))
            m_ref[h:h + 1, :] = m
            l_ref[h:h + 1, :] = psum
            acc_ref[h * HEAD_DIM:(h + 1) * HEAD_DIM, :] = pv

    def stage_scores(j, slot):
        jc = jnp.minimum(j, nb - 1)
        for hp in range(n_heads // 2):
            s_ref[slot, hp] = scores(jc, hp)

    def consume(j, slot):
        jc = jnp.minimum(j, nb - 1)
        for h in range(n_heads):
            s = s_ref[slot, h // 2, :, (h % 2) * tq:(h % 2 + 1) * tq]
            bias = bias_ref[h, pl.ds(jc, 1), :]
            m = m_ref[h:h + 1, :]
            m_new = jnp.maximum(m, jnp.max(s, axis=0, keepdims=True) + bias)
            alpha = jnp.exp2(m - m_new)
            pv, psum = weighted_values(jc, h, jnp.exp2(s + (bias - m_new)))
            m_ref[h:h + 1, :] = m_new
            l_ref[h:h + 1, :] = alpha * l_ref[h:h + 1, :] + psum
            rows = slice(h * HEAD_DIM, (h + 1) * HEAD_DIM)
            acc_ref[rows, :] = alpha * acc_ref[rows, :] + pv

    @pl.when(i > 0)
    def _():
        stage_scores(0, 0)

        def body(t, carry):
            j = 2 * t
            stage_scores(j + 1, 1)
            consume(j, 0)
            stage_scores(j + 2, 0)
            consume(j + 1, 1)
            return carry

        lax.fori_loop(0, (i + 1) // 2, body, 0)
    outs = [acc_ref[h * HEAD_DIM:(h + 1) * HEAD_DIM, :] / l_ref[h:h + 1, :] for h in range(n_heads)]
    o_ref[...] = jnp.concatenate(outs, axis=0).T.astype(o_ref.dtype)


def _moba_prompt(qt, kb, vt, kmean, bsz, seq):
    _, width, tq = qt.shape
    n_heads = width // HEAD_DIM
    nb = seq // MOBA_BLOCK
    assert tq == MOBA_BLOCK and n_heads % 2 == 0
    kern = functools.partial(_moba_prompt_kernel, n_heads=n_heads, nb=nb)
    return pl.pallas_call(
        kern, grid=(bsz, nb),
        in_specs=[
            pl.BlockSpec((1, width, tq), lambda b, i: (b * nb + i, 0, 0)),
            pl.BlockSpec((nb, MOBA_BLOCK, width), lambda b, i: (b, 0, 0)),
            pl.BlockSpec((nb, width, MOBA_BLOCK), lambda b, i: (b, 0, 0)),
            pl.BlockSpec((1, nb, width), lambda b, i: (b, 0, 0)),
        ],
        out_specs=pl.BlockSpec((tq, width), lambda b, i: (b * nb + i, 0)),
        out_shape=jax.ShapeDtypeStruct((bsz * seq, width), BF16),
        scratch_shapes=[pltpu.VMEM((n_heads // 2, 2 * HEAD_DIM, 2 * tq), BF16),
                        pltpu.VMEM((n_heads, nb, tq), F32),
                        pltpu.VMEM((n_heads, tq), F32), pltpu.VMEM((n_heads, tq), F32),
                        pltpu.VMEM((width, tq), F32),
                        pltpu.VMEM((2, n_heads // 2, MOBA_BLOCK, 2 * tq), F32)],
        name='moba_prompt',
        compiler_params=pltpu.CompilerParams(dimension_semantics=('arbitrary', 'arbitrary'),
                                             vmem_limit_bytes=VMEM_LIMIT),
    )(qt, kb, vt, kmean)


def _moba_sample_kernel(pt_ref, q_ref, kn_ref, vn_ref, *rest, pg, n_heads, nb):
    del pt_ref
    k_refs = rest[:pg]
    v_refs = rest[pg:2 * pg]
    o_ref = rest[2 * pg]
    bs_ref, m_ref, l_ref, acc_ref = rest[2 * pg + 1:]
    ds, width = q_ref.shape
    step = pl.program_id(1)
    r = n_heads * ds
    c2 = HEAD_DIM ** -0.5 * LOG2_E
    ppb = MOBA_BLOCK // PAGE_SIZE
    row_h = lax.broadcasted_iota(jnp.int32, (r, width), 0) // ds
    lane_h = lax.broadcasted_iota(jnp.int32, (r, width), 1) // HEAD_DIM
    hmask = row_h == lane_h
    qt = jnp.concatenate([q_ref[...]] * n_heads, axis=0)
    qbd = jnp.where(hmask, qt, jnp.zeros_like(qt))
    lane_b = lax.broadcasted_iota(jnp.int32, (r, nb), 1)

    @pl.when(step == 0)
    def _():
        bs_ref[...] = jnp.zeros_like(bs_ref)
        m_ref[...] = jnp.zeros_like(m_ref)
        l_ref[...] = jnp.zeros_like(l_ref)

    for c in range(pg // ppb):
        kblk = jnp.concatenate([k_refs[ppb * c + t][...] for t in range(ppb)], axis=1).astype(BF16)
        vblk = jnp.concatenate([v_refs[ppb * c + t][...] for t in range(ppb)], axis=1).astype(BF16)
        st = _dot(qbd, kblk)
        s = st * c2
        m = jnp.max(s, axis=-1, keepdims=True)
        p = jnp.exp2(s - m)
        jb = step * (pg // ppb) + c
        col = lane_b == jb
        bs_ref[...] = jnp.where(col, jnp.mean(st, axis=-1, keepdims=True), bs_ref[...])
        m_ref[...] = jnp.where(col, m, m_ref[...])
        l_ref[...] = jnp.where(col, jnp.sum(p, axis=-1, keepdims=True), l_ref[...])
        acc_ref[jb] = _dot_nt(p.astype(BF16), vblk)

    @pl.when(step == pl.num_programs(1) - 1)
    def _():
        sel = _select_topk(bs_ref[...], jnp.ones((r, nb), jnp.bool_), lane_b.astype(F32), nb)
        key_t = lax.broadcasted_iota(jnp.int32, (r, ds), 1)
        qry_t = lax.broadcasted_iota(jnp.int32, (r, ds), 0) % ds
        s_own = jnp.where(key_t <= qry_t, _dot_nt(qbd, kn_ref[...]) * c2, NEG_INF)
        m_all = m_ref[...]
        m_tot = jnp.maximum(jnp.max(jnp.where(sel, m_all, NEG_INF), axis=-1, keepdims=True),
                            jnp.max(s_own, axis=-1, keepdims=True))
        w = jnp.where(sel, jnp.exp2(m_all - m_tot), 0.0)
        p_own = jnp.exp2(s_own - m_tot)
        den = jnp.sum(w * l_ref[...], axis=-1, keepdims=True) + jnp.sum(p_own, axis=-1, keepdims=True)
        num = _dot(p_own.astype(BF16), vn_ref[...])

        def body(j, num):
            wj = jnp.sum(jnp.where(lane_b == j, w, 0.0), axis=-1, keepdims=True)
            return num + wj * acc_ref[j]

        out = jnp.where(hmask, lax.fori_loop(0, nb, body, num) / den, 0.0)
        o = out[0:ds, :]
        for h in range(1, n_heads):
            o = o + out[h * ds:(h + 1) * ds, :]
        o_ref[...] = o.astype(o_ref.dtype)


def _moba_sample(q, kb, vb, cache_k, cache_v, page_table, ds):
    n, width = q.shape
    dbsz, n_pages = page_table.shape
    n_heads = width // HEAD_DIM
    assert (n_pages * PAGE_SIZE) % MOBA_BLOCK == 0, "cached length must fill whole MoBA blocks"
    nb = n_pages * PAGE_SIZE // MOBA_BLOCK
    assert nb >= MOBA_TOPK
    pg = PAGES_PER_STEP
    ppb = MOBA_BLOCK // PAGE_SIZE
    assert n_pages % pg == 0 and pg % ppb == 0 and ds % 8 == 0
    assert cache_k.shape[1:] == (PAGE_SIZE, n_heads, HEAD_DIM)
    r = n_heads * ds
    ck = jnp.transpose(cache_k, (0, 2, 3, 1)).reshape(cache_k.shape[0], width, PAGE_SIZE)
    cv = jnp.transpose(cache_v, (0, 2, 3, 1)).reshape(cache_v.shape[0], width, PAGE_SIZE)

    def page_spec(t):
        return pl.BlockSpec((None, width, PAGE_SIZE), lambda b, s, pt: (pt[b, s * pg + t], 0, 0))

    tok = pl.BlockSpec((ds, width), lambda b, s, pt: (b, 0))
    grid_spec = pltpu.PrefetchScalarGridSpec(
        num_scalar_prefetch=1, grid=(dbsz, n_pages // pg),
        in_specs=[tok, tok, tok] + [page_spec(t) for t in range(pg)] + [page_spec(t) for t in range(pg)],
        out_specs=tok,
        scratch_shapes=[pltpu.VMEM((r, nb), F32), pltpu.VMEM((r, nb), F32), pltpu.VMEM((r, nb), F32),
                        pltpu.VMEM((nb, r, width), F32)],
    )
    kern = functools.partial(_moba_sample_kernel, pg=pg, n_heads=n_heads, nb=nb)
    return pl.pallas_call(
        kern, grid_spec=grid_spec, out_shape=jax.ShapeDtypeStruct((n, width), BF16),
        name='moba_sample',
        compiler_params=pltpu.CompilerParams(dimension_semantics=('arbitrary', 'arbitrary'),
                                             vmem_limit_bytes=VMEM_LIMIT),
    )(page_table, q, kb, vb, *([ck] * pg), *([cv] * pg))


def _pack_bf16_pairs(x):
    w = x.shape[1] // 2
    xb = x.astype(BF16).astype(F32)
    hi = lax.bitcast_convert_type(xb[:, :w], jnp.uint32) & jnp.uint32(0xFFFF0000)
    lo = lax.bitcast_convert_type(xb[:, w:], jnp.uint32) >> 16
    return hi | lo


def _unpack_bf16_pairs(u):
    hi = lax.bitcast_convert_type(u & jnp.uint32(0xFFFF0000), F32)
    lo = lax.bitcast_convert_type(u << 16, F32)
    return jnp.concatenate([hi, lo], axis=1).astype(BF16)


def _post_attn_kernel(x_ref, aterm_ref, sgb_ref, ob_ref, wb_ref, wo_ref, gffn_ref, wr_ref, br_ref,
                      x1_ref, hf_ref, route_ref):
    merged = aterm_ref[...] + sgb_ref[...] * _dot(ob_ref[...], wb_ref[...])
    x1 = x_ref[...] + _dot(merged.astype(BF16), wo_ref[...])
    x1_ref[...] = x1
    hf = _rms(x1, gffn_ref[...])
    hf_ref[...] = _pack_bf16_pairs(hf)
    logits = _dot(hf.astype(BF16), wr_ref[...]) + br_ref[...]
    tm, lanes = logits.shape
    lane = lax.broadcasted_iota(jnp.int32, (tm, lanes), 1).astype(F32)
    n_exp = N_GROUPS * EXPERTS_PER_GROUP
    gmask = lane < N_GROUPS
    gl = jnp.where(gmask, logits, -jnp.inf)
    gmax = jnp.max(gl, axis=-1, keepdims=True)
    grp = jnp.min(jnp.where(gl == gmax, lane, float(lanes)), axis=-1, keepdims=True)
    p_grp = 1.0 / jnp.sum(jnp.where(gmask, jnp.exp(gl - gmax), 0.0), axis=-1, keepdims=True)
    lo = N_GROUPS + grp * EXPERTS_PER_GROUP
    emask = jnp.logical_and(jnp.logical_and(lane >= lo, lane < lo + EXPERTS_PER_GROUP),
                            lane < N_GROUPS + n_exp)
    e1 = jnp.where(emask, logits, -jnp.inf)
    v1 = jnp.max(e1, axis=-1, keepdims=True)
    j1 = jnp.min(jnp.where(e1 == v1, lane, float(lanes)), axis=-1, keepdims=True)
    e2 = jnp.where(lane == j1, -jnp.inf, e1)
    v2 = jnp.max(e2, axis=-1, keepdims=True)
    j2 = jnp.min(jnp.where(e2 == v2, lane, float(lanes)), axis=-1, keepdims=True)
    t = jnp.exp(v2 - v1)
    p1 = 1.0 / (1.0 + t)
    p2 = t / (1.0 + t)
    rec = jnp.where(lane == 0, j1 - N_GROUPS, 0.0)
    rec = jnp.where(lane == 1, j2 - N_GROUPS, rec)
    rec = jnp.where(lane == 2, p_grp * p1, rec)
    rec = jnp.where(lane == 3, p_grp * p2, rec)
    route_ref[...] = rec


def _post_attn(x, aterm, sgb, out_b, wts):
    n, d_model = x.shape
    tm = ROW_TILE
    row = lambda i: (i, 0)
    return pl.pallas_call(
        _post_attn_kernel, grid=(n // tm,),
        in_specs=[pl.BlockSpec((tm, d_model), row), pl.BlockSpec((tm, d_model), row),
                  pl.BlockSpec((tm, d_model), row), pl.BlockSpec((tm, out_b.shape[1]), row),
                  _const_spec(wts['w_b'].shape), _const_spec(wts['w_o'].shape),
                  _const_spec((1, d_model)), _const_spec(wts['w_r'].shape), _const_spec((1, PLE_LANES))],
        out_specs=[pl.BlockSpec((tm, d_model), row), pl.BlockSpec((tm, d_model // 2), row),
                   pl.BlockSpec((tm, PLE_LANES), row)],
        out_shape=[jax.ShapeDtypeStruct((n, d_model), F32),
                   jax.ShapeDtypeStruct((n, d_model // 2), jnp.uint32),
                   jax.ShapeDtypeStruct((n, PLE_LANES), F32)],
        name='post_attn',
        compiler_params=pltpu.CompilerParams(dimension_semantics=('arbitrary',),
                                             vmem_limit_bytes=VMEM_LIMIT),
    )(x, aterm, sgb, out_b, wts['w_b'], wts['w_o'], wts['g_ffn'], wts['w_r'], wts['b_r'])


def _dispatch_kernel(dest_ref, hf_ref, xd_in_ref, xd_ref, sem, *, fanout):
    del xd_in_ref
    tm = hf_ref.shape[0]

    def issue(rr, carry):
        for kk in range(fanout):
            pltpu.make_async_copy(hf_ref.at[pl.ds(rr, 1)],
                                  xd_ref.at[pl.ds(dest_ref[0, 0, rr * fanout + kk], 1)], sem).start()
        return carry

    lax.fori_loop(0, tm, issue, 0, unroll=DMA_ISSUE_UNROLL)
    for kk in range(fanout):
        pltpu.make_async_copy(hf_ref, xd_ref.at[pl.ds(0, tm)], sem).wait()


def _dispatch(hf, dest, n_rows):
    n, w = hf.shape
    fanout = dest.shape[0] // n
    tm = min(n, 512)
    steps = n // tm
    dest3 = dest.reshape(steps, 1, tm * fanout)
    xd0 = jnp.zeros((n_rows, w), hf.dtype)
    kern = functools.partial(_dispatch_kernel, fanout=fanout)
    return pl.pallas_call(
        kern, grid=(steps,),
        in_specs=[pl.BlockSpec((1, 1, tm * fanout), lambda i: (i, 0, 0), memory_space=pltpu.SMEM),
                  pl.BlockSpec((tm, w), lambda i: (i, 0)), pl.BlockSpec(memory_space=pl.ANY)],
        out_specs=pl.BlockSpec(memory_space=pl.ANY),
        out_shape=jax.ShapeDtypeStruct((n_rows, w), hf.dtype),
        scratch_shapes=[pltpu.SemaphoreType.DMA(())],
        input_output_aliases={2: 0},
        name='dispatch',
        compiler_params=pltpu.CompilerParams(dimension_semantics=('arbitrary',)),
    )(dest3, hf, xd0)


def _moe_ffn_kernel(be_ref, nreal_ref, xd_ref, w1_ref, w3_ref, w2_ref, yd_ref):
    del be_ref
    i = pl.program_id(0)

    @pl.when(i < nreal_ref[0])
    def _():
        xb = _unpack_bf16_pairs(xd_ref[...])
        a = _dot(xb, w1_ref[...])
        b = _dot(xb, w3_ref[...])
        yd_ref[...] = _dot((jax.nn.silu(a) * b).astype(BF16), w2_ref[...])

    @pl.when(i >= nreal_ref[0])
    def _():
        yd_ref[...] = jnp.zeros_like(yd_ref)


def _moe_ffn(xd, blk_e, n_real, w1, w3, w2):
    n_rows, half = xd.shape
    d_model = 2 * half
    ff = w1.shape[-1]
    blk = MOE_BLOCK
    grid_spec = pltpu.PrefetchScalarGridSpec(
        num_scalar_prefetch=2, grid=(n_rows // blk,),
        in_specs=[pl.BlockSpec((blk, half), lambda i, be, nr: (i, 0)),
                  pl.BlockSpec((None, d_model, ff), lambda i, be, nr: (be[i], 0, 0)),
                  pl.BlockSpec((None, d_model, ff), lambda i, be, nr: (be[i], 0, 0)),
                  pl.BlockSpec((None, ff, d_model), lambda i, be, nr: (be[i], 0, 0))],
        out_specs=pl.BlockSpec((blk, d_model), lambda i, be, nr: (i, 0)),
    )
    return pl.pallas_call(
        _moe_ffn_kernel, grid_spec=grid_spec,
        out_shape=jax.ShapeDtypeStruct((n_rows, d_model), F32),
        name='moe_ffn',
        compiler_params=pltpu.CompilerParams(dimension_semantics=('arbitrary',),
                                             vmem_limit_bytes=VMEM_LIMIT),
    )(blk_e, n_real, xd, w1, w3, w2)


def _final_kernel(dest_ref, yd_ref, x1_ref, route_ref, p_ref, gple_ref, wpg_ref, wple_ref,
                  y_ref, ybuf, sem, *, fanout):
    tm = x1_ref.shape[0]

    def issue(rr, carry):
        for kk in range(fanout):
            pltpu.make_async_copy(yd_ref.at[pl.ds(dest_ref[0, 0, rr * fanout + kk], 1)],
                                  ybuf.at[kk, pl.ds(rr, 1)], sem).start()
        return carry

    lax.fori_loop(0, tm, issue, 0, unroll=DMA_ISSUE_UNROLL)
    for kk in range(fanout):
        pltpu.make_async_copy(yd_ref.at[pl.ds(0, tm)], ybuf.at[kk], sem).wait()

    route = route_ref[...]
    moe = route[:, 2:3] * ybuf[0]
    for kk in range(1, fanout):
        moe = moe + route[:, 2 + kk:3 + kk] * ybuf[kk]
    x2 = x1_ref[...] + moe
    gate = jax.nn.sigmoid(_dot(_rms(x2, gple_ref[...]).astype(BF16), wpg_ref[...]))
    y_ref[...] = x2 + gate * _dot(p_ref[...].astype(BF16), wple_ref[...])


def _final(x1, yd, dest, route, p, wts):
    n, d_model = x1.shape
    fanout = dest.shape[0] // n
    tm = ROW_TILE
    steps = n // tm
    dest3 = dest.reshape(steps, 1, tm * fanout)
    row = lambda i: (i, 0)
    kern = functools.partial(_final_kernel, fanout=fanout)
    return pl.pallas_call(
        kern, grid=(steps,),
        in_specs=[pl.BlockSpec((1, 1, tm * fanout), lambda i: (i, 0, 0), memory_space=pltpu.SMEM),
                  pl.BlockSpec(memory_space=pl.ANY),
                  pl.BlockSpec((tm, d_model), row), pl.BlockSpec((tm, PLE_LANES), row),
                  pl.BlockSpec((tm, p.shape[1]), row),
                  _const_spec((1, d_model)), _const_spec(wts['w_pg'].shape), _const_spec(wts['w_ple'].shape)],
        out_specs=pl.BlockSpec((tm, d_model), row),
        out_shape=jax.ShapeDtypeStruct((n, d_model), F32),
        scratch_shapes=[pltpu.VMEM((fanout, tm, d_model), F32), pltpu.SemaphoreType.DMA(())],
        name='final',
        compiler_params=pltpu.CompilerParams(dimension_semantics=('arbitrary',),
                                             vmem_limit_bytes=VMEM_LIMIT),
    )(dest3, yd, x1, route, p, wts['g_ple'], wts['w_pg'], wts['w_ple'])


def _routing_plan(eid, blk):
    n_exp = N_GROUPS * EXPERTS_PER_GROUP
    e = eid.reshape(-1)
    n_assign = e.shape[0]
    onehot = (e[:, None] == jnp.arange(n_exp, dtype=jnp.int32)[None, :]).astype(jnp.int32)
    counts = jnp.sum(onehot, axis=0)
    rank = jnp.take_along_axis(jnp.cumsum(onehot, axis=0), e[:, None], axis=1)[:, 0] - 1
    pcounts = ((counts + blk - 1) // blk) * blk
    pend = jnp.cumsum(pcounts)
    pstart = pend - pcounts
    dest = (pstart[e] + rank).astype(jnp.int32)
    n_blocks = -(-(n_assign + n_exp * (blk - 1)) // blk)
    blk_start = jnp.arange(n_blocks, dtype=jnp.int32) * blk
    blk_e = jnp.minimum(jnp.sum(pend[None, :] <= blk_start[:, None], axis=1), n_exp - 1).astype(jnp.int32)
    n_real = (pend[-1:] // blk).astype(jnp.int32)
    return dest, blk_e, n_real, n_blocks * blk


def _rope_tables(pos, n_heads):
    half = HEAD_DIM // 2
    inv = ROPE_THETA ** (-jnp.arange(half, dtype=F32) / half)
    ang = pos.astype(F32)[:, None] * inv[None, :]
    cos = jnp.cos(ang)
    sin = jnp.sin(ang)
    return (jnp.tile(jnp.concatenate([cos, cos], axis=-1), (1, n_heads)),
            jnp.tile(jnp.concatenate([-sin, sin], axis=-1), (1, n_heads)))


def _mix_tables(w_s_l, b_s_l, t_mix, a_width):
    reps = CHUNK // t_mix
    tri = jnp.tril(jnp.ones((t_mix, t_mix), F32))
    wt = w_s_l[:, :t_mix, :t_mix] * tri[None]
    eye = jnp.eye(reps, dtype=F32)
    wbig = jnp.einsum('ab,gts->gatbs', eye, wt).reshape(A_GROUPS, CHUNK, CHUNK)
    wmix = jnp.transpose(wbig, (1, 0, 2)).reshape(CHUNK, A_GROUPS * CHUNK)
    bs = jnp.tile(b_s_l[:, :t_mix], (1, reps))
    bs_tab = jnp.repeat(bs.T, a_width // A_GROUPS, axis=1)
    return dict(wmix=wmix.astype(BF16), bs_tab=bs_tab)


def _layer_weights(l, g_mix, w_in, g_v, g_q, g_k, w_a, w_b, w_o, g_ffn, w_rg, b_rg, w_re, b_re,
                   w1, w3, w2, g_ple, w_pg, w_ple):
    b_width = w_b.shape[1]
    n_heads = b_width // HEAD_DIM
    d_model = w_o.shape[-1]
    hid = jnp.arange(b_width) // HEAD_DIM
    hind = jnp.where(hid[:, None] == hid[None, :], 1.0 / HEAD_DIM, 0.0)
    n_exp = N_GROUPS * EXPERTS_PER_GROUP
    w_r = jnp.zeros((d_model, PLE_LANES), F32)
    w_r = w_r.at[:, :N_GROUPS].set(w_rg[l]).at[:, N_GROUPS:N_GROUPS + n_exp].set(w_re[l])
    b_r = jnp.zeros((1, PLE_LANES), F32)
    b_r = b_r.at[0, :N_GROUPS].set(b_rg[l]).at[0, N_GROUPS:N_GROUPS + n_exp].set(b_re[l])
    return dict(
        g_mix=g_mix[l][None], w_in=w_in[l].astype(BF16), g_v=g_v[l][None], hind=hind.astype(BF16),
        g_q=jnp.tile(g_q[l], n_heads)[None], g_k=jnp.tile(g_k[l], n_heads)[None],
        w_a=w_a[l].astype(BF16), w_b=w_b[l].astype(BF16), w_o=w_o[l].astype(BF16),
        g_ffn=g_ffn[l][None], w_r=w_r.astype(BF16), b_r=b_r,
        w1=w1[l].astype(BF16), w3=w3[l].astype(BF16), w2=w2[l].astype(BF16),
        g_ple=g_ple[l][None], w_pg=w_pg[l].astype(BF16), w_ple=w_ple[l].astype(BF16),
    )


def _finish(x, aterm, sgb, out_b, p, wts):
    x1, hf, route = _post_attn(x, aterm, sgb, out_b, wts)
    eid = route[:, :2].astype(jnp.int32)
    dest, blk_e, n_real, n_rows = _routing_plan(eid, MOE_BLOCK)
    xd = _dispatch(hf, dest, n_rows)
    yd = _moe_ffn(xd, blk_e, n_real, wts['w1'], wts['w3'], wts['w2'])
    return _final(x1, yd, dest, route, p, wts)


def kernel(x_prompt, x_sample, cache_k, cache_v, page_table, p_prompt, p_sample, g_mix, w_in, g_v, w_s, b_s, g_q, g_k, w_a, w_b, w_o, g_ffn, w_router_group, b_router_group, w_router_expert, b_router_expert, w1, w3, w2, g_ple, w_ple_gate, w_ple):
    bsz, seq, d_model = x_prompt.shape
    dbsz, dseq, _ = x_sample.shape
    depth = g_mix.shape[0]
    b_width = w_b.shape[1]
    n_heads = b_width // HEAD_DIM
    past_len = page_table.shape[1] * PAGE_SIZE
    assert seq % MOBA_BLOCK == 0 and (bsz * seq) % ROW_TILE == 0 and (dbsz * dseq) % ROW_TILE == 0
    assert CHUNK % dseq == 0 and ROW_TILE % CHUNK == 0 and ROW_TILE % MOBA_BLOCK == 0
    assert seq // MOBA_BLOCK >= MOBA_TOPK
    params = (g_mix, w_in, g_v, g_q, g_k, w_a, w_b, w_o, g_ffn, w_router_group, b_router_group,
              w_router_expert, b_router_expert, w1, w3, w2, g_ple, w_ple_gate, w_ple)
    a_width = g_v.shape[-1]
    tab_p = _rope_tables(jnp.arange(seq, dtype=jnp.int32), n_heads)
    pos_s = past_len + (jnp.arange(ROW_TILE, dtype=jnp.int32) % dseq)
    tab_s = _rope_tables(pos_s, n_heads)
    xp = x_prompt.reshape(bsz * seq, d_model)
    xs = x_sample.reshape(dbsz * dseq, d_model)
    kp_rows, vp_rows, ks_rows, vs_rows, chunk_rows = [], [], [], [], []
    for l in range(depth):
        wts = _layer_weights(l, *params)
        wts_p = dict(wts, **_mix_tables(w_s[l], b_s[l], CHUNK, a_width))
        wts_s = dict(wts, **_mix_tables(w_s[l], b_s[l], dseq, a_width))
        aterm, sgb, qt, k, v, kb, vt, kmean = _inproj(
            xp, tab_p, wts_p, emit_kmean=True, emit_vchunk=False, pos_blocks=seq // ROW_TILE)
        nbt = bsz * seq // MOBA_BLOCK
        out_b = _moba_prompt(qt, kb.reshape(nbt, MOBA_BLOCK, b_width), vt,
                             kmean.reshape(bsz, seq // MOBA_BLOCK, b_width), bsz, seq)
        xp = _finish(xp, aterm, sgb, out_b, p_prompt[l].reshape(bsz * seq, -1), wts_p)
        kp_rows.append(jnp.transpose(k.reshape(bsz, n_heads, HEAD_DIM, seq), (0, 3, 1, 2)))
        vp_rows.append(jnp.transpose(v.reshape(bsz, n_heads, HEAD_DIM, seq), (0, 3, 1, 2)))
        aterm, sgb, q, k, v, kb, vb, vchunk = _inproj(
            xs, tab_s, wts_s, emit_kmean=False, emit_vchunk=True, pos_blocks=1)
        out_b = _moba_sample(q, kb, vb, cache_k[l], cache_v[l], page_table, dseq)
        xs = _finish(xs, aterm, sgb, out_b, p_sample[l].reshape(dbsz * dseq, -1), wts_s)
        ks_rows.append(k.reshape(dbsz, dseq, n_heads, HEAD_DIM))
        vs_rows.append(v.reshape(dbsz, dseq, n_heads, HEAD_DIM))
        chunk_rows.append(vchunk.reshape(dbsz, dseq, -1))
    return (xp.reshape(bsz, seq, d_model), xs.reshape(dbsz, dseq, d_model),
            jnp.stack(kp_rows), jnp.stack(vp_rows), jnp.stack(ks_rows), jnp.stack(vs_rows),
            jnp.stack(chunk_rows))
```

```python
import functools

import jax
import jax.numpy as jnp
from jax import lax
from jax.experimental import pallas as pl
from jax.experimental.pallas import tpu as pltpu

F32 = jnp.float32
BF16 = jnp.bfloat16

EPS = 1e-6
NEG_INF = -1e30
A_GROUPS = 8
CHUNK = 128
HEAD_DIM = 64
MOBA_BLOCK = 256
MOBA_TOPK = 3
ROPE_THETA = 10000.0
N_GROUPS = 4
EXPERTS_PER_GROUP = 8
PAGE_SIZE = 128
PLE_LANES = 128
ROW_TILE = 256
MOE_BLOCK = 256
PAGES_PER_STEP = 16
VMEM_LIMIT = 56 * 1024 * 1024
LOG2_E = 1.4426950408889634
ONES_ROWS = 16
DMA_ISSUE_UNROLL = 8

_NT = (((1,), (1,)), ((), ()))


def _rms(x, g):
    return x * lax.rsqrt(jnp.mean(x * x, axis=-1, keepdims=True) + EPS) * g


def _dot(a, b):
    return jnp.dot(a, b, preferred_element_type=F32)


def _dot_nt(a, b):
    return lax.dot_general(a, b, _NT, preferred_element_type=F32)


def _dot_tn(a, b):
    return lax.dot_general(a, b, (((0,), (0,)), ((), ())), preferred_element_type=F32)


def _const_spec(shape):
    return pl.BlockSpec(shape, lambda *_: (0,) * len(shape))


def _inproj_kernel(x_ref, gmix_ref, win_ref, gv_ref, wmix_ref, bs_ref, hind_ref, gq_ref, gk_ref,
                   cos_ref, sin_ref, wa_ref,
                   aterm_ref, sgb_ref, q_ref, k_ref, v_ref, kb_ref, vb_ref, *extra,
                   a_width, b_width, d_model, emit_kmean, emit_vchunk, transpose_qv):
    tm = x_ref.shape[0]
    h = _rms(x_ref[...], gmix_ref[...]).astype(BF16)
    offs = [0]

    def proj(width):
        o = offs[0]
        offs[0] = o + width
        return _dot(h, win_ref[:, o:o + width])

    zu = proj(a_width)
    zv = proj(a_width)
    zq = proj(b_width)
    zk = proj(b_width)
    zva = proj(b_width)

    u = jax.nn.gelu(zu)
    vn = _rms(jax.nn.gelu(zv), gv_ref[...])
    vb16 = vn.astype(BF16)
    lane_grp = lax.broadcasted_iota(jnp.int32, (CHUNK, a_width), 1) // (a_width // A_GROUPS)
    parts = []
    for c in range(tm // CHUNK):
        vc = vb16[c * CHUNK:(c + 1) * CHUNK, :]
        rhs = jnp.concatenate(
            [jnp.where(lane_grp == g, vc, jnp.zeros_like(vc)) for g in range(A_GROUPS)], axis=0)
        parts.append(_dot(wmix_ref[...], rhs) + bs_ref[...])
    s = parts[0] if len(parts) == 1 else jnp.concatenate(parts, axis=0)
    out_a = (u * s).astype(BF16)
    ga = proj(d_model)
    aterm_ref[...] = jax.nn.sigmoid(ga) * _dot(out_a, wa_ref[...])
    gb = proj(d_model)
    sgb_ref[...] = jax.nn.sigmoid(gb)

    lane = lax.broadcasted_iota(jnp.int32, (tm, b_width), 1)
    first_half = (lane % HEAD_DIM) < (HEAD_DIM // 2)
    cos = cos_ref[...]
    sin = sin_ref[...]

    def headnorm_rope(z, g):
        ms = _dot((z * z).astype(BF16), hind_ref[...])
        y = z * lax.rsqrt(ms + EPS) * g
        swapped = jnp.where(first_half,
                            pltpu.roll(y, b_width - HEAD_DIM // 2, 1),
                            pltpu.roll(y, HEAD_DIM // 2, 1))
        return y * cos + swapped * sin

    q = headnorm_rope(zq, gq_ref[...])
    k = headnorm_rope(zk, gk_ref[...])
    kb_ref[...] = k.astype(BF16)
    if transpose_qv:
        vt = zva.T
        q_ref[0] = q.T.astype(BF16)
        k_ref[0] = k.T
        v_ref[0] = vt
        vb_ref[0] = vt.astype(BF16)
    else:
        q_ref[...] = q.astype(BF16)
        k_ref[...] = k
        v_ref[...] = zva
        vb_ref[...] = zva.astype(BF16)
    idx = 0
    if emit_kmean:
        km_ref = extra[idx]
        idx += 1
        for bi in range(tm // MOBA_BLOCK):
            km_ref[bi] = jnp.mean(k[bi * MOBA_BLOCK:(bi + 1) * MOBA_BLOCK, :], axis=0, keepdims=True)
    if emit_vchunk:
        extra[idx][...] = vn


def _inproj(x, pos_tables, wts, *, emit_kmean, emit_vchunk, pos_blocks):
    n, d_model = x.shape
    a_width = wts['g_v'].shape[-1]
    b_width = wts['hind'].shape[0]
    tm = ROW_TILE
    cos_t, sin_t = pos_tables
    row = lambda i: (i, 0)
    posrow = lambda i: (i % pos_blocks, 0)
    in_specs = [
        pl.BlockSpec((tm, d_model), row),
        _const_spec((1, d_model)),
        _const_spec(wts['w_in'].shape),
        _const_spec((1, a_width)),
        _const_spec(wts['wmix'].shape),
        _const_spec(wts['bs_tab'].shape),
        _const_spec(wts['hind'].shape),
        _const_spec((1, b_width)),
        _const_spec((1, b_width)),
        pl.BlockSpec((tm, b_width), posrow),
        pl.BlockSpec((tm, b_width), posrow),
        _const_spec(wts['w_a'].shape),
    ]
    transpose_qv = emit_kmean
    if transpose_qv:
        assert tm == MOBA_BLOCK
        qv_shape = jax.ShapeDtypeStruct((n // tm, b_width, tm), BF16)
        qv_spec = pl.BlockSpec((1, b_width, tm), lambda i: (i, 0, 0))
        kv_shape = jax.ShapeDtypeStruct((n // (tm * pos_blocks), b_width, tm * pos_blocks), F32)
        kv_spec = pl.BlockSpec((1, b_width, tm), lambda i: (i // pos_blocks, 0, i % pos_blocks))
    else:
        qv_shape = jax.ShapeDtypeStruct((n, b_width), BF16)
        qv_spec = pl.BlockSpec((tm, b_width), row)
        kv_shape = jax.ShapeDtypeStruct((n, b_width), F32)
        kv_spec = pl.BlockSpec((tm, b_width), row)
    out_shape = [
        jax.ShapeDtypeStruct((n, d_model), F32),
        jax.ShapeDtypeStruct((n, d_model), F32),
        qv_shape,
        kv_shape,
        kv_shape,
        jax.ShapeDtypeStruct((n, b_width), BF16),
        qv_shape,
    ]
    out_specs = [
        pl.BlockSpec((tm, d_model), row), pl.BlockSpec((tm, d_model), row),
        qv_spec, kv_spec, kv_spec, pl.BlockSpec((tm, b_width), row),
        qv_spec,
    ]
    if emit_kmean:
        nbt = tm // MOBA_BLOCK
        out_shape.append(jax.ShapeDtypeStruct((n // MOBA_BLOCK, 1, b_width), F32))
        out_specs.append(pl.BlockSpec((nbt, 1, b_width), lambda i: (i, 0, 0)))
    if emit_vchunk:
        out_shape.append(jax.ShapeDtypeStruct((n, a_width), F32))
        out_specs.append(pl.BlockSpec((tm, a_width), row))
    kern = functools.partial(_inproj_kernel, a_width=a_width, b_width=b_width, d_model=d_model,
                             emit_kmean=emit_kmean, emit_vchunk=emit_vchunk, transpose_qv=transpose_qv)
    return pl.pallas_call(
        kern, grid=(n // tm,), in_specs=in_specs, out_specs=out_specs, out_shape=out_shape,
        name='inproj',
        compiler_params=pltpu.CompilerParams(dimension_semantics=('arbitrary',),
                                             vmem_limit_bytes=VMEM_LIMIT),
    )(x, wts['g_mix'], wts['w_in'], wts['g_v'], wts['wmix'], wts['bs_tab'], wts['hind'],
      wts['g_q'], wts['g_k'], cos_t, sin_t, wts['w_a'])


def _select_topk(scores, allowed, blk_f, nb):
    sel = jnp.zeros(scores.shape, jnp.bool_)
    for _ in range(MOBA_TOPK):
        cand = jnp.logical_and(allowed, jnp.logical_not(sel))
        scm = jnp.where(cand, scores, -jnp.inf)
        mx = jnp.max(scm, axis=-1, keepdims=True)
        is_max = jnp.logical_and(cand, scm == mx)
        first = jnp.min(jnp.where(is_max, blk_f, float(nb)), axis=-1, keepdims=True)
        sel = jnp.logical_or(sel, jnp.logical_and(is_max, blk_f == first))
    return sel


def _select_topk_rows(scores, allowed, blk_f, nb):
    sel = jnp.zeros(scores.shape, jnp.bool_)
    for _ in range(MOBA_TOPK):
        cand = jnp.logical_and(allowed, jnp.logical_not(sel))
        scm = jnp.where(cand, scores, -jnp.inf)
        mx = jnp.max(scm, axis=0, keepdims=True)
        is_max = jnp.logical_and(cand, scm == mx)
        first = jnp.min(jnp.where(is_max, blk_f, float(nb)), axis=0, keepdims=True)
        sel = jnp.logical_or(sel, jnp.logical_and(is_max, blk_f == first))
    return sel


def _moba_prompt_kernel(qt_ref, kb_ref, vt_ref, km_ref, o_ref, w_ref, bias_ref, m_ref, l_ref, acc_ref,
                        s_ref, *, n_heads, nb):
    tq = qt_ref.shape[2]
    blk = MOBA_BLOCK
    pair = 2 * HEAD_DIM
    i = pl.program_id(1)
    scale = HEAD_DIM ** -0.5 * LOG2_E
    km = km_ref[0].astype(BF16)
    blk_f = lax.broadcasted_iota(jnp.int32, (nb, tq), 0).astype(F32)
    allowed = blk_f < i.astype(F32)
    key_t = lax.broadcasted_iota(jnp.int32, (blk, tq), 0)
    qry_t = lax.broadcasted_iota(jnp.int32, (blk, tq), 1)
    causal = key_t <= qry_t
    zeros = jnp.zeros((HEAD_DIM, tq), BF16)
    ones = jnp.ones((ONES_ROWS, blk), BF16)

    for h in range(n_heads):
        qth = qt_ref[0, h * HEAD_DIM:(h + 1) * HEAD_DIM, :]
        sel = _select_topk_rows(_dot(km[:, h * HEAD_DIM:(h + 1) * HEAD_DIM], qth), allowed, blk_f, nb)
        bias_ref[h] = jnp.where(sel, 0.0, NEG_INF)
        qs = (qth.astype(F32) * scale).astype(BF16)
        col = jnp.concatenate([qs, zeros] if h % 2 == 0 else [zeros, qs], axis=0)
        w_ref[h // 2, :, (h % 2) * tq:(h % 2 + 1) * tq] = col

    def scores(j, hp):
        return _dot(kb_ref[j, :, hp * pair:(hp + 1) * pair], w_ref[hp])

    def weighted_values(j, h, p):
        vt1 = jnp.concatenate([vt_ref[j, h * HEAD_DIM:(h + 1) * HEAD_DIM, :], ones], axis=0)
        pv = _dot(vt1, p.astype(BF16))
        return pv[:HEAD_DIM, :], pv[HEAD_DIM:HEAD_DIM + 1, :]

    for hp in range(n_heads // 2):
        s2 = scores(i, hp)
        for h in (2 * hp, 2 * hp + 1):
            s = jnp.where(causal, s2[:, (h % 2) * tq:(h % 2 + 1) * tq], NEG_INF)
            m = jnp.max(s, axis=0, keepdims=True)
            pv, psum = weighted_values(i, h, jnp.exp2(s - m))
            m_ref[h:h + 1, :] = m
            l_ref[h:h + 1, :] = psum
            acc_ref[h * HEAD_DIM:(h + 1) * HEAD_DIM, :] = pv

    def stage_scores(j, slot):
        jc = jnp.minimum(j, nb - 1)
        for hp in range(n_heads // 2):
            s_ref[slot, hp] = scores(jc, hp)

    def consume(j, slot):
        jc = jnp.minimum(j, nb - 1)
        for h in range(n_heads):
            s = s_ref[slot, h // 2, :, (h % 2) * tq:(h % 2 + 1) * tq]
            bias = bias_ref[h, pl.ds(jc, 1), :]
            m = m_ref[h:h + 1, :]
            m_new = jnp.maximum(m, jnp.max(s, axis=0, keepdims=True) + bias)
            alpha = jnp.exp2(m - m_new)
            pv, psum = weighted_values(jc, h, jnp.exp2(s + (bias - m_new)))
            m_ref[h:h + 1, :] = m_new
            l_ref[h:h + 1, :] = alpha * l_ref[h:h + 1, :] + psum
            rows = slice(h * HEAD_DIM, (h + 1) * HEAD_DIM)
            acc_ref[rows, :] = alpha * acc_ref[rows, :] + pv

    @pl.when(i > 0)
    def _():
        stage_scores(0, 0)

        def body(t, carry):
            j = 2 * t
            stage_scores(j + 1, 1)
            consume(j, 0)
            stage_scores(j + 2, 0)
            consume(j + 1, 1)
            return carry

        lax.fori_loop(0, (i + 1) // 2, body, 0)
    outs = [acc_ref[h * HEAD_DIM:(h + 1) * HEAD_DIM, :] / l_ref[h:h + 1, :] for h in range(n_heads)]
    o_ref[...] = jnp.concatenate(outs, axis=0).T.astype(o_ref.dtype)


def _moba_prompt(qt, kb, vt, kmean, bsz, seq):
    _, width, tq = qt.shape
    n_heads = width // HEAD_DIM
    nb = seq // MOBA_BLOCK
    assert tq == MOBA_BLOCK and n_heads % 2 == 0
    kern = functools.partial(_moba_prompt_kernel, n_heads=n_heads, nb=nb)
    return pl.pallas_call(
        kern, grid=(bsz, nb),
        in_specs=[
            pl.BlockSpec((1, width, tq), lambda b, i: (b * nb + i, 0, 0)),
            pl.BlockSpec((nb, MOBA_BLOCK, width), lambda b, i: (b, 0, 0)),
            pl.BlockSpec((nb, width, MOBA_BLOCK), lambda b, i: (b, 0, 0)),
            pl.BlockSpec((1, nb, width), lambda b, i: (b, 0, 0)),
        ],
        out_specs=pl.BlockSpec((tq, width), lambda b, i: (b * nb + i, 0)),
        out_shape=jax.ShapeDtypeStruct((bsz * seq, width), BF16),
        scratch_shapes=[pltpu.VMEM((n_heads // 2, 2 * HEAD_DIM, 2 * tq), BF16),
                        pltpu.VMEM((n_heads, nb, tq), F32),
                        pltpu.VMEM((n_heads, tq), F32), pltpu.VMEM((n_heads, tq), F32),
                        pltpu.VMEM((width, tq), F32),
                        pltpu.VMEM((2, n_heads // 2, MOBA_BLOCK, 2 * tq), F32)],
        name='moba_prompt',
        compiler_params=pltpu.CompilerParams(dimension_semantics=('arbitrary', 'arbitrary'),
                                             vmem_limit_bytes=VMEM_LIMIT),
    )(qt, kb, vt, kmean)


def _moba_sample_kernel(pt_ref, q_ref, kn_ref, vn_ref, *rest, pg, n_heads, nb):
    del pt_ref
    k_refs = rest[:pg]
    v_refs = rest[pg:2 * pg]
    o_ref = rest[2 * pg]
    bs_ref, m_ref, l_ref, acc_ref = rest[2 * pg + 1:]
    ds, width = q_ref.shape
    step = pl.program_id(1)
    r = n_heads * ds
    c2 = HEAD_DIM ** -0.5 * LOG2_E
    ppb = MOBA_BLOCK // PAGE_SIZE
    row_h = lax.broadcasted_iota(jnp.int32, (r, width), 0) // ds
    lane_h = lax.broadcasted_iota(jnp.int32, (r, width), 1) // HEAD_DIM
    hmask = row_h == lane_h
    qt = jnp.concatenate([q_ref[...]] * n_heads, axis=0)
    qbd = jnp.where(hmask, qt, jnp.zeros_like(qt))
    lane_b = lax.broadcasted_iota(jnp.int32, (r, nb), 1)

    @pl.when(step == 0)
    def _():
        bs_ref[...] = jnp.zeros_like(bs_ref)
        m_ref[...] = jnp.zeros_like(m_ref)
        l_ref[...] = jnp.zeros_like(l_ref)

    bs_new, m_new, l_new = bs_ref[...], m_ref[...], l_ref[...]
    kall = jnp.concatenate([k_refs[t][...] for t in range(pg)], axis=1).astype(BF16)
    st_all = _dot(qbd, kall)
    for c in range(pg // ppb):
        st = st_all[:, c * MOBA_BLOCK:(c + 1) * MOBA_BLOCK]
        vblk = jnp.concatenate([v_refs[ppb * c + t][...] for t in range(ppb)], axis=1).astype(BF16)
        s = st * c2
        m = jnp.max(s, axis=-1, keepdims=True)
        p = jnp.exp2(s - m)
        jb = step * (pg // ppb) + c
        col = lane_b == jb
        bs_new = jnp.where(col, jnp.mean(st, axis=-1, keepdims=True), bs_new)
        m_new = jnp.where(col, m, m_new)
        l_new = jnp.where(col, jnp.sum(p, axis=-1, keepdims=True), l_new)
        acc_ref[jb] = _dot_nt(p.astype(BF16), vblk)
    bs_ref[...] = bs_new
    m_ref[...] = m_new
    l_ref[...] = l_new

    @pl.when(step == pl.num_programs(1) - 1)
    def _():
        sel = _select_topk(bs_ref[...], jnp.ones((r, nb), jnp.bool_), lane_b.astype(F32), nb)
        key_t = lax.broadcasted_iota(jnp.int32, (r, ds), 1)
        qry_t = lax.broadcasted_iota(jnp.int32, (r, ds), 0) % ds
        s_own = jnp.where(key_t <= qry_t, _dot_nt(qbd, kn_ref[...]) * c2, NEG_INF)
        m_all = m_ref[...]
        m_tot = jnp.maximum(jnp.max(jnp.where(sel, m_all, NEG_INF), axis=-1, keepdims=True),
                            jnp.max(s_own, axis=-1, keepdims=True))
        w = jnp.where(sel, jnp.exp2(m_all - m_tot), 0.0)
        p_own = jnp.exp2(s_own - m_tot)
        den = jnp.sum(w * l_ref[...], axis=-1, keepdims=True) + jnp.sum(p_own, axis=-1, keepdims=True)
        num = _dot(p_own.astype(BF16), vn_ref[...])

        def body(j, num):
            wj = jnp.sum(jnp.where(lane_b == j, w, 0.0), axis=-1, keepdims=True)
            return num + wj * acc_ref[j]

        out = jnp.where(hmask, lax.fori_loop(0, nb, body, num) / den, 0.0)
        o = out[0:ds, :]
        for h in range(1, n_heads):
            o = o + out[h * ds:(h + 1) * ds, :]
        o_ref[...] = o.astype(o_ref.dtype)


def _moba_sample(q, kb, vb, cache_k, cache_v, page_table, ds):
    n, width = q.shape
    dbsz, n_pages = page_table.shape
    n_heads = width // HEAD_DIM
    assert (n_pages * PAGE_SIZE) % MOBA_BLOCK == 0, "cached length must fill whole MoBA blocks"
    nb = n_pages * PAGE_SIZE // MOBA_BLOCK
    assert nb >= MOBA_TOPK
    pg = PAGES_PER_STEP
    ppb = MOBA_BLOCK // PAGE_SIZE
    assert n_pages % pg == 0 and pg % ppb == 0 and ds % 8 == 0
    assert cache_k.shape[1:] == (PAGE_SIZE, n_heads, HEAD_DIM)
    r = n_heads * ds
    ck = jnp.transpose(cache_k, (0, 2, 3, 1)).reshape(cache_k.shape[0], width, PAGE_SIZE)
    cv = jnp.transpose(cache_v, (0, 2, 3, 1)).reshape(cache_v.shape[0], width, PAGE_SIZE)

    def page_spec(t):
        return pl.BlockSpec((None, width, PAGE_SIZE), lambda b, s, pt: (pt[b, s * pg + t], 0, 0))

    tok = pl.BlockSpec((ds, width), lambda b, s, pt: (b, 0))
    grid_spec = pltpu.PrefetchScalarGridSpec(
        num_scalar_prefetch=1, grid=(dbsz, n_pages // pg),
        in_specs=[tok, tok, tok] + [page_spec(t) for t in range(pg)] + [page_spec(t) for t in range(pg)],
        out_specs=tok,
        scratch_shapes=[pltpu.VMEM((r, nb), F32), pltpu.VMEM((r, nb), F32), pltpu.VMEM((r, nb), F32),
                        pltpu.VMEM((nb, r, width), F32)],
    )
    kern = functools.partial(_moba_sample_kernel, pg=pg, n_heads=n_heads, nb=nb)
    return pl.pallas_call(
        kern, grid_spec=grid_spec, out_shape=jax.ShapeDtypeStruct((n, width), BF16),
        name='moba_sample',
        compiler_params=pltpu.CompilerParams(dimension_semantics=('arbitrary', 'arbitrary'),
                                             vmem_limit_bytes=VMEM_LIMIT),
    )(page_table, q, kb, vb, *([ck] * pg), *([cv] * pg))


def _pack_bf16_pairs(x):
    w = x.shape[1] // 2
    xb = x.astype(BF16).astype(F32)
    hi = lax.bitcast_convert_type(xb[:, :w], jnp.uint32) & jnp.uint32(0xFFFF0000)
    lo = lax.bitcast_convert_type(xb[:, w:], jnp.uint32) >> 16
    return hi | lo


def _unpack_bf16_pairs(u):
    hi = lax.bitcast_convert_type(u & jnp.uint32(0xFFFF0000), F32)
    lo = lax.bitcast_convert_type(u << 16, F32)
    return jnp.concatenate([hi, lo], axis=1).astype(BF16)


def _post_attn_kernel(x_ref, aterm_ref, sgb_ref, ob_ref, wb_ref, wo_ref, gffn_ref, wr_ref, br_ref,
                      x1_ref, hf_ref, route_ref):
    merged = aterm_ref[...] + sgb_ref[...] * _dot(ob_ref[...], wb_ref[...])
    x1 = x_ref[...] + _dot(merged.astype(BF16), wo_ref[...])
    x1_ref[...] = x1
    hf = _rms(x1, gffn_ref[...])
    hf_ref[...] = _pack_bf16_pairs(hf)
    logits = _dot(hf.astype(BF16), wr_ref[...]) + br_ref[...]
    tm, lanes = logits.shape
    lane = lax.broadcasted_iota(jnp.int32, (tm, lanes), 1).astype(F32)
    n_exp = N_GROUPS * EXPERTS_PER_GROUP
    gmask = lane < N_GROUPS
    gl = jnp.where(gmask, logits, -jnp.inf)
    gmax = jnp.max(gl, axis=-1, keepdims=True)
    grp = jnp.min(jnp.where(gl == gmax, lane, float(lanes)), axis=-1, keepdims=True)
    p_grp = 1.0 / jnp.sum(jnp.where(gmask, jnp.exp(gl - gmax), 0.0), axis=-1, keepdims=True)
    lo = N_GROUPS + grp * EXPERTS_PER_GROUP
    emask = jnp.logical_and(jnp.logical_and(lane >= lo, lane < lo + EXPERTS_PER_GROUP),
                            lane < N_GROUPS + n_exp)
    e1 = jnp.where(emask, logits, -jnp.inf)
    v1 = jnp.max(e1, axis=-1, keepdims=True)
    j1 = jnp.min(jnp.where(e1 == v1, lane, float(lanes)), axis=-1, keepdims=True)
    e2 = jnp.where(lane == j1, -jnp.inf, e1)
    v2 = jnp.max(e2, axis=-1, keepdims=True)
    j2 = jnp.min(jnp.where(e2 == v2, lane, float(lanes)), axis=-1, keepdims=True)
    t = jnp.exp(v2 - v1)
    p1 = 1.0 / (1.0 + t)
    p2 = t / (1.0 + t)
    rec = jnp.where(lane == 0, j1 - N_GROUPS, 0.0)
    rec = jnp.where(lane == 1, j2 - N_GROUPS, rec)
    rec = jnp.where(lane == 2, p_grp * p1, rec)
    rec = jnp.where(lane == 3, p_grp * p2, rec)
    route_ref[...] = rec


def _post_attn(x, aterm, sgb, out_b, wts):
    n, d_model = x.shape
    tm = ROW_TILE
    row = lambda i: (i, 0)
    return pl.pallas_call(
        _post_attn_kernel, grid=(n // tm,),
        in_specs=[pl.BlockSpec((tm, d_model), row), pl.BlockSpec((tm, d_model), row),
                  pl.BlockSpec((tm, d_model), row), pl.BlockSpec((tm, out_b.shape[1]), row),
                  _const_spec(wts['w_b'].shape), _const_spec(wts['w_o'].shape),
                  _const_spec((1, d_model)), _const_spec(wts['w_r'].shape), _const_spec((1, PLE_LANES))],
        out_specs=[pl.BlockSpec((tm, d_model), row), pl.BlockSpec((tm, d_model // 2), row),
                   pl.BlockSpec((tm, PLE_LANES), row)],
        out_shape=[jax.ShapeDtypeStruct((n, d_model), F32),
                   jax.ShapeDtypeStruct((n, d_model // 2), jnp.uint32),
                   jax.ShapeDtypeStruct((n, PLE_LANES), F32)],
        name='post_attn',
        compiler_params=pltpu.CompilerParams(dimension_semantics=('arbitrary',),
                                             vmem_limit_bytes=VMEM_LIMIT),
    )(x, aterm, sgb, out_b, wts['w_b'], wts['w_o'], wts['g_ffn'], wts['w_r'], wts['b_r'])


def _dispatch_kernel(dest_ref, hf_ref, xd_in_ref, xd_ref, sem, *, fanout):
    del xd_in_ref
    tm = hf_ref.shape[0]

    def issue(rr, carry):
        for kk in range(fanout):
            pltpu.make_async_copy(hf_ref.at[pl.ds(rr, 1)],
                                  xd_ref.at[pl.ds(dest_ref[0, 0, rr * fanout + kk], 1)], sem).start()
        return carry

    lax.fori_loop(0, tm, issue, 0, unroll=DMA_ISSUE_UNROLL)
    for kk in range(fanout):
        pltpu.make_async_copy(hf_ref, xd_ref.at[pl.ds(0, tm)], sem).wait()


def _dispatch(hf, dest, n_rows):
    n, w = hf.shape
    fanout = dest.shape[0] // n
    tm = min(n, 512)
    steps = n // tm
    dest3 = dest.reshape(steps, 1, tm * fanout)
    xd0 = jnp.zeros((n_rows, w), hf.dtype)
    kern = functools.partial(_dispatch_kernel, fanout=fanout)
    return pl.pallas_call(
        kern, grid=(steps,),
        in_specs=[pl.BlockSpec((1, 1, tm * fanout), lambda i: (i, 0, 0), memory_space=pltpu.SMEM),
                  pl.BlockSpec((tm, w), lambda i: (i, 0)), pl.BlockSpec(memory_space=pl.ANY)],
        out_specs=pl.BlockSpec(memory_space=pl.ANY),
        out_shape=jax.ShapeDtypeStruct((n_rows, w), hf.dtype),
        scratch_shapes=[pltpu.SemaphoreType.DMA(())],
        input_output_aliases={2: 0},
        name='dispatch',
        compiler_params=pltpu.CompilerParams(dimension_semantics=('arbitrary',)),
    )(dest3, hf, xd0)


def _moe_ffn_kernel(be_ref, nreal_ref, xd_ref, w1_ref, w3_ref, w2_ref, yd_ref):
    del be_ref
    i = pl.program_id(0)

    @pl.when(i < nreal_ref[0])
    def _():
        xb = _unpack_bf16_pairs(xd_ref[...])
        a = _dot(xb, w1_ref[...])
        b = _dot(xb, w3_ref[...])
        yd_ref[...] = _dot((jax.nn.silu(a) * b).astype(BF16), w2_ref[...])

    @pl.when(i >= nreal_ref[0])
    def _():
        yd_ref[...] = jnp.zeros_like(yd_ref)


def _moe_ffn(xd, blk_e, n_real, w1, w3, w2):
    n_rows, half = xd.shape
    d_model = 2 * half
    ff = w1.shape[-1]
    blk = MOE_BLOCK
    grid_spec = pltpu.PrefetchScalarGridSpec(
        num_scalar_prefetch=2, grid=(n_rows // blk,),
        in_specs=[pl.BlockSpec((blk, half), lambda i, be, nr: (i, 0)),
                  pl.BlockSpec((None, d_model, ff), lambda i, be, nr: (be[i], 0, 0)),
                  pl.BlockSpec((None, d_model, ff), lambda i, be, nr: (be[i], 0, 0)),
                  pl.BlockSpec((None, ff, d_model), lambda i, be, nr: (be[i], 0, 0))],
        out_specs=pl.BlockSpec((blk, d_model), lambda i, be, nr: (i, 0)),
    )
    return pl.pallas_call(
        _moe_ffn_kernel, grid_spec=grid_spec,
        out_shape=jax.ShapeDtypeStruct((n_rows, d_model), F32),
        name='moe_ffn',
        compiler_params=pltpu.CompilerParams(dimension_semantics=('arbitrary',),
                                             vmem_limit_bytes=VMEM_LIMIT),
    )(blk_e, n_real, xd, w1, w3, w2)


def _final_kernel(d0_ref, dn_ref, yd_ref, x1_ref, route_ref, p_ref, gple_ref, wpg_ref, wple_ref,
                  y_ref, ybuf, sems, *, fanout):
    tm = x1_ref.shape[0]
    step = pl.program_id(0)
    slot = step % 2

    def start_rows(idx_ref, s):
        def issue(rr, carry):
            for kk in range(fanout):
                pltpu.make_async_copy(yd_ref.at[pl.ds(idx_ref[0, 0, rr * fanout + kk], 1)],
                                      ybuf.at[s, kk, pl.ds(rr, 1)], sems.at[s]).start()
            return carry

        lax.fori_loop(0, tm, issue, 0, unroll=DMA_ISSUE_UNROLL)

    def wait_rows(s):
        for kk in range(fanout):
            pltpu.make_async_copy(yd_ref.at[pl.ds(0, tm)], ybuf.at[s, kk], sems.at[s]).wait()

    @pl.when(step == 0)
    def _():
        start_rows(d0_ref, 0)

    start_rows(dn_ref, 1 - slot)
    wait_rows(slot)

    route = route_ref[...]
    moe = route[:, 2:3] * ybuf[slot, 0]
    for kk in range(1, fanout):
        moe = moe + route[:, 2 + kk:3 + kk] * ybuf[slot, kk]
    x2 = x1_ref[...] + moe
    gate = jax.nn.sigmoid(_dot(_rms(x2, gple_ref[...]).astype(BF16), wpg_ref[...]))
    y_ref[...] = x2 + gate * _dot(p_ref[...].astype(BF16), wple_ref[...])

    @pl.when(step == pl.num_programs(0) - 1)
    def _():
        wait_rows(1 - slot)


def _final(x1, yd, dest, route, p, wts):
    n, d_model = x1.shape
    fanout = dest.shape[0] // n
    tm = ROW_TILE
    steps = n // tm
    dest3 = dest.reshape(steps, 1, tm * fanout)
    row = lambda i: (i, 0)
    kern = functools.partial(_final_kernel, fanout=fanout)
    return pl.pallas_call(
        kern, grid=(steps,),
        in_specs=[pl.BlockSpec((1, 1, tm * fanout), lambda i: (0, 0, 0), memory_space=pltpu.SMEM),
                  pl.BlockSpec((1, 1, tm * fanout), lambda i: (jnp.minimum(i + 1, steps - 1), 0, 0),
                               memory_space=pltpu.SMEM),
                  pl.BlockSpec(memory_space=pl.ANY),
                  pl.BlockSpec((tm, d_model), row), pl.BlockSpec((tm, PLE_LANES), row),
                  pl.BlockSpec((tm, p.shape[1]), row),
                  _const_spec((1, d_model)), _const_spec(wts['w_pg'].shape), _const_spec(wts['w_ple'].shape)],
        out_specs=pl.BlockSpec((tm, d_model), row),
        out_shape=jax.ShapeDtypeStruct((n, d_model), F32),
        scratch_shapes=[pltpu.VMEM((2, fanout, tm, d_model), F32), pltpu.SemaphoreType.DMA((2,))],
        name='final',
        compiler_params=pltpu.CompilerParams(dimension_semantics=('arbitrary',),
                                             vmem_limit_bytes=VMEM_LIMIT),
    )(dest3, dest3, yd, x1, route, p, wts['g_ple'], wts['w_pg'], wts['w_ple'])


def _routing_plan(eid, blk):
    n_exp = N_GROUPS * EXPERTS_PER_GROUP
    e = eid.reshape(-1)
    n_assign = e.shape[0]
    onehot = (e[:, None] == jnp.arange(n_exp, dtype=jnp.int32)[None, :]).astype(jnp.int32)
    counts = jnp.sum(onehot, axis=0)
    rank = jnp.take_along_axis(jnp.cumsum(onehot, axis=0), e[:, None], axis=1)[:, 0] - 1
    pcounts = ((counts + blk - 1) // blk) * blk
    pend = jnp.cumsum(pcounts)
    pstart = pend - pcounts
    dest = (pstart[e] + rank).astype(jnp.int32)
    n_blocks = -(-(n_assign + n_exp * (blk - 1)) // blk)
    blk_start = jnp.arange(n_blocks, dtype=jnp.int32) * blk
    blk_e = jnp.minimum(jnp.sum(pend[None, :] <= blk_start[:, None], axis=1), n_exp - 1).astype(jnp.int32)
    n_real = (pend[-1:] // blk).astype(jnp.int32)
    return dest, blk_e, n_real, n_blocks * blk


def _rope_tables(pos, n_heads):
    half = HEAD_DIM // 2
    inv = ROPE_THETA ** (-jnp.arange(half, dtype=F32) / half)
    ang = pos.astype(F32)[:, None] * inv[None, :]
    cos = jnp.cos(ang)
    sin = jnp.sin(ang)
    return (jnp.tile(jnp.concatenate([cos, cos], axis=-1), (1, n_heads)),
            jnp.tile(jnp.concatenate([-sin, sin], axis=-1), (1, n_heads)))


def _mix_tables(w_s_l, b_s_l, t_mix, a_width):
    reps = CHUNK // t_mix
    tri = jnp.tril(jnp.ones((t_mix, t_mix), F32))
    wt = w_s_l[:, :t_mix, :t_mix] * tri[None]
    eye = jnp.eye(reps, dtype=F32)
    wbig = jnp.einsum('ab,gts->gatbs', eye, wt).reshape(A_GROUPS, CHUNK, CHUNK)
    wmix = jnp.transpose(wbig, (1, 0, 2)).reshape(CHUNK, A_GROUPS * CHUNK)
    bs = jnp.tile(b_s_l[:, :t_mix], (1, reps))
    bs_tab = jnp.repeat(bs.T, a_width // A_GROUPS, axis=1)
    return dict(wmix=wmix.astype(BF16), bs_tab=bs_tab)


def _layer_weights(l, g_mix, w_in, g_v, g_q, g_k, w_a, w_b, w_o, g_ffn, w_rg, b_rg, w_re, b_re,
                   w1, w3, w2, g_ple, w_pg, w_ple):
    b_width = w_b.shape[1]
    n_heads = b_width // HEAD_DIM
    d_model = w_o.shape[-1]
    hid = jnp.arange(b_width) // HEAD_DIM
    hind = jnp.where(hid[:, None] == hid[None, :], 1.0 / HEAD_DIM, 0.0)
    n_exp = N_GROUPS * EXPERTS_PER_GROUP
    w_r = jnp.zeros((d_model, PLE_LANES), F32)
    w_r = w_r.at[:, :N_GROUPS].set(w_rg[l]).at[:, N_GROUPS:N_GROUPS + n_exp].set(w_re[l])
    b_r = jnp.zeros((1, PLE_LANES), F32)
    b_r = b_r.at[0, :N_GROUPS].set(b_rg[l]).at[0, N_GROUPS:N_GROUPS + n_exp].set(b_re[l])
    return dict(
        g_mix=g_mix[l][None], w_in=w_in[l].astype(BF16), g_v=g_v[l][None], hind=hind.astype(BF16),
        g_q=jnp.tile(g_q[l], n_heads)[None], g_k=jnp.tile(g_k[l], n_heads)[None],
        w_a=w_a[l].astype(BF16), w_b=w_b[l].astype(BF16), w_o=w_o[l].astype(BF16),
        g_ffn=g_ffn[l][None], w_r=w_r.astype(BF16), b_r=b_r,
        w1=w1[l].astype(BF16), w3=w3[l].astype(BF16), w2=w2[l].astype(BF16),
        g_ple=g_ple[l][None], w_pg=w_pg[l].astype(BF16), w_ple=w_ple[l].astype(BF16),
    )


def _finish(x, aterm, sgb, out_b, p, wts):
    x1, hf, route = _post_attn(x, aterm, sgb, out_b, wts)
    eid = route[:, :2].astype(jnp.int32)
    dest, blk_e, n_real, n_rows = _routing_plan(eid, MOE_BLOCK)
    xd = _dispatch(hf, dest, n_rows)
    yd = _moe_ffn(xd, blk_e, n_real, wts['w1'], wts['w3'], wts['w2'])
    return _final(x1, yd, dest, route, p, wts)


def kernel(x_prompt, x_sample, cache_k, cache_v, page_table, p_prompt, p_sample, g_mix, w_in, g_v, w_s, b_s, g_q, g_k, w_a, w_b, w_o, g_ffn, w_router_group, b_router_group, w_router_expert, b_router_expert, w1, w3, w2, g_ple, w_ple_gate, w_ple):
    bsz, seq, d_model = x_prompt.shape
    dbsz, dseq, _ = x_sample.shape
    depth = g_mix.shape[0]
    b_width = w_b.shape[1]
    n_heads = b_width // HEAD_DIM
    past_len = page_table.shape[1] * PAGE_SIZE
    assert seq % MOBA_BLOCK == 0 and (bsz * seq) % ROW_TILE == 0 and (dbsz * dseq) % ROW_TILE == 0
    assert CHUNK % dseq == 0 and ROW_TILE % CHUNK == 0 and ROW_TILE % MOBA_BLOCK == 0
    assert seq // MOBA_BLOCK >= MOBA_TOPK
    params = (g_mix, w_in, g_v, g_q, g_k, w_a, w_b, w_o, g_ffn, w_router_group, b_router_group,
              w_router_expert, b_router_expert, w1, w3, w2, g_ple, w_ple_gate, w_ple)
    a_width = g_v.shape[-1]
    tab_p = _rope_tables(jnp.arange(seq, dtype=jnp.int32), n_heads)
    pos_s = past_len + (jnp.arange(ROW_TILE, dtype=jnp.int32) % dseq)
    tab_s = _rope_tables(pos_s, n_heads)
    xp = x_prompt.reshape(bsz * seq, d_model)
    xs = x_sample.reshape(dbsz * dseq, d_model)
    kp_rows, vp_rows, ks_rows, vs_rows, chunk_rows = [], [], [], [], []
    for l in range(depth):
        wts = _layer_weights(l, *params)
        wts_p = dict(wts, **_mix_tables(w_s[l], b_s[l], CHUNK, a_width))
        wts_s = dict(wts, **_mix_tables(w_s[l], b_s[l], dseq, a_width))
        aterm, sgb, qt, k, v, kb, vt, kmean = _inproj(
            xp, tab_p, wts_p, emit_kmean=True, emit_vchunk=False, pos_blocks=seq // ROW_TILE)
        nbt = bsz * seq // MOBA_BLOCK
        out_b = _moba_prompt(qt, kb.reshape(nbt, MOBA_BLOCK, b_width), vt,
                             kmean.reshape(bsz, seq // MOBA_BLOCK, b_width), bsz, seq)
        xp = _finish(xp, aterm, sgb, out_b, p_prompt[l].reshape(bsz * seq, -1), wts_p)
        kp_rows.append(jnp.transpose(k.reshape(bsz, n_heads, HEAD_DIM, seq), (0, 3, 1, 2)))
        vp_rows.append(jnp.transpose(v.reshape(bsz, n_heads, HEAD_DIM, seq), (0, 3, 1, 2)))
        aterm, sgb, q, k, v, kb, vb, vchunk = _inproj(
            xs, tab_s, wts_s, emit_kmean=False, emit_vchunk=True, pos_blocks=1)
        out_b = _moba_sample(q, kb, vb, cache_k[l], cache_v[l], page_table, dseq)
        xs = _finish(xs, aterm, sgb, out_b, p_sample[l].reshape(dbsz * dseq, -1), wts_s)
        ks_rows.append(k.reshape(dbsz, dseq, n_heads, HEAD_DIM))
        vs_rows.append(v.reshape(dbsz, dseq, n_heads, HEAD_DIM))
        chunk_rows.append(vchunk.reshape(dbsz, dseq, -1))
    return (xp.reshape(bsz, seq, d_model), xs.reshape(dbsz, dseq, d_model),
            jnp.stack(kp_rows), jnp.stack(vp_rows), jnp.stack(ks_rows), jnp.stack(vs_rows),
            jnp.stack(chunk_rows))
```

```python
import functools

import jax
import jax.numpy as jnp
from jax import lax
from jax.experimental import pallas as pl
from jax.experimental.pallas import tpu as pltpu

F32 = jnp.float32
BF16 = jnp.bfloat16

EPS = 1e-6
NEG_INF = -1e30
A_GROUPS = 8
CHUNK = 128
HEAD_DIM = 64
MOBA_BLOCK = 256
MOBA_TOPK = 3
ROPE_THETA = 10000.0
N_GROUPS = 4
EXPERTS_PER_GROUP = 8
PAGE_SIZE = 128
PLE_LANES = 128
ROW_TILE = 256
MOE_BLOCK = 256
PAGES_PER_STEP = 16
VMEM_LIMIT = 56 * 1024 * 1024
LOG2_E = 1.4426950408889634
ONES_ROWS = 16
DMA_ISSUE_UNROLL = 8

_NT = (((1,), (1,)), ((), ()))


def _rms(x, g):
    return x * lax.rsqrt(jnp.mean(x * x, axis=-1, keepdims=True) + EPS) * g


def _dot(a, b):
    return jnp.dot(a, b, preferred_element_type=F32)


def _dot_nt(a, b):
    return lax.dot_general(a, b, _NT, preferred_element_type=F32)


def _dot_tn(a, b):
    return lax.dot_general(a, b, (((0,), (0,)), ((), ())), preferred_element_type=F32)


def _const_spec(shape):
    return pl.BlockSpec(shape, lambda *_: (0,) * len(shape))


def _inproj_kernel(x_ref, gmix_ref, win_ref, gv_ref, wmix_ref, bs_ref, hind_ref, gq_ref, gk_ref,
                   cos_ref, sin_ref, wa_ref,
                   aterm_ref, sgb_ref, q_ref, k_ref, v_ref, kb_ref, vb_ref, *extra,
                   a_width, b_width, d_model, emit_kmean, emit_vchunk, transpose_qv):
    tm = x_ref.shape[0]
    h = _rms(x_ref[...], gmix_ref[...]).astype(BF16)
    offs = [0]

    def proj(width):
        o = offs[0]
        offs[0] = o + width
        return _dot(h, win_ref[:, o:o + width])

    zu = proj(a_width)
    zv = proj(a_width)
    zq = proj(b_width)
    zk = proj(b_width)
    zva = proj(b_width)

    u = jax.nn.gelu(zu)
    vn = _rms(jax.nn.gelu(zv), gv_ref[...])
    vb16 = vn.astype(BF16)
    lane_grp = lax.broadcasted_iota(jnp.int32, (CHUNK, a_width), 1) // (a_width // A_GROUPS)
    parts = []
    for c in range(tm // CHUNK):
        vc = vb16[c * CHUNK:(c + 1) * CHUNK, :]
        rhs = jnp.concatenate(
            [jnp.where(lane_grp == g, vc, jnp.zeros_like(vc)) for g in range(A_GROUPS)], axis=0)
        parts.append(_dot(wmix_ref[...], rhs) + bs_ref[...])
    s = parts[0] if len(parts) == 1 else jnp.concatenate(parts, axis=0)
    out_a = (u * s).astype(BF16)
    ga = proj(d_model)
    aterm_ref[...] = jax.nn.sigmoid(ga) * _dot(out_a, wa_ref[...])
    gb = proj(d_model)
    sgb_ref[...] = jax.nn.sigmoid(gb)

    lane = lax.broadcasted_iota(jnp.int32, (tm, b_width), 1)
    first_half = (lane % HEAD_DIM) < (HEAD_DIM // 2)
    cos = cos_ref[...]
    sin = sin_ref[...]

    def headnorm_rope(z, g):
        ms = _dot((z * z).astype(BF16), hind_ref[...])
        y = z * lax.rsqrt(ms + EPS) * g
        swapped = jnp.where(first_half,
                            pltpu.roll(y, b_width - HEAD_DIM // 2, 1),
                            pltpu.roll(y, HEAD_DIM // 2, 1))
        return y * cos + swapped * sin

    q = headnorm_rope(zq, gq_ref[...])
    k = headnorm_rope(zk, gk_ref[...])
    kb_ref[...] = k.astype(BF16)
    if transpose_qv:
        vt = zva.T
        q_ref[0] = q.T.astype(BF16)
        k_ref[0] = k.T
        v_ref[0] = vt
        vb_ref[0] = vt.astype(BF16)
    else:
        q_ref[...] = q.astype(BF16)
        k_ref[...] = k
        v_ref[...] = zva
        vb_ref[...] = zva.astype(BF16)
    idx = 0
    if emit_kmean:
        km_ref = extra[idx]
        idx += 1
        for bi in range(tm // MOBA_BLOCK):
            km_ref[bi] = jnp.mean(k[bi * MOBA_BLOCK:(bi + 1) * MOBA_BLOCK, :], axis=0, keepdims=True)
    if emit_vchunk:
        extra[idx][...] = vn


def _inproj(x, pos_tables, wts, *, emit_kmean, emit_vchunk, pos_blocks):
    n, d_model = x.shape
    a_width = wts['g_v'].shape[-1]
    b_width = wts['hind'].shape[0]
    tm = ROW_TILE
    cos_t, sin_t = pos_tables
    row = lambda i: (i, 0)
    posrow = lambda i: (i % pos_blocks, 0)
    in_specs = [
        pl.BlockSpec((tm, d_model), row),
        _const_spec((1, d_model)),
        _const_spec(wts['w_in'].shape),
        _const_spec((1, a_width)),
        _const_spec(wts['wmix'].shape),
        _const_spec(wts['bs_tab'].shape),
        _const_spec(wts['hind'].shape),
        _const_spec((1, b_width)),
        _const_spec((1, b_width)),
        pl.BlockSpec((tm, b_width), posrow),
        pl.BlockSpec((tm, b_width), posrow),
        _const_spec(wts['w_a'].shape),
    ]
    transpose_qv = emit_kmean
    if transpose_qv:
        assert tm == MOBA_BLOCK
        qv_shape = jax.ShapeDtypeStruct((n // tm, b_width, tm), BF16)
        qv_spec = pl.BlockSpec((1, b_width, tm), lambda i: (i, 0, 0))
        kv_shape = jax.ShapeDtypeStruct((n // (tm * pos_blocks), b_width, tm * pos_blocks), F32)
        kv_spec = pl.BlockSpec((1, b_width, tm), lambda i: (i // pos_blocks, 0, i % pos_blocks))
    else:
        qv_shape = jax.ShapeDtypeStruct((n, b_width), BF16)
        qv_spec = pl.BlockSpec((tm, b_width), row)
        kv_shape = jax.ShapeDtypeStruct((n, b_width), F32)
        kv_spec = pl.BlockSpec((tm, b_width), row)
    out_shape = [
        jax.ShapeDtypeStruct((n, d_model), F32),
        jax.ShapeDtypeStruct((n, d_model), F32),
        qv_shape,
        kv_shape,
        kv_shape,
        jax.ShapeDtypeStruct((n, b_width), BF16),
        qv_shape,
    ]
    out_specs = [
        pl.BlockSpec((tm, d_model), row), pl.BlockSpec((tm, d_model), row),
        qv_spec, kv_spec, kv_spec, pl.BlockSpec((tm, b_width), row),
        qv_spec,
    ]
    if emit_kmean:
        nbt = tm // MOBA_BLOCK
        out_shape.append(jax.ShapeDtypeStruct((n // MOBA_BLOCK, 1, b_width), F32))
        out_specs.append(pl.BlockSpec((nbt, 1, b_width), lambda i: (i, 0, 0)))
    if emit_vchunk:
        out_shape.append(jax.ShapeDtypeStruct((n, a_width), F32))
        out_specs.append(pl.BlockSpec((tm, a_width), row))
    kern = functools.partial(_inproj_kernel, a_width=a_width, b_width=b_width, d_model=d_model,
                             emit_kmean=emit_kmean, emit_vchunk=emit_vchunk, transpose_qv=transpose_qv)
    return pl.pallas_call(
        kern, grid=(n // tm,), in_specs=in_specs, out_specs=out_specs, out_shape=out_shape,
        name='inproj',
        compiler_params=pltpu.CompilerParams(dimension_semantics=('arbitrary',),
                                             vmem_limit_bytes=VMEM_LIMIT),
    )(x, wts['g_mix'], wts['w_in'], wts['g_v'], wts['wmix'], wts['bs_tab'], wts['hind'],
      wts['g_q'], wts['g_k'], cos_t, sin_t, wts['w_a'])


def _select_topk(scores, allowed, blk_f, nb):
    sel = jnp.zeros(scores.shape, jnp.bool_)
    for _ in range(MOBA_TOPK):
        cand = jnp.logical_and(allowed, jnp.logical_not(sel))
        scm = jnp.where(cand, scores, -jnp.inf)
        mx = jnp.max(scm, axis=-1, keepdims=True)
        is_max = jnp.logical_and(cand, scm == mx)
        first = jnp.min(jnp.where(is_max, blk_f, float(nb)), axis=-1, keepdims=True)
        sel = jnp.logical_or(sel, jnp.logical_and(is_max, blk_f == first))
    return sel


def _select_topk_rows(scores, allowed, blk_f, nb):
    sel = jnp.zeros(scores.shape, jnp.bool_)
    for _ in range(MOBA_TOPK):
        cand = jnp.logical_and(allowed, jnp.logical_not(sel))
        scm = jnp.where(cand, scores, -jnp.inf)
        mx = jnp.max(scm, axis=0, keepdims=True)
        is_max = jnp.logical_and(cand, scm == mx)
        first = jnp.min(jnp.where(is_max, blk_f, float(nb)), axis=0, keepdims=True)
        sel = jnp.logical_or(sel, jnp.logical_and(is_max, blk_f == first))
    return sel


def _moba_prompt_kernel(qt_ref, kb_ref, vt_ref, km_ref, o_ref, w_ref, bias_ref, m_ref, l_ref, acc_ref,
                        s_ref, *, n_heads, nb):
    tq = qt_ref.shape[2]
    blk = MOBA_BLOCK
    pair = 2 * HEAD_DIM
    i = pl.program_id(1)
    scale = HEAD_DIM ** -0.5 * LOG2_E
    km = km_ref[0].astype(BF16)
    blk_f = lax.broadcasted_iota(jnp.int32, (nb, tq), 0).astype(F32)
    allowed = blk_f < i.astype(F32)
    key_t = lax.broadcasted_iota(jnp.int32, (blk, tq), 0)
    qry_t = lax.broadcasted_iota(jnp.int32, (blk, tq), 1)
    causal = key_t <= qry_t
    zeros = jnp.zeros((HEAD_DIM, tq), BF16)
    ones = jnp.ones((ONES_ROWS, blk), BF16)

    for h in range(n_heads):
        qth = qt_ref[0, h * HEAD_DIM:(h + 1) * HEAD_DIM, :]
        sel = _select_topk_rows(_dot(km[:, h * HEAD_DIM:(h + 1) * HEAD_DIM], qth), allowed, blk_f, nb)
        bias_ref[h] = jnp.where(sel, 0.0, NEG_INF)
        qs = (qth.astype(F32) * scale).astype(BF16)
        col = jnp.concatenate([qs, zeros] if h % 2 == 0 else [zeros, qs], axis=0)
        w_ref[h // 2, :, (h % 2) * tq:(h % 2 + 1) * tq] = col

    def scores(j, hp):
        return _dot(kb_ref[j, :, hp * pair:(hp + 1) * pair], w_ref[hp])

    def weighted_values(j, h, p):
        vt1 = jnp.concatenate([vt_ref[j, h * HEAD_DIM:(h + 1) * HEAD_DIM, :], ones], axis=0)
        pv = _dot(vt1, p.astype(BF16))
        return pv[:HEAD_DIM, :], pv[HEAD_DIM:HEAD_DIM + 1, :]

    for hp in range(n_heads // 2):
        s2 = scores(i, hp)
        for h in (2 * hp, 2 * hp + 1):
            s = jnp.where(causal, s2[:, (h % 2) * tq:(h % 2 + 1) * tq], NEG_INF)
            m = jnp.max(s, axis=0, keepdims=True)
            pv, psum = weighted_values(i, h, jnp.exp2(s - m))
            m_ref[h:h + 1, :] = m
            l_ref[h:h + 1, :] = psum
            acc_ref[h * HEAD_DIM:(h + 1) * HEAD_DIM, :] = pv

    def stage_scores(j, slot):
        jc = jnp.minimum(j, nb - 1)
        for hp in range(n_heads // 2):
            s_ref[slot, hp] = scores(jc, hp)

    def consume(j, slot):
        jc = jnp.minimum(j, nb - 1)
        for h in range(n_heads):
            s = s_ref[slot, h // 2, :, (h % 2) * tq:(h % 2 + 1) * tq]
            bias = bias_ref[h, pl.ds(jc, 1), :]
            m = m_ref[h:h + 1, :]
            m_new = jnp.maximum(m, jnp.max(s, axis=0, keepdims=True) + bias)
            alpha = jnp.exp2(m - m_new)
            pv, psum = weighted_values(jc, h, jnp.exp2(s + (bias - m_new)))
            m_ref[h:h + 1, :] = m_new
            l_ref[h:h + 1, :] = alpha * l_ref[h:h + 1, :] + psum
            rows = slice(h * HEAD_DIM, (h + 1) * HEAD_DIM)
            acc_ref[rows, :] = alpha * acc_ref[rows, :] + pv

    @pl.when(i > 0)
    def _():
        stage_scores(0, 0)

        def body(t, carry):
            j = 2 * t
            stage_scores(j + 1, 1)
            consume(j, 0)
            stage_scores(j + 2, 0)
            consume(j + 1, 1)
            return carry

        lax.fori_loop(0, (i + 1) // 2, body, 0)
    outs = [acc_ref[h * HEAD_DIM:(h + 1) * HEAD_DIM, :] / l_ref[h:h + 1, :] for h in range(n_heads)]
    o_ref[...] = jnp.concatenate(outs, axis=0).T.astype(o_ref.dtype)


def _moba_prompt(qt, kb, vt, kmean, bsz, seq):
    _, width, tq = qt.shape
    n_heads = width // HEAD_DIM
    nb = seq // MOBA_BLOCK
    assert tq == MOBA_BLOCK and n_heads % 2 == 0
    kern = functools.partial(_moba_prompt_kernel, n_heads=n_heads, nb=nb)
    return pl.pallas_call(
        kern, grid=(bsz, nb),
        in_specs=[
            pl.BlockSpec((1, width, tq), lambda b, i: (b * nb + i, 0, 0)),
            pl.BlockSpec((nb, MOBA_BLOCK, width), lambda b, i: (b, 0, 0)),
            pl.BlockSpec((nb, width, MOBA_BLOCK), lambda b, i: (b, 0, 0)),
            pl.BlockSpec((1, nb, width), lambda b, i: (b, 0, 0)),
        ],
        out_specs=pl.BlockSpec((tq, width), lambda b, i: (b * nb + i, 0)),
        out_shape=jax.ShapeDtypeStruct((bsz * seq, width), BF16),
        scratch_shapes=[pltpu.VMEM((n_heads // 2, 2 * HEAD_DIM, 2 * tq), BF16),
                        pltpu.VMEM((n_heads, nb, tq), F32),
                        pltpu.VMEM((n_heads, tq), F32), pltpu.VMEM((n_heads, tq), F32),
                        pltpu.VMEM((width, tq), F32),
                        pltpu.VMEM((2, n_heads // 2, MOBA_BLOCK, 2 * tq), F32)],
        name='moba_prompt',
        compiler_params=pltpu.CompilerParams(dimension_semantics=('arbitrary', 'arbitrary'),
                                             vmem_limit_bytes=VMEM_LIMIT),
    )(qt, kb, vt, kmean)


def _moba_sample_kernel(pt_ref, q_ref, kn_ref, vn_ref, *rest, pg, n_heads, nb):
    del pt_ref
    k_refs = rest[:pg]
    v_refs = rest[pg:2 * pg]
    o_ref = rest[2 * pg]
    bs_ref, m_ref, l_ref, acc_ref = rest[2 * pg + 1:]
    ds, width = q_ref.shape
    step = pl.program_id(1)
    r = n_heads * ds
    c2 = HEAD_DIM ** -0.5 * LOG2_E
    ppb = MOBA_BLOCK // PAGE_SIZE
    row_h = lax.broadcasted_iota(jnp.int32, (r, width), 0) // ds
    lane_h = lax.broadcasted_iota(jnp.int32, (r, width), 1) // HEAD_DIM
    hmask = row_h == lane_h
    qt = jnp.concatenate([q_ref[...]] * n_heads, axis=0)
    qbd = jnp.where(hmask, qt, jnp.zeros_like(qt))
    lane_b = lax.broadcasted_iota(jnp.int32, (r, nb), 1)

    @pl.when(step == 0)
    def _():
        bs_ref[...] = jnp.zeros_like(bs_ref)
        m_ref[...] = jnp.zeros_like(m_ref)
        l_ref[...] = jnp.zeros_like(l_ref)

    bs_new, m_new, l_new = bs_ref[...], m_ref[...], l_ref[...]
    kall = jnp.concatenate([k_refs[t][...] for t in range(pg)], axis=1).astype(BF16)
    st_all = _dot(qbd, kall)
    for c in range(pg // ppb):
        st = st_all[:, c * MOBA_BLOCK:(c + 1) * MOBA_BLOCK]
        vblk = jnp.concatenate([v_refs[ppb * c + t][...] for t in range(ppb)], axis=1).astype(BF16)
        s = st * c2
        m = jnp.max(s, axis=-1, keepdims=True)
        p = jnp.exp2(s - m)
        jb = step * (pg // ppb) + c
        col = lane_b == jb
        bs_new = jnp.where(col, jnp.mean(st, axis=-1, keepdims=True), bs_new)
        m_new = jnp.where(col, m, m_new)
        l_new = jnp.where(col, jnp.sum(p, axis=-1, keepdims=True), l_new)
        acc_ref[jb] = _dot_nt(p.astype(BF16), vblk)
    bs_ref[...] = bs_new
    m_ref[...] = m_new
    l_ref[...] = l_new

    @pl.when(step == pl.num_programs(1) - 1)
    def _():
        sel = _select_topk(bs_ref[...], jnp.ones((r, nb), jnp.bool_), lane_b.astype(F32), nb)
        key_t = lax.broadcasted_iota(jnp.int32, (r, ds), 1)
        qry_t = lax.broadcasted_iota(jnp.int32, (r, ds), 0) % ds
        s_own = jnp.where(key_t <= qry_t, _dot_nt(qbd, kn_ref[...]) * c2, NEG_INF)
        m_all = m_ref[...]
        m_tot = jnp.maximum(jnp.max(jnp.where(sel, m_all, NEG_INF), axis=-1, keepdims=True),
                            jnp.max(s_own, axis=-1, keepdims=True))
        w = jnp.where(sel, jnp.exp2(m_all - m_tot), 0.0)
        p_own = jnp.exp2(s_own - m_tot)
        den = jnp.sum(w * l_ref[...], axis=-1, keepdims=True) + jnp.sum(p_own, axis=-1, keepdims=True)
        num = _dot(p_own.astype(BF16), vn_ref[...])

        def body(j, num):
            wj = jnp.sum(jnp.where(lane_b == j, w, 0.0), axis=-1, keepdims=True)
            return num + wj * acc_ref[j]

        out = jnp.where(hmask, lax.fori_loop(0, nb, body, num) / den, 0.0)
        o = out[0:ds, :]
        for h in range(1, n_heads):
            o = o + out[h * ds:(h + 1) * ds, :]
        o_ref[...] = o.astype(o_ref.dtype)


def _moba_sample(q, kb, vb, cache_k, cache_v, page_table, ds):
    n, width = q.shape
    dbsz, n_pages = page_table.shape
    n_heads = width // HEAD_DIM
    assert (n_pages * PAGE_SIZE) % MOBA_BLOCK == 0, "cached length must fill whole MoBA blocks"
    nb = n_pages * PAGE_SIZE // MOBA_BLOCK
    assert nb >= MOBA_TOPK
    pg = PAGES_PER_STEP
    ppb = MOBA_BLOCK // PAGE_SIZE
    assert n_pages % pg == 0 and pg % ppb == 0 and ds % 8 == 0
    assert cache_k.shape[1:] == (PAGE_SIZE, n_heads, HEAD_DIM)
    r = n_heads * ds
    ck = jnp.transpose(cache_k, (0, 2, 3, 1)).reshape(cache_k.shape[0], width, PAGE_SIZE)
    cv = jnp.transpose(cache_v, (0, 2, 3, 1)).reshape(cache_v.shape[0], width, PAGE_SIZE)

    def page_spec(t):
        return pl.BlockSpec((None, width, PAGE_SIZE), lambda b, s, pt: (pt[b, s * pg + t], 0, 0))

    tok = pl.BlockSpec((ds, width), lambda b, s, pt: (b, 0))
    grid_spec = pltpu.PrefetchScalarGridSpec(
        num_scalar_prefetch=1, grid=(dbsz, n_pages // pg),
        in_specs=[tok, tok, tok] + [page_spec(t) for t in range(pg)] + [page_spec(t) for t in range(pg)],
        out_specs=tok,
        scratch_shapes=[pltpu.VMEM((r, nb), F32), pltpu.VMEM((r, nb), F32), pltpu.VMEM((r, nb), F32),
                        pltpu.VMEM((nb, r, width), F32)],
    )
    kern = functools.partial(_moba_sample_kernel, pg=pg, n_heads=n_heads, nb=nb)
    return pl.pallas_call(
        kern, grid_spec=grid_spec, out_shape=jax.ShapeDtypeStruct((n, width), BF16),
        name='moba_sample',
        compiler_params=pltpu.CompilerParams(dimension_semantics=('arbitrary', 'arbitrary'),
                                             vmem_limit_bytes=VMEM_LIMIT),
    )(page_table, q, kb, vb, *([ck] * pg), *([cv] * pg))


def _pack_bf16_pairs(x):
    w = x.shape[1] // 2
    xb = x.astype(BF16).astype(F32)
    hi = lax.bitcast_convert_type(xb[:, :w], jnp.uint32) & jnp.uint32(0xFFFF0000)
    lo = lax.bitcast_convert_type(xb[:, w:], jnp.uint32) >> 16
    return hi | lo


def _unpack_bf16_pairs(u):
    hi = lax.bitcast_convert_type(u & jnp.uint32(0xFFFF0000), F32)
    lo = lax.bitcast_convert_type(u << 16, F32)
    return jnp.concatenate([hi, lo], axis=1).astype(BF16)


def _post_attn_kernel(x_ref, aterm_ref, sgb_ref, ob_ref, wb_ref, wo_ref, gffn_ref, wr_ref, br_ref,
                      x1_ref, hf_ref, route_ref):
    merged = aterm_ref[...] + sgb_ref[...] * _dot(ob_ref[...], wb_ref[...])
    x1 = x_ref[...] + _dot(merged.astype(BF16), wo_ref[...])
    x1_ref[...] = x1
    hf = _rms(x1, gffn_ref[...])
    hf_ref[...] = _pack_bf16_pairs(hf)
    logits = _dot(hf.astype(BF16), wr_ref[...]) + br_ref[...]
    tm, lanes = logits.shape
    lane = lax.broadcasted_iota(jnp.int32, (tm, lanes), 1).astype(F32)
    n_exp = N_GROUPS * EXPERTS_PER_GROUP
    gmask = lane < N_GROUPS
    gl = jnp.where(gmask, logits, -jnp.inf)
    gmax = jnp.max(gl, axis=-1, keepdims=True)
    grp = jnp.min(jnp.where(gl == gmax, lane, float(lanes)), axis=-1, keepdims=True)
    p_grp = 1.0 / jnp.sum(jnp.where(gmask, jnp.exp(gl - gmax), 0.0), axis=-1, keepdims=True)
    lo = N_GROUPS + grp * EXPERTS_PER_GROUP
    emask = jnp.logical_and(jnp.logical_and(lane >= lo, lane < lo + EXPERTS_PER_GROUP),
                            lane < N_GROUPS + n_exp)
    e1 = jnp.where(emask, logits, -jnp.inf)
    v1 = jnp.max(e1, axis=-1, keepdims=True)
    j1 = jnp.min(jnp.where(e1 == v1, lane, float(lanes)), axis=-1, keepdims=True)
    e2 = jnp.where(lane == j1, -jnp.inf, e1)
    v2 = jnp.max(e2, axis=-1, keepdims=True)
    j2 = jnp.min(jnp.where(e2 == v2, lane, float(lanes)), axis=-1, keepdims=True)
    t = jnp.exp(v2 - v1)
    p1 = 1.0 / (1.0 + t)
    p2 = t / (1.0 + t)
    rec = jnp.where(lane == 0, j1 - N_GROUPS, 0.0)
    rec = jnp.where(lane == 1, j2 - N_GROUPS, rec)
    rec = jnp.where(lane == 2, p_grp * p1, rec)
    rec = jnp.where(lane == 3, p_grp * p2, rec)
    route_ref[...] = rec


def _post_attn(x, aterm, sgb, out_b, wts):
    n, d_model = x.shape
    tm = ROW_TILE
    row = lambda i: (i, 0)
    return pl.pallas_call(
        _post_attn_kernel, grid=(n // tm,),
        in_specs=[pl.BlockSpec((tm, d_model), row), pl.BlockSpec((tm, d_model), row),
                  pl.BlockSpec((tm, d_model), row), pl.BlockSpec((tm, out_b.shape[1]), row),
                  _const_spec(wts['w_b'].shape), _const_spec(wts['w_o'].shape),
                  _const_spec((1, d_model)), _const_spec(wts['w_r'].shape), _const_spec((1, PLE_LANES))],
        out_specs=[pl.BlockSpec((tm, d_model), row), pl.BlockSpec((tm, d_model // 2), row),
                   pl.BlockSpec((tm, PLE_LANES), row)],
        out_shape=[jax.ShapeDtypeStruct((n, d_model), F32),
                   jax.ShapeDtypeStruct((n, d_model // 2), jnp.uint32),
                   jax.ShapeDtypeStruct((n, PLE_LANES), F32)],
        name='post_attn',
        compiler_params=pltpu.CompilerParams(dimension_semantics=('arbitrary',),
                                             vmem_limit_bytes=VMEM_LIMIT),
    )(x, aterm, sgb, out_b, wts['w_b'], wts['w_o'], wts['g_ffn'], wts['w_r'], wts['b_r'])


def _dispatch_kernel(dest_ref, hf_ref, xd_in_ref, xd_ref, sem, *, fanout):
    del xd_in_ref
    tm = hf_ref.shape[0]

    for rr in range(tm):
        for kk in range(fanout):
            pltpu.make_async_copy(hf_ref.at[pl.ds(rr, 1)],
                                  xd_ref.at[pl.ds(dest_ref[0, 0, rr * fanout + kk], 1)], sem).start()
    for kk in range(fanout):
        pltpu.make_async_copy(hf_ref, xd_ref.at[pl.ds(0, tm)], sem).wait()


def _dispatch(hf, dest, n_rows):
    n, w = hf.shape
    fanout = dest.shape[0] // n
    tm = min(n, 512)
    steps = n // tm
    dest3 = dest.reshape(steps, 1, tm * fanout)
    xd0 = jnp.zeros((n_rows, w), hf.dtype)
    kern = functools.partial(_dispatch_kernel, fanout=fanout)
    return pl.pallas_call(
        kern, grid=(steps,),
        in_specs=[pl.BlockSpec((1, 1, tm * fanout), lambda i: (i, 0, 0), memory_space=pltpu.SMEM),
                  pl.BlockSpec((tm, w), lambda i: (i, 0)), pl.BlockSpec(memory_space=pl.ANY)],
        out_specs=pl.BlockSpec(memory_space=pl.ANY),
        out_shape=jax.ShapeDtypeStruct((n_rows, w), hf.dtype),
        scratch_shapes=[pltpu.SemaphoreType.DMA(())],
        input_output_aliases={2: 0},
        name='dispatch',
        compiler_params=pltpu.CompilerParams(dimension_semantics=('arbitrary',)),
    )(dest3, hf, xd0)


def _moe_ffn_kernel(be_ref, nreal_ref, xd_ref, w1_ref, w3_ref, w2_ref, yd_ref, w1b_ref, w3b_ref, w2b_ref):
    i = pl.program_id(0)
    live = i < nreal_ref[0]
    new_expert = jnp.logical_or(i == 0, be_ref[i] != be_ref[jnp.maximum(i - 1, 0)])

    @pl.when(jnp.logical_and(live, new_expert))
    def _():
        w1b_ref[...] = w1_ref[...].astype(BF16)
        w3b_ref[...] = w3_ref[...].astype(BF16)
        w2b_ref[...] = w2_ref[...].astype(BF16)

    @pl.when(live)
    def _():
        xb = _unpack_bf16_pairs(xd_ref[...])
        a = _dot(xb, w1b_ref[...])
        b = _dot(xb, w3b_ref[...])
        yd_ref[...] = _dot((jax.nn.silu(a) * b).astype(BF16), w2b_ref[...])

    @pl.when(i >= nreal_ref[0])
    def _():
        yd_ref[...] = jnp.zeros_like(yd_ref)


def _moe_ffn(xd, blk_e, n_real, w1, w3, w2):
    n_rows, half = xd.shape
    d_model = 2 * half
    ff = w1.shape[-1]
    blk = MOE_BLOCK
    grid_spec = pltpu.PrefetchScalarGridSpec(
        num_scalar_prefetch=2, grid=(n_rows // blk,),
        in_specs=[pl.BlockSpec((blk, half), lambda i, be, nr: (i, 0)),
                  pl.BlockSpec((None, d_model, ff), lambda i, be, nr: (be[i], 0, 0)),
                  pl.BlockSpec((None, d_model, ff), lambda i, be, nr: (be[i], 0, 0)),
                  pl.BlockSpec((None, ff, d_model), lambda i, be, nr: (be[i], 0, 0))],
        out_specs=pl.BlockSpec((blk, d_model), lambda i, be, nr: (i, 0)),
        scratch_shapes=[pltpu.VMEM((d_model, ff), BF16), pltpu.VMEM((d_model, ff), BF16),
                        pltpu.VMEM((ff, d_model), BF16)],
    )
    return pl.pallas_call(
        _moe_ffn_kernel, grid_spec=grid_spec,
        out_shape=jax.ShapeDtypeStruct((n_rows, d_model), F32),
        name='moe_ffn',
        compiler_params=pltpu.CompilerParams(dimension_semantics=('arbitrary',),
                                             vmem_limit_bytes=VMEM_LIMIT),
    )(blk_e, n_real, xd, w1, w3, w2)


def _final_kernel(d0_ref, dn_ref, yd_ref, x1_ref, route_ref, p_ref, gple_ref, wpg_ref, wple_ref,
                  y_ref, ybuf, sems, *, fanout):
    tm = x1_ref.shape[0]
    step = pl.program_id(0)
    slot = step % 2

    def start_rows(idx_ref, s, static_rows):
        def issue(rr, carry):
            for kk in range(fanout):
                pltpu.make_async_copy(yd_ref.at[pl.ds(idx_ref[0, 0, rr * fanout + kk], 1)],
                                      ybuf.at[s, kk, pl.ds(rr, 1)], sems.at[s]).start()
            return carry

        if static_rows:
            for rr in range(tm):
                issue(rr, 0)
        else:
            lax.fori_loop(0, tm, issue, 0, unroll=DMA_ISSUE_UNROLL)

    def wait_rows(s):
        for kk in range(fanout):
            pltpu.make_async_copy(yd_ref.at[pl.ds(0, tm)], ybuf.at[s, kk], sems.at[s]).wait()

    @pl.when(step == 0)
    def _():
        start_rows(d0_ref, 0, False)

    start_rows(dn_ref, 1 - slot, True)
    wait_rows(slot)

    route = route_ref[...]
    moe = route[:, 2:3] * ybuf[slot, 0]
    for kk in range(1, fanout):
        moe = moe + route[:, 2 + kk:3 + kk] * ybuf[slot, kk]
    x2 = x1_ref[...] + moe
    gate = jax.nn.sigmoid(_dot(_rms(x2, gple_ref[...]).astype(BF16), wpg_ref[...]))
    y_ref[...] = x2 + gate * _dot(p_ref[...].astype(BF16), wple_ref[...])

    @pl.when(step == pl.num_programs(0) - 1)
    def _():
        wait_rows(1 - slot)


def _final(x1, yd, dest, route, p, wts):
    n, d_model = x1.shape
    fanout = dest.shape[0] // n
    tm = ROW_TILE
    steps = n // tm
    dest3 = dest.reshape(steps, 1, tm * fanout)
    row = lambda i: (i, 0)
    kern = functools.partial(_final_kernel, fanout=fanout)
    return pl.pallas_call(
        kern, grid=(steps,),
        in_specs=[pl.BlockSpec((1, 1, tm * fanout), lambda i: (0, 0, 0), memory_space=pltpu.SMEM),
                  pl.BlockSpec((1, 1, tm * fanout), lambda i: (jnp.minimum(i + 1, steps - 1), 0, 0),
                               memory_space=pltpu.SMEM),
                  pl.BlockSpec(memory_space=pl.ANY),
                  pl.BlockSpec((tm, d_model), row), pl.BlockSpec((tm, PLE_LANES), row),
                  pl.BlockSpec((tm, p.shape[1]), row),
                  _const_spec((1, d_model)), _const_spec(wts['w_pg'].shape), _const_spec(wts['w_ple'].shape)],
        out_specs=pl.BlockSpec((tm, d_model), row),
        out_shape=jax.ShapeDtypeStruct((n, d_model), F32),
        scratch_shapes=[pltpu.VMEM((2, fanout, tm, d_model), F32), pltpu.SemaphoreType.DMA((2,))],
        name='final',
        compiler_params=pltpu.CompilerParams(dimension_semantics=('arbitrary',),
                                             vmem_limit_bytes=VMEM_LIMIT),
    )(dest3, dest3, yd, x1, route, p, wts['g_ple'], wts['w_pg'], wts['w_ple'])


def _routing_plan(eid, blk):
    n_exp = N_GROUPS * EXPERTS_PER_GROUP
    e = eid.reshape(-1)
    n_assign = e.shape[0]
    onehot = (e[:, None] == jnp.arange(n_exp, dtype=jnp.int32)[None, :]).astype(jnp.int32)
    counts = jnp.sum(onehot, axis=0)
    rank = jnp.take_along_axis(jnp.cumsum(onehot, axis=0), e[:, None], axis=1)[:, 0] - 1
    pcounts = ((counts + blk - 1) // blk) * blk
    pend = jnp.cumsum(pcounts)
    pstart = pend - pcounts
    dest = (pstart[e] + rank).astype(jnp.int32)
    n_blocks = -(-(n_assign + n_exp * (blk - 1)) // blk)
    blk_start = jnp.arange(n_blocks, dtype=jnp.int32) * blk
    blk_e = jnp.minimum(jnp.sum(pend[None, :] <= blk_start[:, None], axis=1), n_exp - 1).astype(jnp.int32)
    n_real = (pend[-1:] // blk).astype(jnp.int32)
    return dest, blk_e, n_real, n_blocks * blk


def _rope_tables(pos, n_heads):
    half = HEAD_DIM // 2
    inv = ROPE_THETA ** (-jnp.arange(half, dtype=F32) / half)
    ang = pos.astype(F32)[:, None] * inv[None, :]
    cos = jnp.cos(ang)
    sin = jnp.sin(ang)
    return (jnp.tile(jnp.concatenate([cos, cos], axis=-1), (1, n_heads)),
            jnp.tile(jnp.concatenate([-sin, sin], axis=-1), (1, n_heads)))


def _mix_tables(w_s_l, b_s_l, t_mix, a_width):
    reps = CHUNK // t_mix
    tri = jnp.tril(jnp.ones((t_mix, t_mix), F32))
    wt = w_s_l[:, :t_mix, :t_mix] * tri[None]
    eye = jnp.eye(reps, dtype=F32)
    wbig = jnp.einsum('ab,gts->gatbs', eye, wt).reshape(A_GROUPS, CHUNK, CHUNK)
    wmix = jnp.transpose(wbig, (1, 0, 2)).reshape(CHUNK, A_GROUPS * CHUNK)
    bs = jnp.tile(b_s_l[:, :t_mix], (1, reps))
    bs_tab = jnp.repeat(bs.T, a_width // A_GROUPS, axis=1)
    return dict(wmix=wmix.astype(BF16), bs_tab=bs_tab)


def _layer_weights(l, g_mix, w_in, g_v, g_q, g_k, w_a, w_b, w_o, g_ffn, w_rg, b_rg, w_re, b_re,
                   w1, w3, w2, g_ple, w_pg, w_ple):
    b_width = w_b.shape[1]
    n_heads = b_width // HEAD_DIM
    d_model = w_o.shape[-1]
    hid = jnp.arange(b_width) // HEAD_DIM
    hind = jnp.where(hid[:, None] == hid[None, :], 1.0 / HEAD_DIM, 0.0)
    n_exp = N_GROUPS * EXPERTS_PER_GROUP
    w_r = jnp.zeros((d_model, PLE_LANES), F32)
    w_r = w_r.at[:, :N_GROUPS].set(w_rg[l]).at[:, N_GROUPS:N_GROUPS + n_exp].set(w_re[l])
    b_r = jnp.zeros((1, PLE_LANES), F32)
    b_r = b_r.at[0, :N_GROUPS].set(b_rg[l]).at[0, N_GROUPS:N_GROUPS + n_exp].set(b_re[l])
    return dict(
        g_mix=g_mix[l][None], w_in=w_in[l].astype(BF16), g_v=g_v[l][None], hind=hind.astype(BF16),
        g_q=jnp.tile(g_q[l], n_heads)[None], g_k=jnp.tile(g_k[l], n_heads)[None],
        w_a=w_a[l].astype(BF16), w_b=w_b[l].astype(BF16), w_o=w_o[l].astype(BF16),
        g_ffn=g_ffn[l][None], w_r=w_r.astype(BF16), b_r=b_r,
        w1=w1[l], w3=w3[l], w2=w2[l],
        g_ple=g_ple[l][None], w_pg=w_pg[l].astype(BF16), w_ple=w_ple[l].astype(BF16),
    )


def _finish(x, aterm, sgb, out_b, p, wts):
    x1, hf, route = _post_attn(x, aterm, sgb, out_b, wts)
    eid = route[:, :2].astype(jnp.int32)
    dest, blk_e, n_real, n_rows = _routing_plan(eid, MOE_BLOCK)
    xd = _dispatch(hf, dest, n_rows)
    yd = _moe_ffn(xd, blk_e, n_real, wts['w1'], wts['w3'], wts['w2'])
    return _final(x1, yd, dest, route, p, wts)


def kernel(x_prompt, x_sample, cache_k, cache_v, page_table, p_prompt, p_sample, g_mix, w_in, g_v, w_s, b_s, g_q, g_k, w_a, w_b, w_o, g_ffn, w_router_group, b_router_group, w_router_expert, b_router_expert, w1, w3, w2, g_ple, w_ple_gate, w_ple):
    bsz, seq, d_model = x_prompt.shape
    dbsz, dseq, _ = x_sample.shape
    depth = g_mix.shape[0]
    b_width = w_b.shape[1]
    n_heads = b_width // HEAD_DIM
    past_len = page_table.shape[1] * PAGE_SIZE
    assert seq % MOBA_BLOCK == 0 and (bsz * seq) % ROW_TILE == 0 and (dbsz * dseq) % ROW_TILE == 0
    assert CHUNK % dseq == 0 and ROW_TILE % CHUNK == 0 and ROW_TILE % MOBA_BLOCK == 0
    assert seq // MOBA_BLOCK >= MOBA_TOPK
    params = (g_mix, w_in, g_v, g_q, g_k, w_a, w_b, w_o, g_ffn, w_router_group, b_router_group,
              w_router_expert, b_router_expert, w1, w3, w2, g_ple, w_ple_gate, w_ple)
    a_width = g_v.shape[-1]
    tab_p = _rope_tables(jnp.arange(seq, dtype=jnp.int32), n_heads)
    pos_s = past_len + (jnp.arange(ROW_TILE, dtype=jnp.int32) % dseq)
    tab_s = _rope_tables(pos_s, n_heads)
    xp = x_prompt.reshape(bsz * seq, d_model)
    xs = x_sample.reshape(dbsz * dseq, d_model)
    kp_rows, vp_rows, ks_rows, vs_rows, chunk_rows = [], [], [], [], []
    for l in range(depth):
        wts = _layer_weights(l, *params)
        wts_p = dict(wts, **_mix_tables(w_s[l], b_s[l], CHUNK, a_width))
        wts_s = dict(wts, **_mix_tables(w_s[l], b_s[l], dseq, a_width))
        aterm, sgb, qt, k, v, kb, vt, kmean = _inproj(
            xp, tab_p, wts_p, emit_kmean=True, emit_vchunk=False, pos_blocks=seq // ROW_TILE)
        nbt = bsz * seq // MOBA_BLOCK
        out_b = _moba_prompt(qt, kb.reshape(nbt, MOBA_BLOCK, b_width), vt,
                             kmean.reshape(bsz, seq // MOBA_BLOCK, b_width), bsz, seq)
        xp = _finish(xp, aterm, sgb, out_b, p_prompt[l].reshape(bsz * seq, -1), wts_p)
        kp_rows.append(jnp.transpose(k.reshape(bsz, n_heads, HEAD_DIM, seq), (0, 3, 1, 2)))
        vp_rows.append(jnp.transpose(v.reshape(bsz, n_heads, HEAD_DIM, seq), (0, 3, 1, 2)))
        aterm, sgb, q, k, v, kb, vb, vchunk = _inproj(
            xs, tab_s, wts_s, emit_kmean=False, emit_vchunk=True, pos_blocks=1)
        out_b = _moba_sample(q, kb, vb, cache_k[l], cache_v[l], page_table, dseq)
        xs = _finish(xs, aterm, sgb, out_b, p_sample[l].reshape(dbsz * dseq, -1), wts_s)
        ks_rows.append(k.reshape(dbsz, dseq, n_heads, HEAD_DIM))
        vs_rows.append(v.reshape(dbsz, dseq, n_heads, HEAD_DIM))
        chunk_rows.append(vchunk.reshape(dbsz, dseq, -1))
    return (xp.reshape(bsz, seq, d_model), xs.reshape(dbsz, dseq, d_model),
            jnp.stack(kp_rows), jnp.stack(vp_rows), jnp.stack(ks_rows), jnp.stack(vs_rows),
            jnp.stack(chunk_rows))
```

```python
import functools

import jax
import jax.numpy as jnp
from jax import lax
from jax.experimental import pallas as pl
from jax.experimental.pallas import tpu as pltpu

F32 = jnp.float32
BF16 = jnp.bfloat16

EPS = 1e-6
NEG_INF = -1e30
A_GROUPS = 8
CHUNK = 128
HEAD_DIM = 64
MOBA_BLOCK = 256
MOBA_TOPK = 3
ROPE_THETA = 10000.0
N_GROUPS = 4
EXPERTS_PER_GROUP = 8
PAGE_SIZE = 128
PLE_LANES = 128
ROW_TILE = 256
MOE_BLOCK = 256
PAGES_PER_STEP = 16
VMEM_LIMIT = 56 * 1024 * 1024
LOG2_E = 1.4426950408889634
ONES_ROWS = 16
DMA_ISSUE_UNROLL = 8

_NT = (((1,), (1,)), ((), ()))


def _rms(x, g):
    return x * lax.rsqrt(jnp.mean(x * x, axis=-1, keepdims=True) + EPS) * g


def _dot(a, b):
    return jnp.dot(a, b, preferred_element_type=F32)


def _dot_nt(a, b):
    return lax.dot_general(a, b, _NT, preferred_element_type=F32)


def _dot_tn(a, b):
    return lax.dot_general(a, b, (((0,), (0,)), ((), ())), preferred_element_type=F32)


def _const_spec(shape):
    return pl.BlockSpec(shape, lambda *_: (0,) * len(shape))


def _inproj_kernel(x_ref, gmix_ref, win_ref, gv_ref, wmix_ref, bs_ref, hind_ref, gq_ref, gk_ref,
                   cos_ref, sin_ref, wa_ref,
                   aterm_ref, sgb_ref, q_ref, k_ref, v_ref, kb_ref, vb_ref, *extra,
                   a_width, b_width, d_model, emit_kmean, emit_vchunk, transpose_qv):
    tm = x_ref.shape[0]
    h = _rms(x_ref[...], gmix_ref[...]).astype(BF16)
    offs = [0]

    def proj(width):
        o = offs[0]
        offs[0] = o + width
        return _dot(h, win_ref[:, o:o + width])

    zu = proj(a_width)
    zv = proj(a_width)
    zq = proj(b_width)
    zk = proj(b_width)
    zva = proj(b_width)

    u = jax.nn.gelu(zu)
    vn = _rms(jax.nn.gelu(zv), gv_ref[...])
    vb16 = vn.astype(BF16)
    lane_grp = lax.broadcasted_iota(jnp.int32, (CHUNK, a_width), 1) // (a_width // A_GROUPS)
    parts = []
    for c in range(tm // CHUNK):
        vc = vb16[c * CHUNK:(c + 1) * CHUNK, :]
        rhs = jnp.concatenate(
            [jnp.where(lane_grp == g, vc, jnp.zeros_like(vc)) for g in range(A_GROUPS)], axis=0)
        parts.append(_dot(wmix_ref[...], rhs) + bs_ref[...])
    s = parts[0] if len(parts) == 1 else jnp.concatenate(parts, axis=0)
    out_a = (u * s).astype(BF16)
    ga = proj(d_model)
    aterm_ref[...] = jax.nn.sigmoid(ga) * _dot(out_a, wa_ref[...])
    gb = proj(d_model)
    sgb_ref[...] = jax.nn.sigmoid(gb)

    lane = lax.broadcasted_iota(jnp.int32, (tm, b_width), 1)
    first_half = (lane % HEAD_DIM) < (HEAD_DIM // 2)
    cos = cos_ref[...]
    sin = sin_ref[...]

    def headnorm_rope(z, g):
        ms = _dot((z * z).astype(BF16), hind_ref[...])
        y = z * lax.rsqrt(ms + EPS) * g
        swapped = jnp.where(first_half,
                            pltpu.roll(y, b_width - HEAD_DIM // 2, 1),
                            pltpu.roll(y, HEAD_DIM // 2, 1))
        return y * cos + swapped * sin

    q = headnorm_rope(zq, gq_ref[...])
    k = headnorm_rope(zk, gk_ref[...])
    kb_ref[...] = k.astype(BF16)
    if transpose_qv:
        vt = zva.T
        q_ref[0] = q.T.astype(BF16)
        k_ref[0] = k.T
        v_ref[0] = vt
        vb_ref[0] = vt.astype(BF16)
    else:
        q_ref[...] = q.astype(BF16)
        k_ref[...] = k
        v_ref[...] = zva
        vb_ref[...] = zva.astype(BF16)
    idx = 0
    if emit_kmean:
        km_ref = extra[idx]
        idx += 1
        for bi in range(tm // MOBA_BLOCK):
            km_ref[bi] = jnp.mean(k[bi * MOBA_BLOCK:(bi + 1) * MOBA_BLOCK, :], axis=0, keepdims=True)
    if emit_vchunk:
        extra[idx][...] = vn


def _inproj(x, pos_tables, wts, *, emit_kmean, emit_vchunk, pos_blocks):
    n, d_model = x.shape
    a_width = wts['g_v'].shape[-1]
    b_width = wts['hind'].shape[0]
    tm = ROW_TILE
    cos_t, sin_t = pos_tables
    row = lambda i: (i, 0)
    posrow = lambda i: (i % pos_blocks, 0)
    in_specs = [
        pl.BlockSpec((tm, d_model), row),
        _const_spec((1, d_model)),
        _const_spec(wts['w_in'].shape),
        _const_spec((1, a_width)),
        _const_spec(wts['wmix'].shape),
        _const_spec(wts['bs_tab'].shape),
        _const_spec(wts['hind'].shape),
        _const_spec((1, b_width)),
        _const_spec((1, b_width)),
        pl.BlockSpec((tm, b_width), posrow),
        pl.BlockSpec((tm, b_width), posrow),
        _const_spec(wts['w_a'].shape),
    ]
    transpose_qv = emit_kmean
    if transpose_qv:
        assert tm == MOBA_BLOCK
        qv_shape = jax.ShapeDtypeStruct((n // tm, b_width, tm), BF16)
        qv_spec = pl.BlockSpec((1, b_width, tm), lambda i: (i, 0, 0))
        kv_shape = jax.ShapeDtypeStruct((n // (tm * pos_blocks), b_width, tm * pos_blocks), F32)
        kv_spec = pl.BlockSpec((1, b_width, tm), lambda i: (i // pos_blocks, 0, i % pos_blocks))
    else:
        qv_shape = jax.ShapeDtypeStruct((n, b_width), BF16)
        qv_spec = pl.BlockSpec((tm, b_width), row)
        kv_shape = jax.ShapeDtypeStruct((n, b_width), F32)
        kv_spec = pl.BlockSpec((tm, b_width), row)
    out_shape = [
        jax.ShapeDtypeStruct((n, d_model), F32),
        jax.ShapeDtypeStruct((n, d_model), F32),
        qv_shape,
        kv_shape,
        kv_shape,
        jax.ShapeDtypeStruct((n, b_width), BF16),
        qv_shape,
    ]
    out_specs = [
        pl.BlockSpec((tm, d_model), row), pl.BlockSpec((tm, d_model), row),
        qv_spec, kv_spec, kv_spec, pl.BlockSpec((tm, b_width), row),
        qv_spec,
    ]
    if emit_kmean:
        nbt = tm // MOBA_BLOCK
        out_shape.append(jax.ShapeDtypeStruct((n // MOBA_BLOCK, 1, b_width), F32))
        out_specs.append(pl.BlockSpec((nbt, 1, b_width), lambda i: (i, 0, 0)))
    if emit_vchunk:
        out_shape.append(jax.ShapeDtypeStruct((n, a_width), F32))
        out_specs.append(pl.BlockSpec((tm, a_width), row))
    kern = functools.partial(_inproj_kernel, a_width=a_width, b_width=b_width, d_model=d_model,
                             emit_kmean=emit_kmean, emit_vchunk=emit_vchunk, transpose_qv=transpose_qv)
    return pl.pallas_call(
        kern, grid=(n // tm,), in_specs=in_specs, out_specs=out_specs, out_shape=out_shape,
        name='inproj',
        compiler_params=pltpu.CompilerParams(dimension_semantics=('arbitrary',),
                                             vmem_limit_bytes=VMEM_LIMIT),
    )(x, wts['g_mix'], wts['w_in'], wts['g_v'], wts['wmix'], wts['bs_tab'], wts['hind'],
      wts['g_q'], wts['g_k'], cos_t, sin_t, wts['w_a'])


def _select_topk(scores, allowed, blk_f, nb):
    sel = jnp.zeros(scores.shape, jnp.bool_)
    for _ in range(MOBA_TOPK):
        cand = jnp.logical_and(allowed, jnp.logical_not(sel))
        scm = jnp.where(cand, scores, -jnp.inf)
        mx = jnp.max(scm, axis=-1, keepdims=True)
        is_max = jnp.logical_and(cand, scm == mx)
        first = jnp.min(jnp.where(is_max, blk_f, float(nb)), axis=-1, keepdims=True)
        sel = jnp.logical_or(sel, jnp.logical_and(is_max, blk_f == first))
    return sel


def _select_topk_rows(scores, allowed, blk_f, nb):
    sel = jnp.zeros(scores.shape, jnp.bool_)
    for _ in range(MOBA_TOPK):
        cand = jnp.logical_and(allowed, jnp.logical_not(sel))
        scm = jnp.where(cand, scores, -jnp.inf)
        mx = jnp.max(scm, axis=0, keepdims=True)
        is_max = jnp.logical_and(cand, scm == mx)
        first = jnp.min(jnp.where(is_max, blk_f, float(nb)), axis=0, keepdims=True)
        sel = jnp.logical_or(sel, jnp.logical_and(is_max, blk_f == first))
    return sel


def _moba_prompt_kernel(qt_ref, kb_ref, vt_ref, km_ref, o_ref, w_ref, bias_ref, m_ref, l_ref, acc_ref,
                        s_ref, *, n_heads, nb):
    tq = qt_ref.shape[2]
    blk = MOBA_BLOCK
    pair = 2 * HEAD_DIM
    i = pl.program_id(1)
    scale = HEAD_DIM ** -0.5 * LOG2_E
    km = km_ref[0].astype(BF16)
    blk_f = lax.broadcasted_iota(jnp.int32, (nb, tq), 0).astype(F32)
    allowed = blk_f < i.astype(F32)
    key_t = lax.broadcasted_iota(jnp.int32, (blk, tq), 0)
    qry_t = lax.broadcasted_iota(jnp.int32, (blk, tq), 1)
    causal = key_t <= qry_t
    zeros = jnp.zeros((HEAD_DIM, tq), BF16)
    ones = jnp.ones((ONES_ROWS, blk), BF16)

    for h in range(n_heads):
        qth = qt_ref[0, h * HEAD_DIM:(h + 1) * HEAD_DIM, :]
        sel = _select_topk_rows(_dot(km[:, h * HEAD_DIM:(h + 1) * HEAD_DIM], qth), allowed, blk_f, nb)
        bias_ref[h] = jnp.where(sel, 0.0, NEG_INF)
        qs = (qth.astype(F32) * scale).astype(BF16)
        col = jnp.concatenate([qs, zeros] if h % 2 == 0 else [zeros, qs], axis=0)
        w_ref[h // 2, :, (h % 2) * tq:(h % 2 + 1) * tq] = col

    def scores(j, hp):
        return _dot(kb_ref[j, :, hp * pair:(hp + 1) * pair], w_ref[hp])

    def weighted_values(j, h, p):
        vt1 = jnp.concatenate([vt_ref[j, h * HEAD_DIM:(h + 1) * HEAD_DIM, :], ones], axis=0)
        pv = _dot(vt1, p.astype(BF16))
        return pv[:HEAD_DIM, :], pv[HEAD_DIM:HEAD_DIM + 1, :]

    for hp in range(n_heads // 2):
        s2 = scores(i, hp)
        for h in (2 * hp, 2 * hp + 1):
            s = jnp.where(causal, s2[:, (h % 2) * tq:(h % 2 + 1) * tq], NEG_INF)
            m = jnp.max(s, axis=0, keepdims=True)
            pv, psum = weighted_values(i, h, jnp.exp2(s - m))
            m_ref[h:h + 1, :] = m
            l_ref[h:h + 1, :] = psum
            acc_ref[h * HEAD_DIM:(h + 1) * HEAD_DIM, :] = pv

    def stage_scores(j, slot):
        jc = jnp.minimum(j, nb - 1)
        for hp in range(n_heads // 2):
            s_ref[slot, hp] = scores(jc, hp)

    def consume(j, slot):
        jc = jnp.minimum(j, nb - 1)
        for h in range(n_heads):
            s = s_ref[slot, h // 2, :, (h % 2) * tq:(h % 2 + 1) * tq]
            bias = bias_ref[h, pl.ds(jc, 1), :]
            m = m_ref[h:h + 1, :]
            m_new = jnp.maximum(m, jnp.max(s, axis=0, keepdims=True) + bias)
            alpha = jnp.exp2(m - m_new)
            pv, psum = weighted_values(jc, h, jnp.exp2(s + (bias - m_new)))
            m_ref[h:h + 1, :] = m_new
            l_ref[h:h + 1, :] = alpha * l_ref[h:h + 1, :] + psum
            rows = slice(h * HEAD_DIM, (h + 1) * HEAD_DIM)
            acc_ref[rows, :] = alpha * acc_ref[rows, :] + pv

    @pl.when(i > 0)
    def _():
        stage_scores(0, 0)

        def body(t, carry):
            j = 2 * t
            stage_scores(j + 1, 1)
            consume(j, 0)
            stage_scores(j + 2, 0)
            consume(j + 1, 1)
            return carry

        lax.fori_loop(0, (i + 1) // 2, body, 0)
    outs = [acc_ref[h * HEAD_DIM:(h + 1) * HEAD_DIM, :] / l_ref[h:h + 1, :] for h in range(n_heads)]
    o_ref[...] = jnp.concatenate(outs, axis=0).T.astype(o_ref.dtype)


def _moba_prompt(qt, kb, vt, kmean, bsz, seq):
    _, width, tq = qt.shape
    n_heads = width // HEAD_DIM
    nb = seq // MOBA_BLOCK
    assert tq == MOBA_BLOCK and n_heads % 2 == 0
    kern = functools.partial(_moba_prompt_kernel, n_heads=n_heads, nb=nb)
    return pl.pallas_call(
        kern, grid=(bsz, nb),
        in_specs=[
            pl.BlockSpec((1, width, tq), lambda b, i: (b * nb + i, 0, 0)),
            pl.BlockSpec((nb, MOBA_BLOCK, width), lambda b, i: (b, 0, 0)),
            pl.BlockSpec((nb, width, MOBA_BLOCK), lambda b, i: (b, 0, 0)),
            pl.BlockSpec((1, nb, width), lambda b, i: (b, 0, 0)),
        ],
        out_specs=pl.BlockSpec((tq, width), lambda b, i: (b * nb + i, 0)),
        out_shape=jax.ShapeDtypeStruct((bsz * seq, width), BF16),
        scratch_shapes=[pltpu.VMEM((n_heads // 2, 2 * HEAD_DIM, 2 * tq), BF16),
                        pltpu.VMEM((n_heads, nb, tq), F32),
                        pltpu.VMEM((n_heads, tq), F32), pltpu.VMEM((n_heads, tq), F32),
                        pltpu.VMEM((width, tq), F32),
                        pltpu.VMEM((2, n_heads // 2, MOBA_BLOCK, 2 * tq), F32)],
        name='moba_prompt',
        compiler_params=pltpu.CompilerParams(dimension_semantics=('arbitrary', 'arbitrary'),
                                             vmem_limit_bytes=VMEM_LIMIT),
    )(qt, kb, vt, kmean)


def _moba_sample_kernel(pt_ref, q_ref, kn_ref, vn_ref, *rest, pg, n_heads, nb):
    ck_ref, cv_ref, o_ref, bs_ref, m_ref, l_ref, acc_ref, kbuf, vbuf, sems = rest
    ds, width = q_ref.shape
    step = pl.program_id(1)
    n_steps = pl.num_programs(1)
    lin = pl.program_id(0) * n_steps + step
    slot = lin % 2

    def fetch(bb, ss, sl):
        for t in range(pg):
            page = pt_ref[bb, ss * pg + t]
            pltpu.make_async_copy(ck_ref.at[page], kbuf.at[sl, t], sems.at[sl, 0]).start()
            pltpu.make_async_copy(cv_ref.at[page], vbuf.at[sl, t], sems.at[sl, 1]).start()

    @pl.when(lin == 0)
    def _():
        fetch(0, 0, 0)

    @pl.when(lin + 1 < pl.num_programs(0) * n_steps)
    def _():
        wrap = step + 1 == n_steps
        fetch(jnp.where(wrap, pl.program_id(0) + 1, pl.program_id(0)), jnp.where(wrap, 0, step + 1), 1 - slot)

    pltpu.make_async_copy(ck_ref.at[pl.ds(0, pg)], kbuf.at[slot], sems.at[slot, 0]).wait()
    pltpu.make_async_copy(cv_ref.at[pl.ds(0, pg)], vbuf.at[slot], sems.at[slot, 1]).wait()
    k_refs = [kbuf.at[slot, t] for t in range(pg)]
    v_refs = [vbuf.at[slot, t] for t in range(pg)]
    r = n_heads * ds
    c2 = HEAD_DIM ** -0.5 * LOG2_E
    ppb = MOBA_BLOCK // PAGE_SIZE
    row_h = lax.broadcasted_iota(jnp.int32, (r, width), 0) // ds
    lane_h = lax.broadcasted_iota(jnp.int32, (r, width), 1) // HEAD_DIM
    hmask = row_h == lane_h
    qt = jnp.concatenate([q_ref[...]] * n_heads, axis=0)
    qbd = jnp.where(hmask, qt, jnp.zeros_like(qt))
    lane_b = lax.broadcasted_iota(jnp.int32, (r, nb), 1)

    @pl.when(step == 0)
    def _():
        bs_ref[...] = jnp.zeros_like(bs_ref)
        m_ref[...] = jnp.zeros_like(m_ref)
        l_ref[...] = jnp.zeros_like(l_ref)

    bs_new, m_new, l_new = bs_ref[...], m_ref[...], l_ref[...]
    kall = jnp.concatenate([k_refs[t][...] for t in range(pg)], axis=1).astype(BF16)
    st_all = _dot(qbd, kall)
    for c in range(pg // ppb):
        st = st_all[:, c * MOBA_BLOCK:(c + 1) * MOBA_BLOCK]
        vblk = jnp.concatenate([v_refs[ppb * c + t][...] for t in range(ppb)], axis=1).astype(BF16)
        s = st * c2
        m = jnp.max(s, axis=-1, keepdims=True)
        p = jnp.exp2(s - m)
        jb = step * (pg // ppb) + c
        col = lane_b == jb
        bs_new = jnp.where(col, jnp.mean(st, axis=-1, keepdims=True), bs_new)
        m_new = jnp.where(col, m, m_new)
        l_new = jnp.where(col, jnp.sum(p, axis=-1, keepdims=True), l_new)
        acc_ref[jb] = _dot_nt(p.astype(BF16), vblk)
    bs_ref[...] = bs_new
    m_ref[...] = m_new
    l_ref[...] = l_new

    @pl.when(step == pl.num_programs(1) - 1)
    def _():
        sel = _select_topk(bs_ref[...], jnp.ones((r, nb), jnp.bool_), lane_b.astype(F32), nb)
        key_t = lax.broadcasted_iota(jnp.int32, (r, ds), 1)
        qry_t = lax.broadcasted_iota(jnp.int32, (r, ds), 0) % ds
        s_own = jnp.where(key_t <= qry_t, _dot_nt(qbd, kn_ref[...]) * c2, NEG_INF)
        m_all = m_ref[...]
        m_tot = jnp.maximum(jnp.max(jnp.where(sel, m_all, NEG_INF), axis=-1, keepdims=True),
                            jnp.max(s_own, axis=-1, keepdims=True))
        w = jnp.where(sel, jnp.exp2(m_all - m_tot), 0.0)
        p_own = jnp.exp2(s_own - m_tot)
        den = jnp.sum(w * l_ref[...], axis=-1, keepdims=True) + jnp.sum(p_own, axis=-1, keepdims=True)
        num = _dot(p_own.astype(BF16), vn_ref[...])

        def body(j, num):
            wj = jnp.sum(jnp.where(lane_b == j, w, 0.0), axis=-1, keepdims=True)
            return num + wj * acc_ref[j]

        out = jnp.where(hmask, lax.fori_loop(0, nb, body, num) / den, 0.0)
        o = out[0:ds, :]
        for h in range(1, n_heads):
            o = o + out[h * ds:(h + 1) * ds, :]
        o_ref[...] = o.astype(o_ref.dtype)


def _moba_sample(q, kb, vb, cache_k, cache_v, page_table, ds):
    n, width = q.shape
    dbsz, n_pages = page_table.shape
    n_heads = width // HEAD_DIM
    assert (n_pages * PAGE_SIZE) % MOBA_BLOCK == 0, "cached length must fill whole MoBA blocks"
    nb = n_pages * PAGE_SIZE // MOBA_BLOCK
    assert nb >= MOBA_TOPK
    pg = PAGES_PER_STEP
    ppb = MOBA_BLOCK // PAGE_SIZE
    assert n_pages % pg == 0 and pg % ppb == 0 and ds % 8 == 0
    assert cache_k.shape[1:] == (PAGE_SIZE, n_heads, HEAD_DIM)
    r = n_heads * ds
    ck = jnp.transpose(cache_k, (0, 2, 3, 1)).reshape(cache_k.shape[0], width, PAGE_SIZE)
    cv = jnp.transpose(cache_v, (0, 2, 3, 1)).reshape(cache_v.shape[0], width, PAGE_SIZE)

    tok = pl.BlockSpec((ds, width), lambda b, s, pt: (b, 0))
    hbm = pl.BlockSpec(memory_space=pl.ANY)
    grid_spec = pltpu.PrefetchScalarGridSpec(
        num_scalar_prefetch=1, grid=(dbsz, n_pages // pg),
        in_specs=[tok, tok, tok, hbm, hbm],
        out_specs=tok,
        scratch_shapes=[pltpu.VMEM((r, nb), F32), pltpu.VMEM((r, nb), F32), pltpu.VMEM((r, nb), F32),
                        pltpu.VMEM((nb, r, width), F32),
                        pltpu.VMEM((2, pg, width, PAGE_SIZE), F32), pltpu.VMEM((2, pg, width, PAGE_SIZE), F32),
                        pltpu.SemaphoreType.DMA((2, 2))],
    )
    kern = functools.partial(_moba_sample_kernel, pg=pg, n_heads=n_heads, nb=nb)
    return pl.pallas_call(
        kern, grid_spec=grid_spec, out_shape=jax.ShapeDtypeStruct((n, width), BF16),
        name='moba_sample',
        compiler_params=pltpu.CompilerParams(dimension_semantics=('arbitrary', 'arbitrary'),
                                             vmem_limit_bytes=VMEM_LIMIT),
    )(page_table, q, kb, vb, ck, cv)


def _pack_bf16_pairs(x):
    w = x.shape[1] // 2
    xb = x.astype(BF16).astype(F32)
    hi = lax.bitcast_convert_type(xb[:, :w], jnp.uint32) & jnp.uint32(0xFFFF0000)
    lo = lax.bitcast_convert_type(xb[:, w:], jnp.uint32) >> 16
    return hi | lo


def _unpack_bf16_pairs(u):
    hi = lax.bitcast_convert_type(u & jnp.uint32(0xFFFF0000), F32)
    lo = lax.bitcast_convert_type(u << 16, F32)
    return jnp.concatenate([hi, lo], axis=1).astype(BF16)


def _post_attn_kernel(x_ref, aterm_ref, sgb_ref, ob_ref, wb_ref, wo_ref, gffn_ref, wr_ref, br_ref,
                      x1_ref, hf_ref, route_ref):
    merged = aterm_ref[...] + sgb_ref[...] * _dot(ob_ref[...], wb_ref[...])
    x1 = x_ref[...] + _dot(merged.astype(BF16), wo_ref[...])
    x1_ref[...] = x1
    hf = _rms(x1, gffn_ref[...])
    hf_ref[...] = _pack_bf16_pairs(hf)
    logits = _dot(hf.astype(BF16), wr_ref[...]) + br_ref[...]
    tm, lanes = logits.shape
    lane = lax.broadcasted_iota(jnp.int32, (tm, lanes), 1).astype(F32)
    n_exp = N_GROUPS * EXPERTS_PER_GROUP
    gmask = lane < N_GROUPS
    gl = jnp.where(gmask, logits, -jnp.inf)
    gmax = jnp.max(gl, axis=-1, keepdims=True)
    grp = jnp.min(jnp.where(gl == gmax, lane, float(lanes)), axis=-1, keepdims=True)
    p_grp = 1.0 / jnp.sum(jnp.where(gmask, jnp.exp(gl - gmax), 0.0), axis=-1, keepdims=True)
    lo = N_GROUPS + grp * EXPERTS_PER_GROUP
    emask = jnp.logical_and(jnp.logical_and(lane >= lo, lane < lo + EXPERTS_PER_GROUP),
                            lane < N_GROUPS + n_exp)
    e1 = jnp.where(emask, logits, -jnp.inf)
    v1 = jnp.max(e1, axis=-1, keepdims=True)
    j1 = jnp.min(jnp.where(e1 == v1, lane, float(lanes)), axis=-1, keepdims=True)
    e2 = jnp.where(lane == j1, -jnp.inf, e1)
    v2 = jnp.max(e2, axis=-1, keepdims=True)
    j2 = jnp.min(jnp.where(e2 == v2, lane, float(lanes)), axis=-1, keepdims=True)
    t = jnp.exp(v2 - v1)
    p1 = 1.0 / (1.0 + t)
    p2 = t / (1.0 + t)
    rec = jnp.where(lane == 0, j1 - N_GROUPS, 0.0)
    rec = jnp.where(lane == 1, j2 - N_GROUPS, rec)
    rec = jnp.where(lane == 2, p_grp * p1, rec)
    rec = jnp.where(lane == 3, p_grp * p2, rec)
    route_ref[...] = rec


def _post_attn(x, aterm, sgb, out_b, wts):
    n, d_model = x.shape
    tm = ROW_TILE
    row = lambda i: (i, 0)
    return pl.pallas_call(
        _post_attn_kernel, grid=(n // tm,),
        in_specs=[pl.BlockSpec((tm, d_model), row), pl.BlockSpec((tm, d_model), row),
                  pl.BlockSpec((tm, d_model), row), pl.BlockSpec((tm, out_b.shape[1]), row),
                  _const_spec(wts['w_b'].shape), _const_spec(wts['w_o'].shape),
                  _const_spec((1, d_model)), _const_spec(wts['w_r'].shape), _const_spec((1, PLE_LANES))],
        out_specs=[pl.BlockSpec((tm, d_model), row), pl.BlockSpec((tm, d_model // 2), row),
                   pl.BlockSpec((tm, PLE_LANES), row)],
        out_shape=[jax.ShapeDtypeStruct((n, d_model), F32),
                   jax.ShapeDtypeStruct((n, d_model // 2), jnp.uint32),
                   jax.ShapeDtypeStruct((n, PLE_LANES), F32)],
        name='post_attn',
        compiler_params=pltpu.CompilerParams(dimension_semantics=('arbitrary',),
                                             vmem_limit_bytes=VMEM_LIMIT),
    )(x, aterm, sgb, out_b, wts['w_b'], wts['w_o'], wts['g_ffn'], wts['w_r'], wts['b_r'])


def _dispatch_kernel(dest_ref, hf_ref, xd_in_ref, xd_ref, sem, *, fanout):
    del xd_in_ref
    tm = hf_ref.shape[0]

    for rr in range(tm):
        for kk in range(fanout):
            pltpu.make_async_copy(hf_ref.at[pl.ds(rr, 1)],
                                  xd_ref.at[pl.ds(dest_ref[0, 0, rr * fanout + kk], 1)], sem).start()
    for kk in range(fanout):
        pltpu.make_async_copy(hf_ref, xd_ref.at[pl.ds(0, tm)], sem).wait()


def _dispatch(hf, dest, n_rows):
    n, w = hf.shape
    fanout = dest.shape[0] // n
    tm = min(n, 512)
    steps = n // tm
    dest3 = dest.reshape(steps, 1, tm * fanout)
    xd0 = jnp.zeros((n_rows, w), hf.dtype)
    kern = functools.partial(_dispatch_kernel, fanout=fanout)
    return pl.pallas_call(
        kern, grid=(steps,),
        in_specs=[pl.BlockSpec((1, 1, tm * fanout), lambda i: (i, 0, 0), memory_space=pltpu.SMEM),
                  pl.BlockSpec((tm, w), lambda i: (i, 0)), pl.BlockSpec(memory_space=pl.ANY)],
        out_specs=pl.BlockSpec(memory_space=pl.ANY),
        out_shape=jax.ShapeDtypeStruct((n_rows, w), hf.dtype),
        scratch_shapes=[pltpu.SemaphoreType.DMA(())],
        input_output_aliases={2: 0},
        name='dispatch',
        compiler_params=pltpu.CompilerParams(dimension_semantics=('arbitrary',)),
    )(dest3, hf, xd0)


def _moe_ffn_kernel(be_ref, nreal_ref, xd_ref, w1_ref, w3_ref, w2_ref, yd_ref, w1b_ref, w3b_ref, w2b_ref):
    i = pl.program_id(0)
    live = i < nreal_ref[0]
    new_expert = jnp.logical_or(i == 0, be_ref[i] != be_ref[jnp.maximum(i - 1, 0)])

    @pl.when(jnp.logical_and(live, new_expert))
    def _():
        w1b_ref[...] = w1_ref[...].astype(BF16)
        w3b_ref[...] = w3_ref[...].astype(BF16)
        w2b_ref[...] = w2_ref[...].astype(BF16)

    @pl.when(live)
    def _():
        xb = _unpack_bf16_pairs(xd_ref[...])
        a = _dot(xb, w1b_ref[...])
        b = _dot(xb, w3b_ref[...])
        yd_ref[...] = _dot((jax.nn.silu(a) * b).astype(BF16), w2b_ref[...])

    @pl.when(i >= nreal_ref[0])
    def _():
        yd_ref[...] = jnp.zeros_like(yd_ref)


def _moe_ffn(xd, blk_e, n_real, w1, w3, w2):
    n_rows, half = xd.shape
    d_model = 2 * half
    ff = w1.shape[-1]
    blk = MOE_BLOCK
    grid_spec = pltpu.PrefetchScalarGridSpec(
        num_scalar_prefetch=2, grid=(n_rows // blk,),
        in_specs=[pl.BlockSpec((blk, half), lambda i, be, nr: (i, 0)),
                  pl.BlockSpec((None, d_model, ff), lambda i, be, nr: (be[i], 0, 0)),
                  pl.BlockSpec((None, d_model, ff), lambda i, be, nr: (be[i], 0, 0)),
                  pl.BlockSpec((None, ff, d_model), lambda i, be, nr: (be[i], 0, 0))],
        out_specs=pl.BlockSpec((blk, d_model), lambda i, be, nr: (i, 0)),
        scratch_shapes=[pltpu.VMEM((d_model, ff), BF16), pltpu.VMEM((d_model, ff), BF16),
                        pltpu.VMEM((ff, d_model), BF16)],
    )
    return pl.pallas_call(
        _moe_ffn_kernel, grid_spec=grid_spec,
        out_shape=jax.ShapeDtypeStruct((n_rows, d_model), F32),
        name='moe_ffn',
        compiler_params=pltpu.CompilerParams(dimension_semantics=('arbitrary',),
                                             vmem_limit_bytes=VMEM_LIMIT),
    )(blk_e, n_real, xd, w1, w3, w2)


def _final_kernel(d0_ref, dn_ref, yd_ref, x1_ref, route_ref, p_ref, gple_ref, wpg_ref, wple_ref,
                  y_ref, ybuf, sems, *, fanout):
    tm = x1_ref.shape[0]
    step = pl.program_id(0)
    slot = step % 2

    def start_rows(idx_ref, s, static_rows):
        def issue(rr, carry):
            for kk in range(fanout):
                pltpu.make_async_copy(yd_ref.at[pl.ds(idx_ref[0, 0, rr * fanout + kk], 1)],
                                      ybuf.at[s, kk, pl.ds(rr, 1)], sems.at[s]).start()
            return carry

        if static_rows:
            for rr in range(tm):
                issue(rr, 0)
        else:
            lax.fori_loop(0, tm, issue, 0, unroll=DMA_ISSUE_UNROLL)

    def wait_rows(s):
        for kk in range(fanout):
            pltpu.make_async_copy(yd_ref.at[pl.ds(0, tm)], ybuf.at[s, kk], sems.at[s]).wait()

    @pl.when(step == 0)
    def _():
        start_rows(d0_ref, 0, False)

    start_rows(dn_ref, 1 - slot, True)
    wait_rows(slot)

    route = route_ref[...]
    moe = route[:, 2:3] * ybuf[slot, 0]
    for kk in range(1, fanout):
        moe = moe + route[:, 2 + kk:3 + kk] * ybuf[slot, kk]
    x2 = x1_ref[...] + moe
    gate = jax.nn.sigmoid(_dot(_rms(x2, gple_ref[...]).astype(BF16), wpg_ref[...]))
    y_ref[...] = x2 + gate * _dot(p_ref[...].astype(BF16), wple_ref[...])

    @pl.when(step == pl.num_programs(0) - 1)
    def _():
        wait_rows(1 - slot)


def _final(x1, yd, dest, route, p, wts):
    n, d_model = x1.shape
    fanout = dest.shape[0] // n
    tm = ROW_TILE
    steps = n // tm
    dest3 = dest.reshape(steps, 1, tm * fanout)
    row = lambda i: (i, 0)
    kern = functools.partial(_final_kernel, fanout=fanout)
    return pl.pallas_call(
        kern, grid=(steps,),
        in_specs=[pl.BlockSpec((1, 1, tm * fanout), lambda i: (0, 0, 0), memory_space=pltpu.SMEM),
                  pl.BlockSpec((1, 1, tm * fanout), lambda i: (jnp.minimum(i + 1, steps - 1), 0, 0),
                               memory_space=pltpu.SMEM),
                  pl.BlockSpec(memory_space=pl.ANY),
                  pl.BlockSpec((tm, d_model), row), pl.BlockSpec((tm, PLE_LANES), row),
                  pl.BlockSpec((tm, p.shape[1]), row),
                  _const_spec((1, d_model)), _const_spec(wts['w_pg'].shape), _const_spec(wts['w_ple'].shape)],
        out_specs=pl.BlockSpec((tm, d_model), row),
        out_shape=jax.ShapeDtypeStruct((n, d_model), F32),
        scratch_shapes=[pltpu.VMEM((2, fanout, tm, d_model), F32), pltpu.SemaphoreType.DMA((2,))],
        name='final',
        compiler_params=pltpu.CompilerParams(dimension_semantics=('arbitrary',),
                                             vmem_limit_bytes=VMEM_LIMIT),
    )(dest3, dest3, yd, x1, route, p, wts['g_ple'], wts['w_pg'], wts['w_ple'])


def _routing_plan(eid, blk):
    n_exp = N_GROUPS * EXPERTS_PER_GROUP
    e = eid.reshape(-1)
    n_assign = e.shape[0]
    onehot = (e[:, None] == jnp.arange(n_exp, dtype=jnp.int32)[None, :]).astype(jnp.int32)
    counts = jnp.sum(onehot, axis=0)
    rank = jnp.take_along_axis(jnp.cumsum(onehot, axis=0), e[:, None], axis=1)[:, 0] - 1
    pcounts = ((counts + blk - 1) // blk) * blk
    pend = jnp.cumsum(pcounts)
    pstart = pend - pcounts
    dest = (pstart[e] + rank).astype(jnp.int32)
    n_blocks = -(-(n_assign + n_exp * (blk - 1)) // blk)
    blk_start = jnp.arange(n_blocks, dtype=jnp.int32) * blk
    blk_e = jnp.minimum(jnp.sum(pend[None, :] <= blk_start[:, None], axis=1), n_exp - 1).astype(jnp.int32)
    n_real = (pend[-1:] // blk).astype(jnp.int32)
    return dest, blk_e, n_real, n_blocks * blk


def _rope_tables(pos, n_heads):
    half = HEAD_DIM // 2
    inv = ROPE_THETA ** (-jnp.arange(half, dtype=F32) / half)
    ang = pos.astype(F32)[:, None] * inv[None, :]
    cos = jnp.cos(ang)
    sin = jnp.sin(ang)
    return (jnp.tile(jnp.concatenate([cos, cos], axis=-1), (1, n_heads)),
            jnp.tile(jnp.concatenate([-sin, sin], axis=-1), (1, n_heads)))


def _mix_tables(w_s_l, b_s_l, t_mix, a_width):
    reps = CHUNK // t_mix
    tri = jnp.tril(jnp.ones((t_mix, t_mix), F32))
    wt = w_s_l[:, :t_mix, :t_mix] * tri[None]
    eye = jnp.eye(reps, dtype=F32)
    wbig = jnp.einsum('ab,gts->gatbs', eye, wt).reshape(A_GROUPS, CHUNK, CHUNK)
    wmix = jnp.transpose(wbig, (1, 0, 2)).reshape(CHUNK, A_GROUPS * CHUNK)
    bs = jnp.tile(b_s_l[:, :t_mix], (1, reps))
    bs_tab = jnp.repeat(bs.T, a_width // A_GROUPS, axis=1)
    return dict(wmix=wmix.astype(BF16), bs_tab=bs_tab)


def _layer_weights(l, g_mix, w_in, g_v, g_q, g_k, w_a, w_b, w_o, g_ffn, w_rg, b_rg, w_re, b_re,
                   w1, w3, w2, g_ple, w_pg, w_ple):
    b_width = w_b.shape[1]
    n_heads = b_width // HEAD_DIM
    d_model = w_o.shape[-1]
    hid = jnp.arange(b_width) // HEAD_DIM
    hind = jnp.where(hid[:, None] == hid[None, :], 1.0 / HEAD_DIM, 0.0)
    n_exp = N_GROUPS * EXPERTS_PER_GROUP
    w_r = jnp.zeros((d_model, PLE_LANES), F32)
    w_r = w_r.at[:, :N_GROUPS].set(w_rg[l]).at[:, N_GROUPS:N_GROUPS + n_exp].set(w_re[l])
    b_r = jnp.zeros((1, PLE_LANES), F32)
    b_r = b_r.at[0, :N_GROUPS].set(b_rg[l]).at[0, N_GROUPS:N_GROUPS + n_exp].set(b_re[l])
    return dict(
        g_mix=g_mix[l][None], w_in=w_in[l].astype(BF16), g_v=g_v[l][None], hind=hind.astype(BF16),
        g_q=jnp.tile(g_q[l], n_heads)[None], g_k=jnp.tile(g_k[l], n_heads)[None],
        w_a=w_a[l].astype(BF16), w_b=w_b[l].astype(BF16), w_o=w_o[l].astype(BF16),
        g_ffn=g_ffn[l][None], w_r=w_r.astype(BF16), b_r=b_r,
        w1=w1[l], w3=w3[l], w2=w2[l],
        g_ple=g_ple[l][None], w_pg=w_pg[l].astype(BF16), w_ple=w_ple[l].astype(BF16),
    )


def _finish(x, aterm, sgb, out_b, p, wts):
    x1, hf, route = _post_attn(x, aterm, sgb, out_b, wts)
    eid = route[:, :2].astype(jnp.int32)
    dest, blk_e, n_real, n_rows = _routing_plan(eid, MOE_BLOCK)
    xd = _dispatch(hf, dest, n_rows)
    yd = _moe_ffn(xd, blk_e, n_real, wts['w1'], wts['w3'], wts['w2'])
    return _final(x1, yd, dest, route, p, wts)


def kernel(x_prompt, x_sample, cache_k, cache_v, page_table, p_prompt, p_sample, g_mix, w_in, g_v, w_s, b_s, g_q, g_k, w_a, w_b, w_o, g_ffn, w_router_group, b_router_group, w_router_expert, b_router_expert, w1, w3, w2, g_ple, w_ple_gate, w_ple):
    bsz, seq, d_model = x_prompt.shape
    dbsz, dseq, _ = x_sample.shape
    depth = g_mix.shape[0]
    b_width = w_b.shape[1]
    n_heads = b_width // HEAD_DIM
    past_len = page_table.shape[1] * PAGE_SIZE
    assert seq % MOBA_BLOCK == 0 and (bsz * seq) % ROW_TILE == 0 and (dbsz * dseq) % ROW_TILE == 0
    assert CHUNK % dseq == 0 and ROW_TILE % CHUNK == 0 and ROW_TILE % MOBA_BLOCK == 0
    assert seq // MOBA_BLOCK >= MOBA_TOPK
    params = (g_mix, w_in, g_v, g_q, g_k, w_a, w_b, w_o, g_ffn, w_router_group, b_router_group,
              w_router_expert, b_router_expert, w1, w3, w2, g_ple, w_ple_gate, w_ple)
    a_width = g_v.shape[-1]
    tab_p = _rope_tables(jnp.arange(seq, dtype=jnp.int32), n_heads)
    pos_s = past_len + (jnp.arange(ROW_TILE, dtype=jnp.int32) % dseq)
    tab_s = _rope_tables(pos_s, n_heads)
    xp = x_prompt.reshape(bsz * seq, d_model)
    xs = x_sample.reshape(dbsz * dseq, d_model)
    kp_rows, vp_rows, ks_rows, vs_rows, chunk_rows = [], [], [], [], []
    for l in range(depth):
        wts = _layer_weights(l, *params)
        wts_p = dict(wts, **_mix_tables(w_s[l], b_s[l], CHUNK, a_width))
        wts_s = dict(wts, **_mix_tables(w_s[l], b_s[l], dseq, a_width))
        aterm, sgb, qt, k, v, kb, vt, kmean = _inproj(
            xp, tab_p, wts_p, emit_kmean=True, emit_vchunk=False, pos_blocks=seq // ROW_TILE)
        nbt = bsz * seq // MOBA_BLOCK
        out_b = _moba_prompt(qt, kb.reshape(nbt, MOBA_BLOCK, b_width), vt,
                             kmean.reshape(bsz, seq // MOBA_BLOCK, b_width), bsz, seq)
        xp = _finish(xp, aterm, sgb, out_b, p_prompt[l].reshape(bsz * seq, -1), wts_p)
        kp_rows.append(jnp.transpose(k.reshape(bsz, n_heads, HEAD_DIM, seq), (0, 3, 1, 2)))
        vp_rows.append(jnp.transpose(v.reshape(bsz, n_heads, HEAD_DIM, seq), (0, 3, 1, 2)))
        aterm, sgb, q, k, v, kb, vb, vchunk = _inproj(
            xs, tab_s, wts_s, emit_kmean=False, emit_vchunk=True, pos_blocks=1)
        out_b = _moba_sample(q, kb, vb, cache_k[l], cache_v[l], page_table, dseq)
        xs = _finish(xs, aterm, sgb, out_b, p_sample[l].reshape(dbsz * dseq, -1), wts_s)
        ks_rows.append(k.reshape(dbsz, dseq, n_heads, HEAD_DIM))
        vs_rows.append(v.reshape(dbsz, dseq, n_heads, HEAD_DIM))
        chunk_rows.append(vchunk.reshape(dbsz, dseq, -1))
    return (xp.reshape(bsz, seq, d_model), xs.reshape(dbsz, dseq, d_model),
            jnp.stack(kp_rows), jnp.stack(vp_rows), jnp.stack(ks_rows), jnp.stack(vs_rows),
            jnp.stack(chunk_rows))
```

```python
import functools

import jax
import jax.numpy as jnp
from jax import lax
from jax.experimental import pallas as pl
from jax.experimental.pallas import tpu as pltpu

F32 = jnp.float32
BF16 = jnp.bfloat16

EPS = 1e-6
NEG_INF = -1e30
A_GROUPS = 8
CHUNK = 128
HEAD_DIM = 64
MOBA_BLOCK = 256
MOBA_TOPK = 3
ROPE_THETA = 10000.0
N_GROUPS = 4
EXPERTS_PER_GROUP = 8
PAGE_SIZE = 128
PLE_LANES = 128
ROW_TILE = 256
MOE_BLOCK = 256
PAGES_PER_STEP = 16
PAGE_SLOTS = 3
VMEM_LIMIT = 56 * 1024 * 1024
LOG2_E = 1.4426950408889634
ONES_ROWS = 16
DMA_ISSUE_UNROLL = 8

_NT = (((1,), (1,)), ((), ()))


def _rms(x, g):
    return x * lax.rsqrt(jnp.mean(x * x, axis=-1, keepdims=True) + EPS) * g


def _dot(a, b):
    return jnp.dot(a, b, preferred_element_type=F32)


def _dot_nt(a, b):
    return lax.dot_general(a, b, _NT, preferred_element_type=F32)


def _dot_tn(a, b):
    return lax.dot_general(a, b, (((0,), (0,)), ((), ())), preferred_element_type=F32)


def _const_spec(shape):
    return pl.BlockSpec(shape, lambda *_: (0,) * len(shape))


def _inproj_kernel(x_ref, gmix_ref, win_ref, gv_ref, wmix_ref, bs_ref, hind_ref, gq_ref, gk_ref,
                   cos_ref, sin_ref, wa_ref,
                   aterm_ref, sgb_ref, q_ref, k_ref, v_ref, kb_ref, vb_ref, *extra,
                   a_width, b_width, d_model, emit_kmean, emit_vchunk, transpose_qv):
    tm = x_ref.shape[0]
    h = _rms(x_ref[...], gmix_ref[...]).astype(BF16)
    offs = [0]

    def proj(width):
        o = offs[0]
        offs[0] = o + width
        return _dot(h, win_ref[:, o:o + width])

    zu = proj(a_width)
    zv = proj(a_width)
    zq = proj(b_width)
    zk = proj(b_width)
    zva = proj(b_width)

    u = jax.nn.gelu(zu)
    vn = _rms(jax.nn.gelu(zv), gv_ref[...])
    vb16 = vn.astype(BF16)
    lane_grp = lax.broadcasted_iota(jnp.int32, (CHUNK, a_width), 1) // (a_width // A_GROUPS)
    parts = []
    for c in range(tm // CHUNK):
        vc = vb16[c * CHUNK:(c + 1) * CHUNK, :]
        rhs = jnp.concatenate(
            [jnp.where(lane_grp == g, vc, jnp.zeros_like(vc)) for g in range(A_GROUPS)], axis=0)
        parts.append(_dot(wmix_ref[...], rhs) + bs_ref[...])
    s = parts[0] if len(parts) == 1 else jnp.concatenate(parts, axis=0)
    out_a = (u * s).astype(BF16)
    ga = proj(d_model)
    aterm_ref[...] = jax.nn.sigmoid(ga) * _dot(out_a, wa_ref[...])
    gb = proj(d_model)
    sgb_ref[...] = jax.nn.sigmoid(gb)

    lane = lax.broadcasted_iota(jnp.int32, (tm, b_width), 1)
    first_half = (lane % HEAD_DIM) < (HEAD_DIM // 2)
    cos = cos_ref[...]
    sin = sin_ref[...]

    def headnorm_rope(z, g):
        ms = _dot((z * z).astype(BF16), hind_ref[...])
        y = z * lax.rsqrt(ms + EPS) * g
        swapped = jnp.where(first_half,
                            pltpu.roll(y, b_width - HEAD_DIM // 2, 1),
                            pltpu.roll(y, HEAD_DIM // 2, 1))
        return y * cos + swapped * sin

    q = headnorm_rope(zq, gq_ref[...])
    k = headnorm_rope(zk, gk_ref[...])
    kb_ref[...] = k.astype(BF16)
    if transpose_qv:
        vt = zva.T
        q_ref[0] = q.T.astype(BF16)
        k_ref[0] = k.T
        v_ref[0] = vt
        vb_ref[0] = vt.astype(BF16)
    else:
        q_ref[...] = q.astype(BF16)
        k_ref[...] = k
        v_ref[...] = zva
        vb_ref[...] = zva.astype(BF16)
    idx = 0
    if emit_kmean:
        km_ref = extra[idx]
        idx += 1
        for bi in range(tm // MOBA_BLOCK):
            km_ref[bi] = jnp.mean(k[bi * MOBA_BLOCK:(bi + 1) * MOBA_BLOCK, :], axis=0, keepdims=True)
    if emit_vchunk:
        extra[idx][...] = vn


def _inproj(x, pos_tables, wts, *, emit_kmean, emit_vchunk, pos_blocks):
    n, d_model = x.shape
    a_width = wts['g_v'].shape[-1]
    b_width = wts['hind'].shape[0]
    tm = ROW_TILE
    cos_t, sin_t = pos_tables
    row = lambda i: (i, 0)
    posrow = lambda i: (i % pos_blocks, 0)
    in_specs = [
        pl.BlockSpec((tm, d_model), row),
        _const_spec((1, d_model)),
        _const_spec(wts['w_in'].shape),
        _const_spec((1, a_width)),
        _const_spec(wts['wmix'].shape),
        _const_spec(wts['bs_tab'].shape),
        _const_spec(wts['hind'].shape),
        _const_spec((1, b_width)),
        _const_spec((1, b_width)),
        pl.BlockSpec((tm, b_width), posrow),
        pl.BlockSpec((tm, b_width), posrow),
        _const_spec(wts['w_a'].shape),
    ]
    transpose_qv = emit_kmean
    if transpose_qv:
        assert tm == MOBA_BLOCK
        qv_shape = jax.ShapeDtypeStruct((n // tm, b_width, tm), BF16)
        qv_spec = pl.BlockSpec((1, b_width, tm), lambda i: (i, 0, 0))
        kv_shape = jax.ShapeDtypeStruct((n // (tm * pos_blocks), b_width, tm * pos_blocks), F32)
        kv_spec = pl.BlockSpec((1, b_width, tm), lambda i: (i // pos_blocks, 0, i % pos_blocks))
    else:
        qv_shape = jax.ShapeDtypeStruct((n, b_width), BF16)
        qv_spec = pl.BlockSpec((tm, b_width), row)
        kv_shape = jax.ShapeDtypeStruct((n, b_width), F32)
        kv_spec = pl.BlockSpec((tm, b_width), row)
    out_shape = [
        jax.ShapeDtypeStruct((n, d_model), F32),
        jax.ShapeDtypeStruct((n, d_model), F32),
        qv_shape,
        kv_shape,
        kv_shape,
        jax.ShapeDtypeStruct((n, b_width), BF16),
        qv_shape,
    ]
    out_specs = [
        pl.BlockSpec((tm, d_model), row), pl.BlockSpec((tm, d_model), row),
        qv_spec, kv_spec, kv_spec, pl.BlockSpec((tm, b_width), row),
        qv_spec,
    ]
    if emit_kmean:
        nbt = tm // MOBA_BLOCK
        out_shape.append(jax.ShapeDtypeStruct((n // MOBA_BLOCK, 1, b_width), F32))
        out_specs.append(pl.BlockSpec((nbt, 1, b_width), lambda i: (i, 0, 0)))
    if emit_vchunk:
        out_shape.append(jax.ShapeDtypeStruct((n, a_width), F32))
        out_specs.append(pl.BlockSpec((tm, a_width), row))
    kern = functools.partial(_inproj_kernel, a_width=a_width, b_width=b_width, d_model=d_model,
                             emit_kmean=emit_kmean, emit_vchunk=emit_vchunk, transpose_qv=transpose_qv)
    return pl.pallas_call(
        kern, grid=(n // tm,), in_specs=in_specs, out_specs=out_specs, out_shape=out_shape,
        name='inproj',
        compiler_params=pltpu.CompilerParams(dimension_semantics=('arbitrary',),
                                             vmem_limit_bytes=VMEM_LIMIT),
    )(x, wts['g_mix'], wts['w_in'], wts['g_v'], wts['wmix'], wts['bs_tab'], wts['hind'],
      wts['g_q'], wts['g_k'], cos_t, sin_t, wts['w_a'])


def _select_topk(scores, allowed, blk_f, nb):
    sel = jnp.zeros(scores.shape, jnp.bool_)
    for _ in range(MOBA_TOPK):
        cand = jnp.logical_and(allowed, jnp.logical_not(sel))
        scm = jnp.where(cand, scores, -jnp.inf)
        mx = jnp.max(scm, axis=-1, keepdims=True)
        is_max = jnp.logical_and(cand, scm == mx)
        first = jnp.min(jnp.where(is_max, blk_f, float(nb)), axis=-1, keepdims=True)
        sel = jnp.logical_or(sel, jnp.logical_and(is_max, blk_f == first))
    return sel


def _select_topk_rows(scores, allowed, blk_f, nb):
    sel = jnp.zeros(scores.shape, jnp.bool_)
    for _ in range(MOBA_TOPK):
        cand = jnp.logical_and(allowed, jnp.logical_not(sel))
        scm = jnp.where(cand, scores, -jnp.inf)
        mx = jnp.max(scm, axis=0, keepdims=True)
        is_max = jnp.logical_and(cand, scm == mx)
        first = jnp.min(jnp.where(is_max, blk_f, float(nb)), axis=0, keepdims=True)
        sel = jnp.logical_or(sel, jnp.logical_and(is_max, blk_f == first))
    return sel


def _moba_prompt_kernel(qt_ref, kb_ref, vt_ref, km_ref, o_ref, w_ref, bias_ref, m_ref, l_ref, acc_ref,
                        s_ref, *, n_heads, nb):
    tq = qt_ref.shape[2]
    blk = MOBA_BLOCK
    pair = 2 * HEAD_DIM
    i = pl.program_id(1)
    scale = HEAD_DIM ** -0.5 * LOG2_E
    km = km_ref[0].astype(BF16)
    blk_f = lax.broadcasted_iota(jnp.int32, (nb, tq), 0).astype(F32)
    allowed = blk_f < i.astype(F32)
    key_t = lax.broadcasted_iota(jnp.int32, (blk, tq), 0)
    qry_t = lax.broadcasted_iota(jnp.int32, (blk, tq), 1)
    causal = key_t <= qry_t
    zeros = jnp.zeros((HEAD_DIM, tq), BF16)
    ones = jnp.ones((ONES_ROWS, blk), BF16)

    for h in range(n_heads):
        qth = qt_ref[0, h * HEAD_DIM:(h + 1) * HEAD_DIM, :]
        sel = _select_topk_rows(_dot(km[:, h * HEAD_DIM:(h + 1) * HEAD_DIM], qth), allowed, blk_f, nb)
        bias_ref[h] = jnp.where(sel, 0.0, NEG_INF)
        qs = (qth.astype(F32) * scale).astype(BF16)
        col = jnp.concatenate([qs, zeros] if h % 2 == 0 else [zeros, qs], axis=0)
        w_ref[h // 2, :, (h % 2) * tq:(h % 2 + 1) * tq] = col

    def scores(j, hp):
        return _dot(kb_ref[j, :, hp * pair:(hp + 1) * pair], w_ref[hp])

    def weighted_values(j, h, p):
        vt1 = jnp.concatenate([vt_ref[j, h * HEAD_DIM:(h + 1) * HEAD_DIM, :], ones], axis=0)
        pv = _dot(vt1, p.astype(BF16))
        return pv[:HEAD_DIM, :], pv[HEAD_DIM:HEAD_DIM + 1, :]

    for hp in range(n_heads // 2):
        s2 = scores(i, hp)
        for h in (2 * hp, 2 * hp + 1):
            s = jnp.where(causal, s2[:, (h % 2) * tq:(h % 2 + 1) * tq], NEG_INF)
            m = jnp.max(s, axis=0, keepdims=True)
            pv, psum = weighted_values(i, h, jnp.exp2(s - m))
            m_ref[h:h + 1, :] = m
            l_ref[h:h + 1, :] = psum
            acc_ref[h * HEAD_DIM:(h + 1) * HEAD_DIM, :] = pv

    def stage_scores(j, slot):
        jc = jnp.minimum(j, nb - 1)
        for hp in range(n_heads // 2):
            s_ref[slot, hp] = scores(jc, hp)

    def consume(j, slot):
        jc = jnp.minimum(j, nb - 1)
        for h in range(n_heads):
            s = s_ref[slot, h // 2, :, (h % 2) * tq:(h % 2 + 1) * tq]
            bias = bias_ref[h, pl.ds(jc, 1), :]
            m = m_ref[h:h + 1, :]
            m_new = jnp.maximum(m, jnp.max(s, axis=0, keepdims=True) + bias)
            alpha = jnp.exp2(m - m_new)
            pv, psum = weighted_values(jc, h, jnp.exp2(s + (bias - m_new)))
            m_ref[h:h + 1, :] = m_new
            l_ref[h:h + 1, :] = alpha * l_ref[h:h + 1, :] + psum
            rows = slice(h * HEAD_DIM, (h + 1) * HEAD_DIM)
            acc_ref[rows, :] = alpha * acc_ref[rows, :] + pv

    @pl.when(i > 0)
    def _():
        stage_scores(0, 0)

        def body(t, carry):
            j = 2 * t
            stage_scores(j + 1, 1)
            consume(j, 0)
            stage_scores(j + 2, 0)
            consume(j + 1, 1)
            return carry

        lax.fori_loop(0, (i + 1) // 2, body, 0)
    outs = [acc_ref[h * HEAD_DIM:(h + 1) * HEAD_DIM, :] / l_ref[h:h + 1, :] for h in range(n_heads)]
    o_ref[...] = jnp.concatenate(outs, axis=0).T.astype(o_ref.dtype)


def _moba_prompt(qt, kb, vt, kmean, bsz, seq):
    _, width, tq = qt.shape
    n_heads = width // HEAD_DIM
    nb = seq // MOBA_BLOCK
    assert tq == MOBA_BLOCK and n_heads % 2 == 0
    kern = functools.partial(_moba_prompt_kernel, n_heads=n_heads, nb=nb)
    return pl.pallas_call(
        kern, grid=(bsz, nb),
        in_specs=[
            pl.BlockSpec((1, width, tq), lambda b, i: (b * nb + i, 0, 0)),
            pl.BlockSpec((nb, MOBA_BLOCK, width), lambda b, i: (b, 0, 0)),
            pl.BlockSpec((nb, width, MOBA_BLOCK), lambda b, i: (b, 0, 0)),
            pl.BlockSpec((1, nb, width), lambda b, i: (b, 0, 0)),
        ],
        out_specs=pl.BlockSpec((tq, width), lambda b, i: (b * nb + i, 0)),
        out_shape=jax.ShapeDtypeStruct((bsz * seq, width), BF16),
        scratch_shapes=[pltpu.VMEM((n_heads // 2, 2 * HEAD_DIM, 2 * tq), BF16),
                        pltpu.VMEM((n_heads, nb, tq), F32),
                        pltpu.VMEM((n_heads, tq), F32), pltpu.VMEM((n_heads, tq), F32),
                        pltpu.VMEM((width, tq), F32),
                        pltpu.VMEM((2, n_heads // 2, MOBA_BLOCK, 2 * tq), F32)],
        name='moba_prompt',
        compiler_params=pltpu.CompilerParams(dimension_semantics=('arbitrary', 'arbitrary'),
                                             vmem_limit_bytes=VMEM_LIMIT),
    )(qt, kb, vt, kmean)


def _moba_sample_kernel(pt_ref, q_ref, kn_ref, vn_ref, *rest, pg, n_heads, nb, n_steps, n_total):
    ck_ref, cv_ref, o_ref, bs_ref, m_ref, l_ref, acc_ref, kbuf, vbuf, sems = rest
    ds, width = q_ref.shape
    step = pl.program_id(1)
    n_slots = kbuf.shape[0]
    ahead = n_slots - 1
    lin = pl.program_id(0) * n_steps + step
    slot = lin % n_slots

    def fetch(idx):
        bb = idx // n_steps
        ss = idx % n_steps
        sl = idx % n_slots
        for t in range(pg):
            page = pt_ref[bb, ss * pg + t]
            pltpu.make_async_copy(ck_ref.at[page], kbuf.at[sl, t], sems.at[sl, 0]).start()
            pltpu.make_async_copy(cv_ref.at[page], vbuf.at[sl, t], sems.at[sl, 1]).start()

    @pl.when(lin == 0)
    def _():
        for a in range(min(ahead, n_total)):
            fetch(jnp.int32(a))

    @pl.when(lin + ahead < n_total)
    def _():
        fetch(lin + ahead)

    pltpu.make_async_copy(ck_ref.at[pl.ds(0, pg)], kbuf.at[slot], sems.at[slot, 0]).wait()
    pltpu.make_async_copy(cv_ref.at[pl.ds(0, pg)], vbuf.at[slot], sems.at[slot, 1]).wait()
    k_refs = [kbuf.at[slot, t] for t in range(pg)]
    v_refs = [vbuf.at[slot, t] for t in range(pg)]
    r = n_heads * ds
    c2 = HEAD_DIM ** -0.5 * LOG2_E
    ppb = MOBA_BLOCK // PAGE_SIZE
    row_h = lax.broadcasted_iota(jnp.int32, (r, width), 0) // ds
    lane_h = lax.broadcasted_iota(jnp.int32, (r, width), 1) // HEAD_DIM
    hmask = row_h == lane_h
    qt = jnp.concatenate([q_ref[...]] * n_heads, axis=0)
    qbd = jnp.where(hmask, qt, jnp.zeros_like(qt))
    lane_b = lax.broadcasted_iota(jnp.int32, (r, nb), 1)

    @pl.when(step == 0)
    def _():
        bs_ref[...] = jnp.zeros_like(bs_ref)
        m_ref[...] = jnp.zeros_like(m_ref)
        l_ref[...] = jnp.zeros_like(l_ref)

    bs_new, m_new, l_new = bs_ref[...], m_ref[...], l_ref[...]
    kall = jnp.concatenate([k_refs[t][...] for t in range(pg)], axis=1).astype(BF16)
    st_all = _dot(qbd, kall)
    for c in range(pg // ppb):
        st = st_all[:, c * MOBA_BLOCK:(c + 1) * MOBA_BLOCK]
        vblk = jnp.concatenate([v_refs[ppb * c + t][...] for t in range(ppb)], axis=1).astype(BF16)
        s = st * c2
        m = jnp.max(s, axis=-1, keepdims=True)
        p = jnp.exp2(s - m)
        jb = step * (pg // ppb) + c
        col = lane_b == jb
        bs_new = jnp.where(col, jnp.mean(st, axis=-1, keepdims=True), bs_new)
        m_new = jnp.where(col, m, m_new)
        l_new = jnp.where(col, jnp.sum(p, axis=-1, keepdims=True), l_new)
        acc_ref[jb] = _dot_nt(p.astype(BF16), vblk)
    bs_ref[...] = bs_new
    m_ref[...] = m_new
    l_ref[...] = l_new

    @pl.when(step == n_steps - 1)
    def _():
        sel = _select_topk(bs_ref[...], jnp.ones((r, nb), jnp.bool_), lane_b.astype(F32), nb)
        key_t = lax.broadcasted_iota(jnp.int32, (r, ds), 1)
        qry_t = lax.broadcasted_iota(jnp.int32, (r, ds), 0) % ds
        s_own = jnp.where(key_t <= qry_t, _dot_nt(qbd, kn_ref[...]) * c2, NEG_INF)
        m_all = m_ref[...]
        m_tot = jnp.maximum(jnp.max(jnp.where(sel, m_all, NEG_INF), axis=-1, keepdims=True),
                            jnp.max(s_own, axis=-1, keepdims=True))
        w = jnp.where(sel, jnp.exp2(m_all - m_tot), 0.0)
        p_own = jnp.exp2(s_own - m_tot)
        den = jnp.sum(w * l_ref[...], axis=-1, keepdims=True) + jnp.sum(p_own, axis=-1, keepdims=True)
        num = _dot(p_own.astype(BF16), vn_ref[...])

        def body(j, num):
            wj = jnp.sum(jnp.where(lane_b == j, w, 0.0), axis=-1, keepdims=True)
            return num + wj * acc_ref[j]

        out = jnp.where(hmask, lax.fori_loop(0, nb, body, num) / den, 0.0)
        o = out[0:ds, :]
        for h in range(1, n_heads):
            o = o + out[h * ds:(h + 1) * ds, :]
        o_ref[...] = o.astype(o_ref.dtype)


def _moba_sample(q, kb, vb, cache_k, cache_v, page_table, ds):
    n, width = q.shape
    dbsz, n_pages = page_table.shape
    n_heads = width // HEAD_DIM
    assert (n_pages * PAGE_SIZE) % MOBA_BLOCK == 0, "cached length must fill whole MoBA blocks"
    nb = n_pages * PAGE_SIZE // MOBA_BLOCK
    assert nb >= MOBA_TOPK
    pg = PAGES_PER_STEP
    ppb = MOBA_BLOCK // PAGE_SIZE
    assert n_pages % pg == 0 and pg % ppb == 0 and ds % 8 == 0
    assert cache_k.shape[1:] == (PAGE_SIZE, n_heads, HEAD_DIM)
    r = n_heads * ds
    ck = jnp.transpose(cache_k, (0, 2, 3, 1)).reshape(cache_k.shape[0], width, PAGE_SIZE)
    cv = jnp.transpose(cache_v, (0, 2, 3, 1)).reshape(cache_v.shape[0], width, PAGE_SIZE)

    tok = pl.BlockSpec((ds, width), lambda b, s, pt: (b, 0))
    hbm = pl.BlockSpec(memory_space=pl.ANY)
    grid_spec = pltpu.PrefetchScalarGridSpec(
        num_scalar_prefetch=1, grid=(dbsz, n_pages // pg),
        in_specs=[tok, tok, tok, hbm, hbm],
        out_specs=tok,
        scratch_shapes=[pltpu.VMEM((r, nb), F32), pltpu.VMEM((r, nb), F32), pltpu.VMEM((r, nb), F32),
                        pltpu.VMEM((nb, r, width), F32),
                        pltpu.VMEM((PAGE_SLOTS, pg, width, PAGE_SIZE), F32),
                        pltpu.VMEM((PAGE_SLOTS, pg, width, PAGE_SIZE), F32),
                        pltpu.SemaphoreType.DMA((PAGE_SLOTS, 2))],
    )
    kern = functools.partial(_moba_sample_kernel, pg=pg, n_heads=n_heads, nb=nb, n_steps=n_pages // pg,
                             n_total=dbsz * (n_pages // pg))
    return pl.pallas_call(
        kern, grid_spec=grid_spec, out_shape=jax.ShapeDtypeStruct((n, width), BF16),
        name='moba_sample',
        compiler_params=pltpu.CompilerParams(dimension_semantics=('arbitrary', 'arbitrary'),
                                             vmem_limit_bytes=VMEM_LIMIT),
    )(page_table, q, kb, vb, ck, cv)


def _pack_bf16_pairs(x):
    w = x.shape[1] // 2
    xb = x.astype(BF16).astype(F32)
    hi = lax.bitcast_convert_type(xb[:, :w], jnp.uint32) & jnp.uint32(0xFFFF0000)
    lo = lax.bitcast_convert_type(xb[:, w:], jnp.uint32) >> 16
    return hi | lo


def _unpack_bf16_pairs(u):
    hi = lax.bitcast_convert_type(u & jnp.uint32(0xFFFF0000), F32)
    lo = lax.bitcast_convert_type(u << 16, F32)
    return jnp.concatenate([hi, lo], axis=1).astype(BF16)


def _post_attn_kernel(x_ref, aterm_ref, sgb_ref, ob_ref, wb_ref, wo_ref, gffn_ref, wr_ref, br_ref,
                      x1_ref, hf_ref, route_ref):
    merged = aterm_ref[...] + sgb_ref[...] * _dot(ob_ref[...], wb_ref[...])
    x1 = x_ref[...] + _dot(merged.astype(BF16), wo_ref[...])
    x1_ref[...] = x1
    hf = _rms(x1, gffn_ref[...])
    hf_ref[...] = _pack_bf16_pairs(hf)
    logits = _dot(hf.astype(BF16), wr_ref[...]) + br_ref[...]
    tm, lanes = logits.shape
    lane = lax.broadcasted_iota(jnp.int32, (tm, lanes), 1).astype(F32)
    n_exp = N_GROUPS * EXPERTS_PER_GROUP
    gmask = lane < N_GROUPS
    gl = jnp.where(gmask, logits, -jnp.inf)
    gmax = jnp.max(gl, axis=-1, keepdims=True)
    grp = jnp.min(jnp.where(gl == gmax, lane, float(lanes)), axis=-1, keepdims=True)
    p_grp = 1.0 / jnp.sum(jnp.where(gmask, jnp.exp(gl - gmax), 0.0), axis=-1, keepdims=True)
    lo = N_GROUPS + grp * EXPERTS_PER_GROUP
    emask = jnp.logical_and(jnp.logical_and(lane >= lo, lane < lo + EXPERTS_PER_GROUP),
                            lane < N_GROUPS + n_exp)
    e1 = jnp.where(emask, logits, -jnp.inf)
    v1 = jnp.max(e1, axis=-1, keepdims=True)
    j1 = jnp.min(jnp.where(e1 == v1, lane, float(lanes)), axis=-1, keepdims=True)
    e2 = jnp.where(lane == j1, -jnp.inf, e1)
    v2 = jnp.max(e2, axis=-1, keepdims=True)
    j2 = jnp.min(jnp.where(e2 == v2, lane, float(lanes)), axis=-1, keepdims=True)
    t = jnp.exp(v2 - v1)
    p1 = 1.0 / (1.0 + t)
    p2 = t / (1.0 + t)
    rec = jnp.where(lane == 0, j1 - N_GROUPS, 0.0)
    rec = jnp.where(lane == 1, j2 - N_GROUPS, rec)
    rec = jnp.where(lane == 2, p_grp * p1, rec)
    rec = jnp.where(lane == 3, p_grp * p2, rec)
    route_ref[...] = rec


def _post_attn(x, aterm, sgb, out_b, wts):
    n, d_model = x.shape
    tm = ROW_TILE
    row = lambda i: (i, 0)
    return pl.pallas_call(
        _post_attn_kernel, grid=(n // tm,),
        in_specs=[pl.BlockSpec((tm, d_model), row), pl.BlockSpec((tm, d_model), row),
                  pl.BlockSpec((tm, d_model), row), pl.BlockSpec((tm, out_b.shape[1]), row),
                  _const_spec(wts['w_b'].shape), _const_spec(wts['w_o'].shape),
                  _const_spec((1, d_model)), _const_spec(wts['w_r'].shape), _const_spec((1, PLE_LANES))],
        out_specs=[pl.BlockSpec((tm, d_model), row), pl.BlockSpec((tm, d_model // 2), row),
                   pl.BlockSpec((tm, PLE_LANES), row)],
        out_shape=[jax.ShapeDtypeStruct((n, d_model), F32),
                   jax.ShapeDtypeStruct((n, d_model // 2), jnp.uint32),
                   jax.ShapeDtypeStruct((n, PLE_LANES), F32)],
        name='post_attn',
        compiler_params=pltpu.CompilerParams(dimension_semantics=('arbitrary',),
                                             vmem_limit_bytes=VMEM_LIMIT),
    )(x, aterm, sgb, out_b, wts['w_b'], wts['w_o'], wts['g_ffn'], wts['w_r'], wts['b_r'])


def _dispatch_kernel(dest_ref, hf_ref, xd_in_ref, xd_ref, sem, *, fanout):
    del xd_in_ref
    tm = hf_ref.shape[0]

    for rr in range(tm):
        for kk in range(fanout):
            pltpu.make_async_copy(hf_ref.at[pl.ds(rr, 1)],
                                  xd_ref.at[pl.ds(dest_ref[0, 0, rr * fanout + kk], 1)], sem).start()
    for kk in range(fanout):
        pltpu.make_async_copy(hf_ref, xd_ref.at[pl.ds(0, tm)], sem).wait()


def _dispatch(hf, dest, n_rows):
    n, w = hf.shape
    fanout = dest.shape[0] // n
    tm = min(n, 512)
    steps = n // tm
    dest3 = dest.reshape(steps, 1, tm * fanout)
    xd0 = jnp.zeros((n_rows, w), hf.dtype)
    kern = functools.partial(_dispatch_kernel, fanout=fanout)
    return pl.pallas_call(
        kern, grid=(steps,),
        in_specs=[pl.BlockSpec((1, 1, tm * fanout), lambda i: (i, 0, 0), memory_space=pltpu.SMEM),
                  pl.BlockSpec((tm, w), lambda i: (i, 0)), pl.BlockSpec(memory_space=pl.ANY)],
        out_specs=pl.BlockSpec(memory_space=pl.ANY),
        out_shape=jax.ShapeDtypeStruct((n_rows, w), hf.dtype),
        scratch_shapes=[pltpu.SemaphoreType.DMA(())],
        input_output_aliases={2: 0},
        name='dispatch',
        compiler_params=pltpu.CompilerParams(dimension_semantics=('arbitrary',)),
    )(dest3, hf, xd0)


def _moe_ffn_kernel(be_ref, nreal_ref, xd_ref, w1_ref, w3_ref, w2_ref, yd_ref, w1b_ref, w3b_ref, w2b_ref):
    i = pl.program_id(0)
    live = i < nreal_ref[0]
    new_expert = jnp.logical_or(i == 0, be_ref[i] != be_ref[jnp.maximum(i - 1, 0)])

    @pl.when(jnp.logical_and(live, new_expert))
    def _():
        w1b_ref[...] = w1_ref[...].astype(BF16)
        w3b_ref[...] = w3_ref[...].astype(BF16)
        w2b_ref[...] = w2_ref[...].astype(BF16)

    @pl.when(live)
    def _():
        xb = _unpack_bf16_pairs(xd_ref[...])
        a = _dot(xb, w1b_ref[...])
        b = _dot(xb, w3b_ref[...])
        yd_ref[...] = _dot((jax.nn.silu(a) * b).astype(BF16), w2b_ref[...])

    @pl.when(i >= nreal_ref[0])
    def _():
        yd_ref[...] = jnp.zeros_like(yd_ref)


def _moe_ffn(xd, blk_e, n_real, w1, w3, w2):
    n_rows, half = xd.shape
    d_model = 2 * half
    ff = w1.shape[-1]
    blk = MOE_BLOCK
    grid_spec = pltpu.PrefetchScalarGridSpec(
        num_scalar_prefetch=2, grid=(n_rows // blk,),
        in_specs=[pl.BlockSpec((blk, half), lambda i, be, nr: (i, 0)),
                  pl.BlockSpec((None, d_model, ff), lambda i, be, nr: (be[i], 0, 0)),
                  pl.BlockSpec((None, d_model, ff), lambda i, be, nr: (be[i], 0, 0)),
                  pl.BlockSpec((None, ff, d_model), lambda i, be, nr: (be[i], 0, 0))],
        out_specs=pl.BlockSpec((blk, d_model), lambda i, be, nr: (i, 0)),
        scratch_shapes=[pltpu.VMEM((d_model, ff), BF16), pltpu.VMEM((d_model, ff), BF16),
                        pltpu.VMEM((ff, d_model), BF16)],
    )
    return pl.pallas_call(
        _moe_ffn_kernel, grid_spec=grid_spec,
        out_shape=jax.ShapeDtypeStruct((n_rows, d_model), F32),
        name='moe_ffn',
        compiler_params=pltpu.CompilerParams(dimension_semantics=('arbitrary',),
                                             vmem_limit_bytes=VMEM_LIMIT),
    )(blk_e, n_real, xd, w1, w3, w2)


def _final_kernel(d0_ref, dn_ref, yd_ref, x1_ref, route_ref, p_ref, gple_ref, wpg_ref, wple_ref,
                  y_ref, ybuf, sems, *, fanout):
    tm = x1_ref.shape[0]
    step = pl.program_id(0)
    slot = step % 2

    def start_rows(idx_ref, s, static_rows):
        def issue(rr, carry):
            for kk in range(fanout):
                pltpu.make_async_copy(yd_ref.at[pl.ds(idx_ref[0, 0, rr * fanout + kk], 1)],
                                      ybuf.at[s, kk, pl.ds(rr, 1)], sems.at[s]).start()
            return carry

        if static_rows:
            for rr in range(tm):
                issue(rr, 0)
        else:
            lax.fori_loop(0, tm, issue, 0, unroll=DMA_ISSUE_UNROLL)

    def wait_rows(s):
        for kk in range(fanout):
            pltpu.make_async_copy(yd_ref.at[pl.ds(0, tm)], ybuf.at[s, kk], sems.at[s]).wait()

    @pl.when(step == 0)
    def _():
        start_rows(d0_ref, 0, False)

    start_rows(dn_ref, 1 - slot, True)
    wait_rows(slot)

    route = route_ref[...]
    moe = route[:, 2:3] * ybuf[slot, 0]
    for kk in range(1, fanout):
        moe = moe + route[:, 2 + kk:3 + kk] * ybuf[slot, kk]
    x2 = x1_ref[...] + moe
    gate = jax.nn.sigmoid(_dot(_rms(x2, gple_ref[...]).astype(BF16), wpg_ref[...]))
    y_ref[...] = x2 + gate * _dot(p_ref[...].astype(BF16), wple_ref[...])

    @pl.when(step == pl.num_programs(0) - 1)
    def _():
        wait_rows(1 - slot)


def _final(x1, yd, dest, route, p, wts):
    n, d_model = x1.shape
    fanout = dest.shape[0] // n
    tm = ROW_TILE
    steps = n // tm
    dest3 = dest.reshape(steps, 1, tm * fanout)
    row = lambda i: (i, 0)
    kern = functools.partial(_final_kernel, fanout=fanout)
    return pl.pallas_call(
        kern, grid=(steps,),
        in_specs=[pl.BlockSpec((1, 1, tm * fanout), lambda i: (0, 0, 0), memory_space=pltpu.SMEM),
                  pl.BlockSpec((1, 1, tm * fanout), lambda i: (jnp.minimum(i + 1, steps - 1), 0, 0),
                               memory_space=pltpu.SMEM),
                  pl.BlockSpec(memory_space=pl.ANY),
                  pl.BlockSpec((tm, d_model), row), pl.BlockSpec((tm, PLE_LANES), row),
                  pl.BlockSpec((tm, p.shape[1]), row),
                  _const_spec((1, d_model)), _const_spec(wts['w_pg'].shape), _const_spec(wts['w_ple'].shape)],
        out_specs=pl.BlockSpec((tm, d_model), row),
        out_shape=jax.ShapeDtypeStruct((n, d_model), F32),
        scratch_shapes=[pltpu.VMEM((2, fanout, tm, d_model), F32), pltpu.SemaphoreType.DMA((2,))],
        name='final',
        compiler_params=pltpu.CompilerParams(dimension_semantics=('arbitrary',),
                                             vmem_limit_bytes=VMEM_LIMIT),
    )(dest3, dest3, yd, x1, route, p, wts['g_ple'], wts['w_pg'], wts['w_ple'])


def _routing_plan(eid, blk):
    n_exp = N_GROUPS * EXPERTS_PER_GROUP
    e = eid.reshape(-1)
    n_assign = e.shape[0]
    onehot = (e[:, None] == jnp.arange(n_exp, dtype=jnp.int32)[None, :]).astype(jnp.int32)
    counts = jnp.sum(onehot, axis=0)
    rank = jnp.take_along_axis(jnp.cumsum(onehot, axis=0), e[:, None], axis=1)[:, 0] - 1
    pcounts = ((counts + blk - 1) // blk) * blk
    pend = jnp.cumsum(pcounts)
    pstart = pend - pcounts
    dest = (pstart[e] + rank).astype(jnp.int32)
    n_blocks = -(-(n_assign + n_exp * (blk - 1)) // blk)
    blk_start = jnp.arange(n_blocks, dtype=jnp.int32) * blk
    blk_e = jnp.minimum(jnp.sum(pend[None, :] <= blk_start[:, None], axis=1), n_exp - 1).astype(jnp.int32)
    n_real = (pend[-1:] // blk).astype(jnp.int32)
    return dest, blk_e, n_real, n_blocks * blk


def _rope_tables(pos, n_heads):
    half = HEAD_DIM // 2
    inv = ROPE_THETA ** (-jnp.arange(half, dtype=F32) / half)
    ang = pos.astype(F32)[:, None] * inv[None, :]
    cos = jnp.cos(ang)
    sin = jnp.sin(ang)
    return (jnp.tile(jnp.concatenate([cos, cos], axis=-1), (1, n_heads)),
            jnp.tile(jnp.concatenate([-sin, sin], axis=-1), (1, n_heads)))


def _mix_tables(w_s_l, b_s_l, t_mix, a_width):
    reps = CHUNK // t_mix
    tri = jnp.tril(jnp.ones((t_mix, t_mix), F32))
    wt = w_s_l[:, :t_mix, :t_mix] * tri[None]
    eye = jnp.eye(reps, dtype=F32)
    wbig = jnp.einsum('ab,gts->gatbs', eye, wt).reshape(A_GROUPS, CHUNK, CHUNK)
    wmix = jnp.transpose(wbig, (1, 0, 2)).reshape(CHUNK, A_GROUPS * CHUNK)
    bs = jnp.tile(b_s_l[:, :t_mix], (1, reps))
    bs_tab = jnp.repeat(bs.T, a_width // A_GROUPS, axis=1)
    return dict(wmix=wmix.astype(BF16), bs_tab=bs_tab)


def _layer_weights(l, g_mix, w_in, g_v, g_q, g_k, w_a, w_b, w_o, g_ffn, w_rg, b_rg, w_re, b_re,
                   w1, w3, w2, g_ple, w_pg, w_ple):
    b_width = w_b.shape[1]
    n_heads = b_width // HEAD_DIM
    d_model = w_o.shape[-1]
    hid = jnp.arange(b_width) // HEAD_DIM
    hind = jnp.where(hid[:, None] == hid[None, :], 1.0 / HEAD_DIM, 0.0)
    n_exp = N_GROUPS * EXPERTS_PER_GROUP
    w_r = jnp.zeros((d_model, PLE_LANES), F32)
    w_r = w_r.at[:, :N_GROUPS].set(w_rg[l]).at[:, N_GROUPS:N_GROUPS + n_exp].set(w_re[l])
    b_r = jnp.zeros((1, PLE_LANES), F32)
    b_r = b_r.at[0, :N_GROUPS].set(b_rg[l]).at[0, N_GROUPS:N_GROUPS + n_exp].set(b_re[l])
    return dict(
        g_mix=g_mix[l][None], w_in=w_in[l].astype(BF16), g_v=g_v[l][None], hind=hind.astype(BF16),
        g_q=jnp.tile(g_q[l], n_heads)[None], g_k=jnp.tile(g_k[l], n_heads)[None],
        w_a=w_a[l].astype(BF16), w_b=w_b[l].astype(BF16), w_o=w_o[l].astype(BF16),
        g_ffn=g_ffn[l][None], w_r=w_r.astype(BF16), b_r=b_r,
        w1=w1[l], w3=w3[l], w2=w2[l],
        g_ple=g_ple[l][None], w_pg=w_pg[l].astype(BF16), w_ple=w_ple[l].astype(BF16),
    )


def _finish(x, aterm, sgb, out_b, p, wts):
    x1, hf, route = _post_attn(x, aterm, sgb, out_b, wts)
    eid = route[:, :2].astype(jnp.int32)
    dest, blk_e, n_real, n_rows = _routing_plan(eid, MOE_BLOCK)
    xd = _dispatch(hf, dest, n_rows)
    yd = _moe_ffn(xd, blk_e, n_real, wts['w1'], wts['w3'], wts['w2'])
    return _final(x1, yd, dest, route, p, wts)


def kernel(x_prompt, x_sample, cache_k, cache_v, page_table, p_prompt, p_sample, g_mix, w_in, g_v, w_s, b_s, g_q, g_k, w_a, w_b, w_o, g_ffn, w_router_group, b_router_group, w_router_expert, b_router_expert, w1, w3, w2, g_ple, w_ple_gate, w_ple):
    bsz, seq, d_model = x_prompt.shape
    dbsz, dseq, _ = x_sample.shape
    depth = g_mix.shape[0]
    b_width = w_b.shape[1]
    n_heads = b_width // HEAD_DIM
    past_len = page_table.shape[1] * PAGE_SIZE
    assert seq % MOBA_BLOCK == 0 and (bsz * seq) % ROW_TILE == 0 and (dbsz * dseq) % ROW_TILE == 0
    assert CHUNK % dseq == 0 and ROW_TILE % CHUNK == 0 and ROW_TILE % MOBA_BLOCK == 0
    assert seq // MOBA_BLOCK >= MOBA_TOPK
    params = (g_mix, w_in, g_v, g_q, g_k, w_a, w_b, w_o, g_ffn, w_router_group, b_router_group,
              w_router_expert, b_router_expert, w1, w3, w2, g_ple, w_ple_gate, w_ple)
    a_width = g_v.shape[-1]
    tab_p = _rope_tables(jnp.arange(seq, dtype=jnp.int32), n_heads)
    pos_s = past_len + (jnp.arange(ROW_TILE, dtype=jnp.int32) % dseq)
    tab_s = _rope_tables(pos_s, n_heads)
    xp = x_prompt.reshape(bsz * seq, d_model)
    xs = x_sample.reshape(dbsz * dseq, d_model)
    kp_rows, vp_rows, ks_rows, vs_rows, chunk_rows = [], [], [], [], []
    for l in range(depth):
        wts = _layer_weights(l, *params)
        wts_p = dict(wts, **_mix_tables(w_s[l], b_s[l], CHUNK, a_width))
        wts_s = dict(wts, **_mix_tables(w_s[l], b_s[l], dseq, a_width))
        aterm, sgb, qt, k, v, kb, vt, kmean = _inproj(
            xp, tab_p, wts_p, emit_kmean=True, emit_vchunk=False, pos_blocks=seq // ROW_TILE)
        nbt = bsz * seq // MOBA_BLOCK
        out_b = _moba_prompt(qt, kb.reshape(nbt, MOBA_BLOCK, b_width), vt,
                             kmean.reshape(bsz, seq // MOBA_BLOCK, b_width), bsz, seq)
        xp = _finish(xp, aterm, sgb, out_b, p_prompt[l].reshape(bsz * seq, -1), wts_p)
        kp_rows.append(jnp.transpose(k.reshape(bsz, n_heads, HEAD_DIM, seq), (0, 3, 1, 2)))
        vp_rows.append(jnp.transpose(v.reshape(bsz, n_heads, HEAD_DIM, seq), (0, 3, 1, 2)))
        aterm, sgb, q, k, v, kb, vb, vchunk = _inproj(
            xs, tab_s, wts_s, emit_kmean=False, emit_vchunk=True, pos_blocks=1)
        out_b = _moba_sample(q, kb, vb, cache_k[l], cache_v[l], page_table, dseq)
        xs = _finish(xs, aterm, sgb, out_b, p_sample[l].reshape(dbsz * dseq, -1), wts_s)
        ks_rows.append(k.reshape(dbsz, dseq, n_heads, HEAD_DIM))
        vs_rows.append(v.reshape(dbsz, dseq, n_heads, HEAD_DIM))
        chunk_rows.append(vchunk.reshape(dbsz, dseq, -1))
    return (xp.reshape(bsz, seq, d_model), xs.reshape(dbsz, dseq, d_model),
            jnp.stack(kp_rows), jnp.stack(vp_rows), jnp.stack(ks_rows), jnp.stack(vs_rows),
            jnp.stack(chunk_rows))
```

```python
import functools

import jax
import jax.numpy as jnp
from jax import lax
from jax.experimental import pallas as pl
from jax.experimental.pallas import tpu as pltpu

F32 = jnp.float32
BF16 = jnp.bfloat16

EPS = 1e-6
NEG_INF = -1e30
A_GROUPS = 8
CHUNK = 128
HEAD_DIM = 64
MOBA_BLOCK = 256
MOBA_TOPK = 3
ROPE_THETA = 10000.0
N_GROUPS = 4
EXPERTS_PER_GROUP = 8
PAGE_SIZE = 128
PLE_LANES = 128
ROUTE_ROWS = 8
ROW_TILE = 256
MOE_BLOCK = 256
PAGES_PER_STEP = 16
PAGE_SLOTS = 3
VMEM_LIMIT = 56 * 1024 * 1024
LOG2_E = 1.4426950408889634
ONES_ROWS = 16
DMA_ISSUE_UNROLL = 8

_NT = (((1,), (1,)), ((), ()))


def _rms(x, g):
    return x * lax.rsqrt(jnp.mean(x * x, axis=-1, keepdims=True) + EPS) * g


def _dot(a, b):
    return jnp.dot(a, b, preferred_element_type=F32)


def _dot_nt(a, b):
    return lax.dot_general(a, b, _NT, preferred_element_type=F32)


def _dot_tn(a, b):
    return lax.dot_general(a, b, (((0,), (0,)), ((), ())), preferred_element_type=F32)


def _const_spec(shape):
    return pl.BlockSpec(shape, lambda *_: (0,) * len(shape))


def _inproj_kernel(x_ref, gmix_ref, win_ref, gv_ref, wmix_ref, bs_ref, hind_ref, gq_ref, gk_ref,
                   cos_ref, sin_ref, wa_ref,
                   aterm_ref, sgb_ref, q_ref, k_ref, v_ref, kb_ref, vb_ref, *extra,
                   a_width, b_width, d_model, emit_kmean, emit_vchunk, transpose_qv):
    tm = x_ref.shape[0]
    h = _rms(x_ref[...], gmix_ref[...]).astype(BF16)
    offs = [0]

    def proj(width):
        o = offs[0]
        offs[0] = o + width
        return _dot(h, win_ref[:, o:o + width])

    zu = proj(a_width)
    zv = proj(a_width)
    zq = proj(b_width)
    zk = proj(b_width)
    zva = proj(b_width)

    u = jax.nn.gelu(zu)
    vn = _rms(jax.nn.gelu(zv), gv_ref[...])
    vb16 = vn.astype(BF16)
    lane_grp = lax.broadcasted_iota(jnp.int32, (CHUNK, a_width), 1) // (a_width // A_GROUPS)
    parts = []
    for c in range(tm // CHUNK):
        vc = vb16[c * CHUNK:(c + 1) * CHUNK, :]
        rhs = jnp.concatenate(
            [jnp.where(lane_grp == g, vc, jnp.zeros_like(vc)) for g in range(A_GROUPS)], axis=0)
        parts.append(_dot(wmix_ref[...], rhs) + bs_ref[...])
    s = parts[0] if len(parts) == 1 else jnp.concatenate(parts, axis=0)
    out_a = (u * s).astype(BF16)
    ga = proj(d_model)
    aterm_ref[...] = jax.nn.sigmoid(ga) * _dot(out_a, wa_ref[...])
    gb = proj(d_model)
    sgb_ref[...] = jax.nn.sigmoid(gb)

    lane = lax.broadcasted_iota(jnp.int32, (tm, b_width), 1)
    first_half = (lane % HEAD_DIM) < (HEAD_DIM // 2)
    cos = cos_ref[...]
    sin = sin_ref[...]

    def headnorm_rope(z, g):
        ms = _dot((z * z).astype(BF16), hind_ref[...])
        y = z * lax.rsqrt(ms + EPS) * g
        swapped = jnp.where(first_half,
                            pltpu.roll(y, b_width - HEAD_DIM // 2, 1),
                            pltpu.roll(y, HEAD_DIM // 2, 1))
        return y * cos + swapped * sin

    q = headnorm_rope(zq, gq_ref[...])
    k = headnorm_rope(zk, gk_ref[...])
    kb_ref[...] = k.astype(BF16)
    if transpose_qv:
        vt = zva.T
        q_ref[0] = q.T.astype(BF16)
        k_ref[0] = k.T
        v_ref[0] = vt
        vb_ref[0] = vt.astype(BF16)
    else:
        q_ref[...] = q.astype(BF16)
        k_ref[...] = k
        v_ref[...] = zva
        vb_ref[...] = zva.astype(BF16)
    idx = 0
    if emit_kmean:
        km_ref = extra[idx]
        idx += 1
        for bi in range(tm // MOBA_BLOCK):
            km_ref[bi] = jnp.mean(k[bi * MOBA_BLOCK:(bi + 1) * MOBA_BLOCK, :], axis=0, keepdims=True)
    if emit_vchunk:
        extra[idx][...] = vn


def _inproj(x, pos_tables, wts, *, emit_kmean, emit_vchunk, pos_blocks):
    n, d_model = x.shape
    a_width = wts['g_v'].shape[-1]
    b_width = wts['hind'].shape[0]
    tm = ROW_TILE
    cos_t, sin_t = pos_tables
    row = lambda i: (i, 0)
    posrow = lambda i: (i % pos_blocks, 0)
    in_specs = [
        pl.BlockSpec((tm, d_model), row),
        _const_spec((1, d_model)),
        _const_spec(wts['w_in'].shape),
        _const_spec((1, a_width)),
        _const_spec(wts['wmix'].shape),
        _const_spec(wts['bs_tab'].shape),
        _const_spec(wts['hind'].shape),
        _const_spec((1, b_width)),
        _const_spec((1, b_width)),
        pl.BlockSpec((tm, b_width), posrow),
        pl.BlockSpec((tm, b_width), posrow),
        _const_spec(wts['w_a'].shape),
    ]
    transpose_qv = emit_kmean
    if transpose_qv:
        assert tm == MOBA_BLOCK
        qv_shape = jax.ShapeDtypeStruct((n // tm, b_width, tm), BF16)
        qv_spec = pl.BlockSpec((1, b_width, tm), lambda i: (i, 0, 0))
        kv_shape = jax.ShapeDtypeStruct((n // (tm * pos_blocks), b_width, tm * pos_blocks), F32)
        kv_spec = pl.BlockSpec((1, b_width, tm), lambda i: (i // pos_blocks, 0, i % pos_blocks))
    else:
        qv_shape = jax.ShapeDtypeStruct((n, b_width), BF16)
        qv_spec = pl.BlockSpec((tm, b_width), row)
        kv_shape = jax.ShapeDtypeStruct((n, b_width), F32)
        kv_spec = pl.BlockSpec((tm, b_width), row)
    out_shape = [
        jax.ShapeDtypeStruct((n, d_model), F32),
        jax.ShapeDtypeStruct((n, d_model), F32),
        qv_shape,
        kv_shape,
        kv_shape,
        jax.ShapeDtypeStruct((n, b_width), BF16),
        qv_shape,
    ]
    out_specs = [
        pl.BlockSpec((tm, d_model), row), pl.BlockSpec((tm, d_model), row),
        qv_spec, kv_spec, kv_spec, pl.BlockSpec((tm, b_width), row),
        qv_spec,
    ]
    if emit_kmean:
        nbt = tm // MOBA_BLOCK
        out_shape.append(jax.ShapeDtypeStruct((n // MOBA_BLOCK, 1, b_width), F32))
        out_specs.append(pl.BlockSpec((nbt, 1, b_width), lambda i: (i, 0, 0)))
    if emit_vchunk:
        out_shape.append(jax.ShapeDtypeStruct((n, a_width), F32))
        out_specs.append(pl.BlockSpec((tm, a_width), row))
    kern = functools.partial(_inproj_kernel, a_width=a_width, b_width=b_width, d_model=d_model,
                             emit_kmean=emit_kmean, emit_vchunk=emit_vchunk, transpose_qv=transpose_qv)
    return pl.pallas_call(
        kern, grid=(n // tm,), in_specs=in_specs, out_specs=out_specs, out_shape=out_shape,
        name='inproj',
        compiler_params=pltpu.CompilerParams(dimension_semantics=('arbitrary',),
                                             vmem_limit_bytes=VMEM_LIMIT),
    )(x, wts['g_mix'], wts['w_in'], wts['g_v'], wts['wmix'], wts['bs_tab'], wts['hind'],
      wts['g_q'], wts['g_k'], cos_t, sin_t, wts['w_a'])


def _select_topk(scores, allowed, blk_f, nb):
    sel = jnp.zeros(scores.shape, jnp.bool_)
    for _ in range(MOBA_TOPK):
        cand = jnp.logical_and(allowed, jnp.logical_not(sel))
        scm = jnp.where(cand, scores, -jnp.inf)
        mx = jnp.max(scm, axis=-1, keepdims=True)
        is_max = jnp.logical_and(cand, scm == mx)
        first = jnp.min(jnp.where(is_max, blk_f, float(nb)), axis=-1, keepdims=True)
        sel = jnp.logical_or(sel, jnp.logical_and(is_max, blk_f == first))
    return sel


def _select_topk_rows(scores, allowed, blk_f, nb):
    sel = jnp.zeros(scores.shape, jnp.bool_)
    for _ in range(MOBA_TOPK):
        cand = jnp.logical_and(allowed, jnp.logical_not(sel))
        scm = jnp.where(cand, scores, -jnp.inf)
        mx = jnp.max(scm, axis=0, keepdims=True)
        is_max = jnp.logical_and(cand, scm == mx)
        first = jnp.min(jnp.where(is_max, blk_f, float(nb)), axis=0, keepdims=True)
        sel = jnp.logical_or(sel, jnp.logical_and(is_max, blk_f == first))
    return sel


def _moba_prompt_kernel(qt_ref, kb_ref, vt_ref, km_ref, o_ref, w_ref, bias_ref, m_ref, l_ref, acc_ref,
                        s_ref, *, n_heads, nb):
    tq = qt_ref.shape[2]
    blk = MOBA_BLOCK
    pair = 2 * HEAD_DIM
    i = pl.program_id(1)
    scale = HEAD_DIM ** -0.5 * LOG2_E
    km = km_ref[0].astype(BF16)
    blk_f = lax.broadcasted_iota(jnp.int32, (nb, tq), 0).astype(F32)
    allowed = blk_f < i.astype(F32)
    key_t = lax.broadcasted_iota(jnp.int32, (blk, tq), 0)
    qry_t = lax.broadcasted_iota(jnp.int32, (blk, tq), 1)
    causal = key_t <= qry_t
    zeros = jnp.zeros((HEAD_DIM, tq), BF16)
    ones = jnp.ones((ONES_ROWS, blk), BF16)

    for h in range(n_heads):
        qth = qt_ref[0, h * HEAD_DIM:(h + 1) * HEAD_DIM, :]
        sel = _select_topk_rows(_dot(km[:, h * HEAD_DIM:(h + 1) * HEAD_DIM], qth), allowed, blk_f, nb)
        bias_ref[h] = jnp.where(sel, 0.0, NEG_INF)
        qs = (qth.astype(F32) * scale).astype(BF16)
        col = jnp.concatenate([qs, zeros] if h % 2 == 0 else [zeros, qs], axis=0)
        w_ref[h // 2, :, (h % 2) * tq:(h % 2 + 1) * tq] = col

    def scores(j, hp):
        return _dot(kb_ref[j, :, hp * pair:(hp + 1) * pair], w_ref[hp])

    def weighted_values(j, h, p):
        vt1 = jnp.concatenate([vt_ref[j, h * HEAD_DIM:(h + 1) * HEAD_DIM, :], ones], axis=0)
        pv = _dot(vt1, p.astype(BF16))
        return pv[:HEAD_DIM, :], pv[HEAD_DIM:HEAD_DIM + 1, :]

    for hp in range(n_heads // 2):
        s2 = scores(i, hp)
        for h in (2 * hp, 2 * hp + 1):
            s = jnp.where(causal, s2[:, (h % 2) * tq:(h % 2 + 1) * tq], NEG_INF)
            m = jnp.max(s, axis=0, keepdims=True)
            pv, psum = weighted_values(i, h, jnp.exp2(s - m))
            m_ref[h:h + 1, :] = m
            l_ref[h:h + 1, :] = psum
            acc_ref[h * HEAD_DIM:(h + 1) * HEAD_DIM, :] = pv

    def stage_scores(j, slot):
        jc = jnp.minimum(j, nb - 1)
        for hp in range(n_heads // 2):
            s_ref[slot, hp] = scores(jc, hp)

    def consume(j, slot):
        jc = jnp.minimum(j, nb - 1)
        for h in range(n_heads):
            s = s_ref[slot, h // 2, :, (h % 2) * tq:(h % 2 + 1) * tq]
            bias = bias_ref[h, pl.ds(jc, 1), :]
            m = m_ref[h:h + 1, :]
            m_new = jnp.maximum(m, jnp.max(s, axis=0, keepdims=True) + bias)
            alpha = jnp.exp2(m - m_new)
            pv, psum = weighted_values(jc, h, jnp.exp2(s + (bias - m_new)))
            m_ref[h:h + 1, :] = m_new
            l_ref[h:h + 1, :] = alpha * l_ref[h:h + 1, :] + psum
            rows = slice(h * HEAD_DIM, (h + 1) * HEAD_DIM)
            acc_ref[rows, :] = alpha * acc_ref[rows, :] + pv

    @pl.when(i > 0)
    def _():
        stage_scores(0, 0)

        def body(t, carry):
            j = 2 * t
            stage_scores(j + 1, 1)
            consume(j, 0)
            stage_scores(j + 2, 0)
            consume(j + 1, 1)
            return carry

        lax.fori_loop(0, (i + 1) // 2, body, 0)
    outs = [acc_ref[h * HEAD_DIM:(h + 1) * HEAD_DIM, :] / l_ref[h:h + 1, :] for h in range(n_heads)]
    o_ref[...] = jnp.concatenate(outs, axis=0).T.astype(o_ref.dtype)


def _moba_prompt(qt, kb, vt, kmean, bsz, seq):
    _, width, tq = qt.shape
    n_heads = width // HEAD_DIM
    nb = seq // MOBA_BLOCK
    assert tq == MOBA_BLOCK and n_heads % 2 == 0
    kern = functools.partial(_moba_prompt_kernel, n_heads=n_heads, nb=nb)
    return pl.pallas_call(
        kern, grid=(bsz, nb),
        in_specs=[
            pl.BlockSpec((1, width, tq), lambda b, i: (b * nb + i, 0, 0)),
            pl.BlockSpec((nb, MOBA_BLOCK, width), lambda b, i: (b, 0, 0)),
            pl.BlockSpec((nb, width, MOBA_BLOCK), lambda b, i: (b, 0, 0)),
            pl.BlockSpec((1, nb, width), lambda b, i: (b, 0, 0)),
        ],
        out_specs=pl.BlockSpec((tq, width), lambda b, i: (b * nb + i, 0)),
        out_shape=jax.ShapeDtypeStruct((bsz * seq, width), BF16),
        scratch_shapes=[pltpu.VMEM((n_heads // 2, 2 * HEAD_DIM, 2 * tq), BF16),
                        pltpu.VMEM((n_heads, nb, tq), F32),
                        pltpu.VMEM((n_heads, tq), F32), pltpu.VMEM((n_heads, tq), F32),
                        pltpu.VMEM((width, tq), F32),
                        pltpu.VMEM((2, n_heads // 2, MOBA_BLOCK, 2 * tq), F32)],
        name='moba_prompt',
        compiler_params=pltpu.CompilerParams(dimension_semantics=('arbitrary', 'arbitrary'),
                                             vmem_limit_bytes=VMEM_LIMIT),
    )(qt, kb, vt, kmean)


def _moba_sample_kernel(pt_ref, q_ref, kn_ref, vn_ref, *rest, pg, n_heads, nb, n_steps, n_total):
    ck_ref, cv_ref, o_ref, bs_ref, m_ref, l_ref, acc_ref, kbuf, vbuf, sems = rest
    ds, width = q_ref.shape
    step = pl.program_id(1)
    n_slots = kbuf.shape[0]
    ahead = n_slots - 1
    lin = pl.program_id(0) * n_steps + step
    slot = lin % n_slots

    def fetch(idx):
        bb = idx // n_steps
        ss = idx % n_steps
        sl = idx % n_slots
        for t in range(pg):
            page = pt_ref[bb, ss * pg + t]
            pltpu.make_async_copy(ck_ref.at[page], kbuf.at[sl, t], sems.at[sl, 0]).start()
            pltpu.make_async_copy(cv_ref.at[page], vbuf.at[sl, t], sems.at[sl, 1]).start()

    @pl.when(lin == 0)
    def _():
        for a in range(min(ahead, n_total)):
            fetch(jnp.int32(a))

    @pl.when(lin + ahead < n_total)
    def _():
        fetch(lin + ahead)

    pltpu.make_async_copy(ck_ref.at[pl.ds(0, pg)], kbuf.at[slot], sems.at[slot, 0]).wait()
    pltpu.make_async_copy(cv_ref.at[pl.ds(0, pg)], vbuf.at[slot], sems.at[slot, 1]).wait()
    k_refs = [kbuf.at[slot, t] for t in range(pg)]
    v_refs = [vbuf.at[slot, t] for t in range(pg)]
    r = n_heads * ds
    c2 = HEAD_DIM ** -0.5 * LOG2_E
    ppb = MOBA_BLOCK // PAGE_SIZE
    row_h = lax.broadcasted_iota(jnp.int32, (r, width), 0) // ds
    lane_h = lax.broadcasted_iota(jnp.int32, (r, width), 1) // HEAD_DIM
    hmask = row_h == lane_h
    qt = jnp.concatenate([q_ref[...]] * n_heads, axis=0)
    qbd = jnp.where(hmask, qt, jnp.zeros_like(qt))
    lane_b = lax.broadcasted_iota(jnp.int32, (r, nb), 1)

    @pl.when(step == 0)
    def _():
        bs_ref[...] = jnp.zeros_like(bs_ref)
        m_ref[...] = jnp.zeros_like(m_ref)
        l_ref[...] = jnp.zeros_like(l_ref)

    bs_new, m_new, l_new = bs_ref[...], m_ref[...], l_ref[...]
    kall = jnp.concatenate([k_refs[t][...] for t in range(pg)], axis=1).astype(BF16)
    st_all = _dot(qbd, kall)
    for c in range(pg // ppb):
        st = st_all[:, c * MOBA_BLOCK:(c + 1) * MOBA_BLOCK]
        vblk = jnp.concatenate([v_refs[ppb * c + t][...] for t in range(ppb)], axis=1).astype(BF16)
        s = st * c2
        m = jnp.max(s, axis=-1, keepdims=True)
        p = jnp.exp2(s - m)
        jb = step * (pg // ppb) + c
        col = lane_b == jb
        bs_new = jnp.where(col, jnp.mean(st, axis=-1, keepdims=True), bs_new)
        m_new = jnp.where(col, m, m_new)
        l_new = jnp.where(col, jnp.sum(p, axis=-1, keepdims=True), l_new)
        acc_ref[jb] = _dot_nt(p.astype(BF16), vblk)
    bs_ref[...] = bs_new
    m_ref[...] = m_new
    l_ref[...] = l_new

    @pl.when(step == n_steps - 1)
    def _():
        sel = _select_topk(bs_ref[...], jnp.ones((r, nb), jnp.bool_), lane_b.astype(F32), nb)
        key_t = lax.broadcasted_iota(jnp.int32, (r, ds), 1)
        qry_t = lax.broadcasted_iota(jnp.int32, (r, ds), 0) % ds
        s_own = jnp.where(key_t <= qry_t, _dot_nt(qbd, kn_ref[...]) * c2, NEG_INF)
        m_all = m_ref[...]
        m_tot = jnp.maximum(jnp.max(jnp.where(sel, m_all, NEG_INF), axis=-1, keepdims=True),
                            jnp.max(s_own, axis=-1, keepdims=True))
        w = jnp.where(sel, jnp.exp2(m_all - m_tot), 0.0)
        p_own = jnp.exp2(s_own - m_tot)
        den = jnp.sum(w * l_ref[...], axis=-1, keepdims=True) + jnp.sum(p_own, axis=-1, keepdims=True)
        num = _dot(p_own.astype(BF16), vn_ref[...])

        def body(j, num):
            wj = jnp.sum(jnp.where(lane_b == j, w, 0.0), axis=-1, keepdims=True)
            return num + wj * acc_ref[j]

        out = jnp.where(hmask, lax.fori_loop(0, nb, body, num) / den, 0.0)
        o = out[0:ds, :]
        for h in range(1, n_heads):
            o = o + out[h * ds:(h + 1) * ds, :]
        o_ref[...] = o.astype(o_ref.dtype)


def _moba_sample(q, kb, vb, cache_k, cache_v, page_table, ds):
    n, width = q.shape
    dbsz, n_pages = page_table.shape
    n_heads = width // HEAD_DIM
    assert (n_pages * PAGE_SIZE) % MOBA_BLOCK == 0, "cached length must fill whole MoBA blocks"
    nb = n_pages * PAGE_SIZE // MOBA_BLOCK
    assert nb >= MOBA_TOPK
    pg = PAGES_PER_STEP
    ppb = MOBA_BLOCK // PAGE_SIZE
    assert n_pages % pg == 0 and pg % ppb == 0 and ds % 8 == 0
    assert cache_k.shape[1:] == (PAGE_SIZE, n_heads, HEAD_DIM)
    r = n_heads * ds
    ck = jnp.transpose(cache_k, (0, 2, 3, 1)).reshape(cache_k.shape[0], width, PAGE_SIZE)
    cv = jnp.transpose(cache_v, (0, 2, 3, 1)).reshape(cache_v.shape[0], width, PAGE_SIZE)

    tok = pl.BlockSpec((ds, width), lambda b, s, pt: (b, 0))
    hbm = pl.BlockSpec(memory_space=pl.ANY)
    grid_spec = pltpu.PrefetchScalarGridSpec(
        num_scalar_prefetch=1, grid=(dbsz, n_pages // pg),
        in_specs=[tok, tok, tok, hbm, hbm],
        out_specs=tok,
        scratch_shapes=[pltpu.VMEM((r, nb), F32), pltpu.VMEM((r, nb), F32), pltpu.VMEM((r, nb), F32),
                        pltpu.VMEM((nb, r, width), F32),
                        pltpu.VMEM((PAGE_SLOTS, pg, width, PAGE_SIZE), F32),
                        pltpu.VMEM((PAGE_SLOTS, pg, width, PAGE_SIZE), F32),
                        pltpu.SemaphoreType.DMA((PAGE_SLOTS, 2))],
    )
    kern = functools.partial(_moba_sample_kernel, pg=pg, n_heads=n_heads, nb=nb, n_steps=n_pages // pg,
                             n_total=dbsz * (n_pages // pg))
    return pl.pallas_call(
        kern, grid_spec=grid_spec, out_shape=jax.ShapeDtypeStruct((n, width), BF16),
        name='moba_sample',
        compiler_params=pltpu.CompilerParams(dimension_semantics=('arbitrary', 'arbitrary'),
                                             vmem_limit_bytes=VMEM_LIMIT),
    )(page_table, q, kb, vb, ck, cv)


def _pack_bf16_pairs(x):
    w = x.shape[1] // 2
    xb = x.astype(BF16).astype(F32)
    hi = lax.bitcast_convert_type(xb[:, :w], jnp.uint32) & jnp.uint32(0xFFFF0000)
    lo = lax.bitcast_convert_type(xb[:, w:], jnp.uint32) >> 16
    return hi | lo


def _unpack_bf16_pairs(u):
    hi = lax.bitcast_convert_type(u & jnp.uint32(0xFFFF0000), F32)
    lo = lax.bitcast_convert_type(u << 16, F32)
    return jnp.concatenate([hi, lo], axis=1).astype(BF16)


def _post_attn_kernel(x_ref, aterm_ref, sgb_ref, ob_ref, wb_ref, wo_ref, gffn_ref, wr_ref, br_ref,
                      x1_ref, hf_ref, route_ref, route_t_ref):
    merged = aterm_ref[...] + sgb_ref[...] * _dot(ob_ref[...], wb_ref[...])
    x1 = x_ref[...] + _dot(merged.astype(BF16), wo_ref[...])
    x1_ref[...] = x1
    hf = _rms(x1, gffn_ref[...])
    hf_ref[...] = _pack_bf16_pairs(hf)
    logits = _dot(hf.astype(BF16), wr_ref[...]) + br_ref[...]
    tm, lanes = logits.shape
    lane = lax.broadcasted_iota(jnp.int32, (tm, lanes), 1).astype(F32)
    n_exp = N_GROUPS * EXPERTS_PER_GROUP
    gmask = lane < N_GROUPS
    gl = jnp.where(gmask, logits, -jnp.inf)
    gmax = jnp.max(gl, axis=-1, keepdims=True)
    grp = jnp.min(jnp.where(gl == gmax, lane, float(lanes)), axis=-1, keepdims=True)
    p_grp = 1.0 / jnp.sum(jnp.where(gmask, jnp.exp(gl - gmax), 0.0), axis=-1, keepdims=True)
    lo = N_GROUPS + grp * EXPERTS_PER_GROUP
    emask = jnp.logical_and(jnp.logical_and(lane >= lo, lane < lo + EXPERTS_PER_GROUP),
                            lane < N_GROUPS + n_exp)
    e1 = jnp.where(emask, logits, -jnp.inf)
    v1 = jnp.max(e1, axis=-1, keepdims=True)
    j1 = jnp.min(jnp.where(e1 == v1, lane, float(lanes)), axis=-1, keepdims=True)
    e2 = jnp.where(lane == j1, -jnp.inf, e1)
    v2 = jnp.max(e2, axis=-1, keepdims=True)
    j2 = jnp.min(jnp.where(e2 == v2, lane, float(lanes)), axis=-1, keepdims=True)
    t = jnp.exp(v2 - v1)
    p1 = 1.0 / (1.0 + t)
    p2 = t / (1.0 + t)
    rec = jnp.where(lane == 0, j1 - N_GROUPS, 0.0)
    rec = jnp.where(lane == 1, j2 - N_GROUPS, rec)
    rec = jnp.where(lane == 2, p_grp * p1, rec)
    rec = jnp.where(lane == 3, p_grp * p2, rec)
    route_ref[...] = rec
    route_t_ref[...] = rec.T[:ROUTE_ROWS, :]


def _post_attn(x, aterm, sgb, out_b, wts):
    n, d_model = x.shape
    tm = ROW_TILE
    row = lambda i: (i, 0)
    return pl.pallas_call(
        _post_attn_kernel, grid=(n // tm,),
        in_specs=[pl.BlockSpec((tm, d_model), row), pl.BlockSpec((tm, d_model), row),
                  pl.BlockSpec((tm, d_model), row), pl.BlockSpec((tm, out_b.shape[1]), row),
                  _const_spec(wts['w_b'].shape), _const_spec(wts['w_o'].shape),
                  _const_spec((1, d_model)), _const_spec(wts['w_r'].shape), _const_spec((1, PLE_LANES))],
        out_specs=[pl.BlockSpec((tm, d_model), row), pl.BlockSpec((tm, d_model // 2), row),
                   pl.BlockSpec((tm, PLE_LANES), row), pl.BlockSpec((ROUTE_ROWS, tm), lambda i: (0, i))],
        out_shape=[jax.ShapeDtypeStruct((n, d_model), F32),
                   jax.ShapeDtypeStruct((n, d_model // 2), jnp.uint32),
                   jax.ShapeDtypeStruct((n, PLE_LANES), F32),
                   jax.ShapeDtypeStruct((ROUTE_ROWS, n), F32)],
        name='post_attn',
        compiler_params=pltpu.CompilerParams(dimension_semantics=('arbitrary',),
                                             vmem_limit_bytes=VMEM_LIMIT),
    )(x, aterm, sgb, out_b, wts['w_b'], wts['w_o'], wts['g_ffn'], wts['w_r'], wts['b_r'])


def _dispatch_kernel(dest_ref, hf_ref, xd_in_ref, xd_ref, sem, *, fanout):
    del xd_in_ref
    tm = hf_ref.shape[0]

    for rr in range(tm):
        for kk in range(fanout):
            pltpu.make_async_copy(hf_ref.at[pl.ds(rr, 1)],
                                  xd_ref.at[pl.ds(dest_ref[0, kk, rr], 1)], sem).start()
    for kk in range(fanout):
        pltpu.make_async_copy(hf_ref, xd_ref.at[pl.ds(0, tm)], sem).wait()


def _dest_tiles(dest, tm):
    fanout, n = dest.shape
    return dest.reshape(fanout, n // tm, tm).transpose(1, 0, 2)


def _dispatch(hf, dest, n_rows):
    n, w = hf.shape
    fanout = dest.shape[0]
    tm = min(n, 512)
    steps = n // tm
    dest3 = _dest_tiles(dest, tm)
    xd0 = jnp.zeros((n_rows, w), hf.dtype)
    kern = functools.partial(_dispatch_kernel, fanout=fanout)
    return pl.pallas_call(
        kern, grid=(steps,),
        in_specs=[pl.BlockSpec((1, fanout, tm), lambda i: (i, 0, 0), memory_space=pltpu.SMEM),
                  pl.BlockSpec((tm, w), lambda i: (i, 0)), pl.BlockSpec(memory_space=pl.ANY)],
        out_specs=pl.BlockSpec(memory_space=pl.ANY),
        out_shape=jax.ShapeDtypeStruct((n_rows, w), hf.dtype),
        scratch_shapes=[pltpu.SemaphoreType.DMA(())],
        input_output_aliases={2: 0},
        name='dispatch',
        compiler_params=pltpu.CompilerParams(dimension_semantics=('arbitrary',)),
    )(dest3, hf, xd0)


def _moe_ffn_kernel(be_ref, nreal_ref, xd_ref, w1_ref, w3_ref, w2_ref, yd_ref, w1b_ref, w3b_ref, w2b_ref):
    i = pl.program_id(0)
    live = i < nreal_ref[0]
    new_expert = jnp.logical_or(i == 0, be_ref[i] != be_ref[jnp.maximum(i - 1, 0)])

    @pl.when(jnp.logical_and(live, new_expert))
    def _():
        w1b_ref[...] = w1_ref[...].astype(BF16)
        w3b_ref[...] = w3_ref[...].astype(BF16)
        w2b_ref[...] = w2_ref[...].astype(BF16)

    @pl.when(live)
    def _():
        xb = _unpack_bf16_pairs(xd_ref[...])
        a = _dot(xb, w1b_ref[...])
        b = _dot(xb, w3b_ref[...])
        yd_ref[...] = _dot((jax.nn.silu(a) * b).astype(BF16), w2b_ref[...])

    @pl.when(i >= nreal_ref[0])
    def _():
        yd_ref[...] = jnp.zeros_like(yd_ref)


def _moe_ffn(xd, blk_e, n_real, w1, w3, w2):
    n_rows, half = xd.shape
    d_model = 2 * half
    ff = w1.shape[-1]
    blk = MOE_BLOCK
    grid_spec = pltpu.PrefetchScalarGridSpec(
        num_scalar_prefetch=2, grid=(n_rows // blk,),
        in_specs=[pl.BlockSpec((blk, half), lambda i, be, nr: (i, 0)),
                  pl.BlockSpec((None, d_model, ff), lambda i, be, nr: (be[i], 0, 0)),
                  pl.BlockSpec((None, d_model, ff), lambda i, be, nr: (be[i], 0, 0)),
                  pl.BlockSpec((None, ff, d_model), lambda i, be, nr: (be[i], 0, 0))],
        out_specs=pl.BlockSpec((blk, d_model), lambda i, be, nr: (i, 0)),
        scratch_shapes=[pltpu.VMEM((d_model, ff), BF16), pltpu.VMEM((d_model, ff), BF16),
                        pltpu.VMEM((ff, d_model), BF16)],
    )
    return pl.pallas_call(
        _moe_ffn_kernel, grid_spec=grid_spec,
        out_shape=jax.ShapeDtypeStruct((n_rows, d_model), F32),
        name='moe_ffn',
        compiler_params=pltpu.CompilerParams(dimension_semantics=('arbitrary',),
                                             vmem_limit_bytes=VMEM_LIMIT),
    )(blk_e, n_real, xd, w1, w3, w2)


def _final_kernel(d0_ref, dn_ref, yd_ref, x1_ref, route_ref, p_ref, gple_ref, wpg_ref, wple_ref,
                  y_ref, ybuf, sems, *, fanout):
    tm = x1_ref.shape[0]
    step = pl.program_id(0)
    slot = step % 2

    def start_rows(idx_ref, s, static_rows):
        def issue(rr, carry):
            for kk in range(fanout):
                pltpu.make_async_copy(yd_ref.at[pl.ds(idx_ref[0, kk, rr], 1)],
                                      ybuf.at[s, kk, pl.ds(rr, 1)], sems.at[s]).start()
            return carry

        if static_rows:
            for rr in range(tm):
                issue(rr, 0)
        else:
            lax.fori_loop(0, tm, issue, 0, unroll=DMA_ISSUE_UNROLL)

    def wait_rows(s):
        for kk in range(fanout):
            pltpu.make_async_copy(yd_ref.at[pl.ds(0, tm)], ybuf.at[s, kk], sems.at[s]).wait()

    @pl.when(step == 0)
    def _():
        start_rows(d0_ref, 0, False)

    start_rows(dn_ref, 1 - slot, True)
    wait_rows(slot)

    route = route_ref[...]
    moe = route[:, 2:3] * ybuf[slot, 0]
    for kk in range(1, fanout):
        moe = moe + route[:, 2 + kk:3 + kk] * ybuf[slot, kk]
    x2 = x1_ref[...] + moe
    gate = jax.nn.sigmoid(_dot(_rms(x2, gple_ref[...]).astype(BF16), wpg_ref[...]))
    y_ref[...] = x2 + gate * _dot(p_ref[...].astype(BF16), wple_ref[...])

    @pl.when(step == pl.num_programs(0) - 1)
    def _():
        wait_rows(1 - slot)


def _final(x1, yd, dest, route, p, wts):
    n, d_model = x1.shape
    fanout = dest.shape[0]
    tm = ROW_TILE
    steps = n // tm
    dest3 = _dest_tiles(dest, tm)
    row = lambda i: (i, 0)
    kern = functools.partial(_final_kernel, fanout=fanout)
    return pl.pallas_call(
        kern, grid=(steps,),
        in_specs=[pl.BlockSpec((1, fanout, tm), lambda i: (0, 0, 0), memory_space=pltpu.SMEM),
                  pl.BlockSpec((1, fanout, tm), lambda i: (jnp.minimum(i + 1, steps - 1), 0, 0),
                               memory_space=pltpu.SMEM),
                  pl.BlockSpec(memory_space=pl.ANY),
                  pl.BlockSpec((tm, d_model), row), pl.BlockSpec((tm, PLE_LANES), row),
                  pl.BlockSpec((tm, p.shape[1]), row),
                  _const_spec((1, d_model)), _const_spec(wts['w_pg'].shape), _const_spec(wts['w_ple'].shape)],
        out_specs=pl.BlockSpec((tm, d_model), row),
        out_shape=jax.ShapeDtypeStruct((n, d_model), F32),
        scratch_shapes=[pltpu.VMEM((2, fanout, tm, d_model), F32), pltpu.SemaphoreType.DMA((2,))],
        name='final',
        compiler_params=pltpu.CompilerParams(dimension_semantics=('arbitrary',),
                                             vmem_limit_bytes=VMEM_LIMIT),
    )(dest3, dest3, yd, x1, route, p, wts['g_ple'], wts['w_pg'], wts['w_ple'])


def _routing_plan(eid, blk):
    n_exp = N_GROUPS * EXPERTS_PER_GROUP
    e = eid.reshape(-1)
    n_assign = e.shape[0]
    onehot = (jnp.arange(n_exp, dtype=jnp.int32)[:, None] == e[None, :]).astype(jnp.int32)
    counts = jnp.sum(onehot, axis=1)
    pcounts = ((counts + blk - 1) // blk) * blk
    pend = jnp.cumsum(pcounts)
    pstart = pend - pcounts
    dest = jnp.sum(onehot * (jnp.cumsum(onehot, axis=1) - 1 + pstart[:, None]), axis=0).astype(jnp.int32)
    n_blocks = -(-(n_assign + n_exp * (blk - 1)) // blk)
    blk_start = jnp.arange(n_blocks, dtype=jnp.int32) * blk
    blk_e = jnp.minimum(jnp.sum(pend[None, :] <= blk_start[:, None], axis=1), n_exp - 1).astype(jnp.int32)
    n_real = (pend[-1:] // blk).astype(jnp.int32)
    return dest.reshape(eid.shape), blk_e, n_real, n_blocks * blk


def _rope_tables(pos, n_heads):
    half = HEAD_DIM // 2
    inv = ROPE_THETA ** (-jnp.arange(half, dtype=F32) / half)
    ang = pos.astype(F32)[:, None] * inv[None, :]
    cos = jnp.cos(ang)
    sin = jnp.sin(ang)
    return (jnp.tile(jnp.concatenate([cos, cos], axis=-1), (1, n_heads)),
            jnp.tile(jnp.concatenate([-sin, sin], axis=-1), (1, n_heads)))


def _mix_tables(w_s_l, b_s_l, t_mix, a_width):
    reps = CHUNK // t_mix
    tri = jnp.tril(jnp.ones((t_mix, t_mix), F32))
    wt = w_s_l[:, :t_mix, :t_mix] * tri[None]
    eye = jnp.eye(reps, dtype=F32)
    wbig = jnp.einsum('ab,gts->gatbs', eye, wt).reshape(A_GROUPS, CHUNK, CHUNK)
    wmix = jnp.transpose(wbig, (1, 0, 2)).reshape(CHUNK, A_GROUPS * CHUNK)
    bs = jnp.tile(b_s_l[:, :t_mix], (1, reps))
    bs_tab = jnp.repeat(bs.T, a_width // A_GROUPS, axis=1)
    return dict(wmix=wmix.astype(BF16), bs_tab=bs_tab)


def _layer_weights(l, g_mix, w_in, g_v, g_q, g_k, w_a, w_b, w_o, g_ffn, w_rg, b_rg, w_re, b_re,
                   w1, w3, w2, g_ple, w_pg, w_ple):
    b_width = w_b.shape[1]
    n_heads = b_width // HEAD_DIM
    d_model = w_o.shape[-1]
    hid = jnp.arange(b_width) // HEAD_DIM
    hind = jnp.where(hid[:, None] == hid[None, :], 1.0 / HEAD_DIM, 0.0)
    n_exp = N_GROUPS * EXPERTS_PER_GROUP
    w_r = jnp.zeros((d_model, PLE_LANES), F32)
    w_r = w_r.at[:, :N_GROUPS].set(w_rg[l]).at[:, N_GROUPS:N_GROUPS + n_exp].set(w_re[l])
    b_r = jnp.zeros((1, PLE_LANES), F32)
    b_r = b_r.at[0, :N_GROUPS].set(b_rg[l]).at[0, N_GROUPS:N_GROUPS + n_exp].set(b_re[l])
    return dict(
        g_mix=g_mix[l][None], w_in=w_in[l].astype(BF16), g_v=g_v[l][None], hind=hind.astype(BF16),
        g_q=jnp.tile(g_q[l], n_heads)[None], g_k=jnp.tile(g_k[l], n_heads)[None],
        w_a=w_a[l].astype(BF16), w_b=w_b[l].astype(BF16), w_o=w_o[l].astype(BF16),
        g_ffn=g_ffn[l][None], w_r=w_r.astype(BF16), b_r=b_r,
        w1=w1[l], w3=w3[l], w2=w2[l],
        g_ple=g_ple[l][None], w_pg=w_pg[l].astype(BF16), w_ple=w_ple[l].astype(BF16),
    )


def _finish(x, aterm, sgb, out_b, p, wts):
    x1, hf, route, route_t = _post_attn(x, aterm, sgb, out_b, wts)
    eid = route_t[:2].astype(jnp.int32)
    dest, blk_e, n_real, n_rows = _routing_plan(eid, MOE_BLOCK)
    xd = _dispatch(hf, dest, n_rows)
    yd = _moe_ffn(xd, blk_e, n_real, wts['w1'], wts['w3'], wts['w2'])
    return _final(x1, yd, dest, route, p, wts)


def kernel(x_prompt, x_sample, cache_k, cache_v, page_table, p_prompt, p_sample, g_mix, w_in, g_v, w_s, b_s, g_q, g_k, w_a, w_b, w_o, g_ffn, w_router_group, b_router_group, w_router_expert, b_router_expert, w1, w3, w2, g_ple, w_ple_gate, w_ple):
    bsz, seq, d_model = x_prompt.shape
    dbsz, dseq, _ = x_sample.shape
    depth = g_mix.shape[0]
    b_width = w_b.shape[1]
    n_heads = b_width // HEAD_DIM
    past_len = page_table.shape[1] * PAGE_SIZE
    assert seq % MOBA_BLOCK == 0 and (bsz * seq) % ROW_TILE == 0 and (dbsz * dseq) % ROW_TILE == 0
    assert CHUNK % dseq == 0 and ROW_TILE % CHUNK == 0 and ROW_TILE % MOBA_BLOCK == 0
    assert seq // MOBA_BLOCK >= MOBA_TOPK
    params = (g_mix, w_in, g_v, g_q, g_k, w_a, w_b, w_o, g_ffn, w_router_group, b_router_group,
              w_router_expert, b_router_expert, w1, w3, w2, g_ple, w_ple_gate, w_ple)
    a_width = g_v.shape[-1]
    tab_p = _rope_tables(jnp.arange(seq, dtype=jnp.int32), n_heads)
    pos_s = past_len + (jnp.arange(ROW_TILE, dtype=jnp.int32) % dseq)
    tab_s = _rope_tables(pos_s, n_heads)
    xp = x_prompt.reshape(bsz * seq, d_model)
    xs = x_sample.reshape(dbsz * dseq, d_model)
    kp_rows, vp_rows, ks_rows, vs_rows, chunk_rows = [], [], [], [], []
    for l in range(depth):
        wts = _layer_weights(l, *params)
        wts_p = dict(wts, **_mix_tables(w_s[l], b_s[l], CHUNK, a_width))
        wts_s = dict(wts, **_mix_tables(w_s[l], b_s[l], dseq, a_width))
        aterm, sgb, qt, k, v, kb, vt, kmean = _inproj(
            xp, tab_p, wts_p, emit_kmean=True, emit_vchunk=False, pos_blocks=seq // ROW_TILE)
        nbt = bsz * seq // MOBA_BLOCK
        out_b = _moba_prompt(qt, kb.reshape(nbt, MOBA_BLOCK, b_width), vt,
                             kmean.reshape(bsz, seq // MOBA_BLOCK, b_width), bsz, seq)
        xp = _finish(xp, aterm, sgb, out_b, p_prompt[l].reshape(bsz * seq, -1), wts_p)
        kp_rows.append(jnp.transpose(k.reshape(bsz, n_heads, HEAD_DIM, seq), (0, 3, 1, 2)))
        vp_rows.append(jnp.transpose(v.reshape(bsz, n_heads, HEAD_DIM, seq), (0, 3, 1, 2)))
        aterm, sgb, q, k, v, kb, vb, vchunk = _inproj(
            xs, tab_s, wts_s, emit_kmean=False, emit_vchunk=True, pos_blocks=1)
        out_b = _moba_sample(q, kb, vb, cache_k[l], cache_v[l], page_table, dseq)
        xs = _finish(xs, aterm, sgb, out_b, p_sample[l].reshape(dbsz * dseq, -1), wts_s)
        ks_rows.append(k.reshape(dbsz, dseq, n_heads, HEAD_DIM))
        vs_rows.append(v.reshape(dbsz, dseq, n_heads, HEAD_DIM))
        chunk_rows.append(vchunk.reshape(dbsz, dseq, -1))
    return (xp.reshape(bsz, seq, d_model), xs.reshape(dbsz, dseq, d_model),
            jnp.stack(kp_rows), jnp.stack(vp_rows), jnp.stack(ks_rows), jnp.stack(vs_rows),
            jnp.stack(chunk_rows))
```

```python
import functools

import jax
import jax.numpy as jnp
from jax import lax
from jax.experimental import pallas as pl
from jax.experimental.pallas import tpu as pltpu

F32 = jnp.float32
BF16 = jnp.bfloat16

EPS = 1e-6
NEG_INF = -1e30
A_GROUPS = 8
CHUNK = 128
HEAD_DIM = 64
MOBA_BLOCK = 256
MOBA_TOPK = 3
ROPE_THETA = 10000.0
N_GROUPS = 4
EXPERTS_PER_GROUP = 8
PAGE_SIZE = 128
PLE_LANES = 128
ROUTE_ROWS = 8
ROW_TILE = 256
TOKEN_TILE = 512
MOE_BLOCK = 256
PAGES_PER_STEP = 16
PAGE_SLOTS = 3
VMEM_LIMIT = 56 * 1024 * 1024
LOG2_E = 1.4426950408889634
ONES_ROWS = 16
DMA_ISSUE_UNROLL = 8

_NT = (((1,), (1,)), ((), ()))


def _rms(x, g):
    return x * lax.rsqrt(jnp.mean(x * x, axis=-1, keepdims=True) + EPS) * g


def _dot(a, b):
    return jnp.dot(a, b, preferred_element_type=F32)


def _dot_nt(a, b):
    return lax.dot_general(a, b, _NT, preferred_element_type=F32)


def _dot_tn(a, b):
    return lax.dot_general(a, b, (((0,), (0,)), ((), ())), preferred_element_type=F32)


def _const_spec(shape):
    return pl.BlockSpec(shape, lambda *_: (0,) * len(shape))


def _inproj_kernel(x_ref, gmix_ref, win_ref, gv_ref, wmix_ref, bs_ref, hind_ref, gq_ref, gk_ref,
                   cos_ref, sin_ref, wa_ref,
                   aterm_ref, sgb_ref, q_ref, k_ref, v_ref, kb_ref, vb_ref, *extra,
                   a_width, b_width, d_model, emit_kmean, emit_vchunk, transpose_qv):
    tm = x_ref.shape[0]
    h = _rms(x_ref[...], gmix_ref[...]).astype(BF16)
    offs = [0]

    def proj(width):
        o = offs[0]
        offs[0] = o + width
        return _dot(h, win_ref[:, o:o + width])

    zu = proj(a_width)
    zv = proj(a_width)
    zq = proj(b_width)
    zk = proj(b_width)
    zva = proj(b_width)

    u = jax.nn.gelu(zu)
    vn = _rms(jax.nn.gelu(zv), gv_ref[...])
    vb16 = vn.astype(BF16)
    lane_grp = lax.broadcasted_iota(jnp.int32, (CHUNK, a_width), 1) // (a_width // A_GROUPS)
    parts = []
    for c in range(tm // CHUNK):
        vc = vb16[c * CHUNK:(c + 1) * CHUNK, :]
        rhs = jnp.concatenate(
            [jnp.where(lane_grp == g, vc, jnp.zeros_like(vc)) for g in range(A_GROUPS)], axis=0)
        parts.append(_dot(wmix_ref[...], rhs) + bs_ref[...])
    s = parts[0] if len(parts) == 1 else jnp.concatenate(parts, axis=0)
    out_a = (u * s).astype(BF16)
    ga = proj(d_model)
    aterm_ref[...] = jax.nn.sigmoid(ga) * _dot(out_a, wa_ref[...])
    gb = proj(d_model)
    sgb_ref[...] = jax.nn.sigmoid(gb)

    lane = lax.broadcasted_iota(jnp.int32, (tm, b_width), 1)
    first_half = (lane % HEAD_DIM) < (HEAD_DIM // 2)
    cos = cos_ref[...]
    sin = sin_ref[...]

    def headnorm_rope(z, g):
        ms = _dot((z * z).astype(BF16), hind_ref[...])
        y = z * lax.rsqrt(ms + EPS) * g
        swapped = jnp.where(first_half,
                            pltpu.roll(y, b_width - HEAD_DIM // 2, 1),
                            pltpu.roll(y, HEAD_DIM // 2, 1))
        return y * cos + swapped * sin

    q = headnorm_rope(zq, gq_ref[...])
    k = headnorm_rope(zk, gk_ref[...])
    kb_ref[...] = k.astype(BF16)
    if transpose_qv:
        vt = zva.T
        q_ref[0] = q.T.astype(BF16)
        k_ref[0] = k.T
        v_ref[0] = vt
        vb_ref[0] = vt.astype(BF16)
    else:
        q_ref[...] = q.astype(BF16)
        k_ref[...] = k
        v_ref[...] = zva
        vb_ref[...] = zva.astype(BF16)
    idx = 0
    if emit_kmean:
        km_ref = extra[idx]
        idx += 1
        for bi in range(tm // MOBA_BLOCK):
            km_ref[bi] = jnp.mean(k[bi * MOBA_BLOCK:(bi + 1) * MOBA_BLOCK, :], axis=0, keepdims=True)
    if emit_vchunk:
        extra[idx][...] = vn


def _inproj(x, pos_tables, wts, *, emit_kmean, emit_vchunk, pos_blocks):
    n, d_model = x.shape
    a_width = wts['g_v'].shape[-1]
    b_width = wts['hind'].shape[0]
    tm = ROW_TILE
    cos_t, sin_t = pos_tables
    row = lambda i: (i, 0)
    posrow = lambda i: (i % pos_blocks, 0)
    in_specs = [
        pl.BlockSpec((tm, d_model), row),
        _const_spec((1, d_model)),
        _const_spec(wts['w_in'].shape),
        _const_spec((1, a_width)),
        _const_spec(wts['wmix'].shape),
        _const_spec(wts['bs_tab'].shape),
        _const_spec(wts['hind'].shape),
        _const_spec((1, b_width)),
        _const_spec((1, b_width)),
        pl.BlockSpec((tm, b_width), posrow),
        pl.BlockSpec((tm, b_width), posrow),
        _const_spec(wts['w_a'].shape),
    ]
    transpose_qv = emit_kmean
    if transpose_qv:
        assert tm == MOBA_BLOCK
        qv_shape = jax.ShapeDtypeStruct((n // tm, b_width, tm), BF16)
        qv_spec = pl.BlockSpec((1, b_width, tm), lambda i: (i, 0, 0))
        kv_shape = jax.ShapeDtypeStruct((n // (tm * pos_blocks), b_width, tm * pos_blocks), F32)
        kv_spec = pl.BlockSpec((1, b_width, tm), lambda i: (i // pos_blocks, 0, i % pos_blocks))
    else:
        qv_shape = jax.ShapeDtypeStruct((n, b_width), BF16)
        qv_spec = pl.BlockSpec((tm, b_width), row)
        kv_shape = jax.ShapeDtypeStruct((n, b_width), F32)
        kv_spec = pl.BlockSpec((tm, b_width), row)
    out_shape = [
        jax.ShapeDtypeStruct((n, d_model), F32),
        jax.ShapeDtypeStruct((n, d_model), F32),
        qv_shape,
        kv_shape,
        kv_shape,
        jax.ShapeDtypeStruct((n, b_width), BF16),
        qv_shape,
    ]
    out_specs = [
        pl.BlockSpec((tm, d_model), row), pl.BlockSpec((tm, d_model), row),
        qv_spec, kv_spec, kv_spec, pl.BlockSpec((tm, b_width), row),
        qv_spec,
    ]
    if emit_kmean:
        nbt = tm // MOBA_BLOCK
        out_shape.append(jax.ShapeDtypeStruct((n // MOBA_BLOCK, 1, b_width), F32))
        out_specs.append(pl.BlockSpec((nbt, 1, b_width), lambda i: (i, 0, 0)))
    if emit_vchunk:
        out_shape.append(jax.ShapeDtypeStruct((n, a_width), F32))
        out_specs.append(pl.BlockSpec((tm, a_width), row))
    kern = functools.partial(_inproj_kernel, a_width=a_width, b_width=b_width, d_model=d_model,
                             emit_kmean=emit_kmean, emit_vchunk=emit_vchunk, transpose_qv=transpose_qv)
    return pl.pallas_call(
        kern, grid=(n // tm,), in_specs=in_specs, out_specs=out_specs, out_shape=out_shape,
        name='inproj',
        compiler_params=pltpu.CompilerParams(dimension_semantics=('arbitrary',),
                                             vmem_limit_bytes=VMEM_LIMIT),
    )(x, wts['g_mix'], wts['w_in'], wts['g_v'], wts['wmix'], wts['bs_tab'], wts['hind'],
      wts['g_q'], wts['g_k'], cos_t, sin_t, wts['w_a'])


def _select_topk(scores, allowed, blk_f, nb):
    sel = jnp.zeros(scores.shape, jnp.bool_)
    for _ in range(MOBA_TOPK):
        cand = jnp.logical_and(allowed, jnp.logical_not(sel))
        scm = jnp.where(cand, scores, -jnp.inf)
        mx = jnp.max(scm, axis=-1, keepdims=True)
        is_max = jnp.logical_and(cand, scm == mx)
        first = jnp.min(jnp.where(is_max, blk_f, float(nb)), axis=-1, keepdims=True)
        sel = jnp.logical_or(sel, jnp.logical_and(is_max, blk_f == first))
    return sel


def _select_topk_rows(scores, allowed, blk_f, nb):
    sel = jnp.zeros(scores.shape, jnp.bool_)
    for _ in range(MOBA_TOPK):
        cand = jnp.logical_and(allowed, jnp.logical_not(sel))
        scm = jnp.where(cand, scores, -jnp.inf)
        mx = jnp.max(scm, axis=0, keepdims=True)
        is_max = jnp.logical_and(cand, scm == mx)
        first = jnp.min(jnp.where(is_max, blk_f, float(nb)), axis=0, keepdims=True)
        sel = jnp.logical_or(sel, jnp.logical_and(is_max, blk_f == first))
    return sel


def _moba_prompt_kernel(qt_ref, kb_ref, vt_ref, km_ref, o_ref, w_ref, bias_ref, m_ref, l_ref, acc_ref,
                        s_ref, *, n_heads, nb):
    tq = qt_ref.shape[2]
    blk = MOBA_BLOCK
    pair = 2 * HEAD_DIM
    i = pl.program_id(1)
    scale = HEAD_DIM ** -0.5 * LOG2_E
    km = km_ref[0].astype(BF16)
    blk_f = lax.broadcasted_iota(jnp.int32, (nb, tq), 0).astype(F32)
    allowed = blk_f < i.astype(F32)
    key_t = lax.broadcasted_iota(jnp.int32, (blk, tq), 0)
    qry_t = lax.broadcasted_iota(jnp.int32, (blk, tq), 1)
    causal = key_t <= qry_t
    zeros = jnp.zeros((HEAD_DIM, tq), BF16)
    ones = jnp.ones((ONES_ROWS, blk), BF16)

    for h in range(n_heads):
        qth = qt_ref[0, h * HEAD_DIM:(h + 1) * HEAD_DIM, :]
        sel = _select_topk_rows(_dot(km[:, h * HEAD_DIM:(h + 1) * HEAD_DIM], qth), allowed, blk_f, nb)
        bias_ref[h] = jnp.where(sel, 0.0, NEG_INF)
        qs = (qth.astype(F32) * scale).astype(BF16)
        col = jnp.concatenate([qs, zeros] if h % 2 == 0 else [zeros, qs], axis=0)
        w_ref[h // 2, :, (h % 2) * tq:(h % 2 + 1) * tq] = col

    def scores(j, hp):
        return _dot(kb_ref[j, :, hp * pair:(hp + 1) * pair], w_ref[hp])

    def weighted_values(j, h, p):
        vt1 = jnp.concatenate([vt_ref[j, h * HEAD_DIM:(h + 1) * HEAD_DIM, :], ones], axis=0)
        pv = _dot(vt1, p.astype(BF16))
        return pv[:HEAD_DIM, :], pv[HEAD_DIM:HEAD_DIM + 1, :]

    for hp in range(n_heads // 2):
        s2 = scores(i, hp)
        for h in (2 * hp, 2 * hp + 1):
            s = jnp.where(causal, s2[:, (h % 2) * tq:(h % 2 + 1) * tq], NEG_INF)
            m = jnp.max(s, axis=0, keepdims=True)
            pv, psum = weighted_values(i, h, jnp.exp2(s - m))
            m_ref[h:h + 1, :] = m
            l_ref[h:h + 1, :] = psum
            acc_ref[h * HEAD_DIM:(h + 1) * HEAD_DIM, :] = pv

    def stage_scores(j, slot):
        jc = jnp.minimum(j, nb - 1)
        for hp in range(n_heads // 2):
            s_ref[slot, hp] = scores(jc, hp)

    def consume(j, slot):
        jc = jnp.minimum(j, nb - 1)
        for h in range(n_heads):
            s = s_ref[slot, h // 2, :, (h % 2) * tq:(h % 2 + 1) * tq]
            bias = bias_ref[h, pl.ds(jc, 1), :]
            m = m_ref[h:h + 1, :]
            m_new = jnp.maximum(m, jnp.max(s, axis=0, keepdims=True) + bias)
            alpha = jnp.exp2(m - m_new)
            pv, psum = weighted_values(jc, h, jnp.exp2(s + (bias - m_new)))
            m_ref[h:h + 1, :] = m_new
            l_ref[h:h + 1, :] = alpha * l_ref[h:h + 1, :] + psum
            rows = slice(h * HEAD_DIM, (h + 1) * HEAD_DIM)
            acc_ref[rows, :] = alpha * acc_ref[rows, :] + pv

    @pl.when(i > 0)
    def _():
        stage_scores(0, 0)

        def body(t, carry):
            j = 2 * t
            stage_scores(j + 1, 1)
            consume(j, 0)
            stage_scores(j + 2, 0)
            consume(j + 1, 1)
            return carry

        lax.fori_loop(0, (i + 1) // 2, body, 0)
    outs = [acc_ref[h * HEAD_DIM:(h + 1) * HEAD_DIM, :] / l_ref[h:h + 1, :] for h in range(n_heads)]
    o_ref[...] = jnp.concatenate(outs, axis=0).T.astype(o_ref.dtype)


def _moba_prompt(qt, kb, vt, kmean, bsz, seq):
    _, width, tq = qt.shape
    n_heads = width // HEAD_DIM
    nb = seq // MOBA_BLOCK
    assert tq == MOBA_BLOCK and n_heads % 2 == 0
    kern = functools.partial(_moba_prompt_kernel, n_heads=n_heads, nb=nb)
    return pl.pallas_call(
        kern, grid=(bsz, nb),
        in_specs=[
            pl.BlockSpec((1, width, tq), lambda b, i: (b * nb + i, 0, 0)),
            pl.BlockSpec((nb, MOBA_BLOCK, width), lambda b, i: (b, 0, 0)),
            pl.BlockSpec((nb, width, MOBA_BLOCK), lambda b, i: (b, 0, 0)),
            pl.BlockSpec((1, nb, width), lambda b, i: (b, 0, 0)),
        ],
        out_specs=pl.BlockSpec((tq, width), lambda b, i: (b * nb + i, 0)),
        out_shape=jax.ShapeDtypeStruct((bsz * seq, width), BF16),
        scratch_shapes=[pltpu.VMEM((n_heads // 2, 2 * HEAD_DIM, 2 * tq), BF16),
                        pltpu.VMEM((n_heads, nb, tq), F32),
                        pltpu.VMEM((n_heads, tq), F32), pltpu.VMEM((n_heads, tq), F32),
                        pltpu.VMEM((width, tq), F32),
                        pltpu.VMEM((2, n_heads // 2, MOBA_BLOCK, 2 * tq), F32)],
        name='moba_prompt',
        compiler_params=pltpu.CompilerParams(dimension_semantics=('arbitrary', 'arbitrary'),
                                             vmem_limit_bytes=VMEM_LIMIT),
    )(qt, kb, vt, kmean)


def _moba_sample_kernel(pt_ref, q_ref, kn_ref, vn_ref, *rest, pg, n_heads, nb, n_steps, n_total):
    ck_ref, cv_ref, o_ref, bs_ref, m_ref, l_ref, acc_ref, kbuf, vbuf, sems = rest
    ds, width = q_ref.shape
    step = pl.program_id(1)
    n_slots = kbuf.shape[0]
    ahead = n_slots - 1
    lin = pl.program_id(0) * n_steps + step
    slot = lin % n_slots

    def fetch(idx):
        bb = idx // n_steps
        ss = idx % n_steps
        sl = idx % n_slots
        for t in range(pg):
            page = pt_ref[bb, ss * pg + t]
            pltpu.make_async_copy(ck_ref.at[page], kbuf.at[sl, t], sems.at[sl, 0]).start()
            pltpu.make_async_copy(cv_ref.at[page], vbuf.at[sl, t], sems.at[sl, 1]).start()

    @pl.when(lin == 0)
    def _():
        for a in range(min(ahead, n_total)):
            fetch(jnp.int32(a))

    @pl.when(lin + ahead < n_total)
    def _():
        fetch(lin + ahead)

    pltpu.make_async_copy(ck_ref.at[pl.ds(0, pg)], kbuf.at[slot], sems.at[slot, 0]).wait()
    pltpu.make_async_copy(cv_ref.at[pl.ds(0, pg)], vbuf.at[slot], sems.at[slot, 1]).wait()
    k_refs = [kbuf.at[slot, t] for t in range(pg)]
    v_refs = [vbuf.at[slot, t] for t in range(pg)]
    r = n_heads * ds
    c2 = HEAD_DIM ** -0.5 * LOG2_E
    ppb = MOBA_BLOCK // PAGE_SIZE
    row_h = lax.broadcasted_iota(jnp.int32, (r, width), 0) // ds
    lane_h = lax.broadcasted_iota(jnp.int32, (r, width), 1) // HEAD_DIM
    hmask = row_h == lane_h
    qt = jnp.concatenate([q_ref[...]] * n_heads, axis=0)
    qbd = jnp.where(hmask, qt, jnp.zeros_like(qt))
    lane_b = lax.broadcasted_iota(jnp.int32, (r, nb), 1)

    @pl.when(step == 0)
    def _():
        bs_ref[...] = jnp.zeros_like(bs_ref)
        m_ref[...] = jnp.zeros_like(m_ref)
        l_ref[...] = jnp.zeros_like(l_ref)

    bs_new, m_new, l_new = bs_ref[...], m_ref[...], l_ref[...]
    kall = jnp.concatenate([k_refs[t][...] for t in range(pg)], axis=1).astype(BF16)
    st_all = _dot(qbd, kall)
    for c in range(pg // ppb):
        st = st_all[:, c * MOBA_BLOCK:(c + 1) * MOBA_BLOCK]
        vblk = jnp.concatenate([v_refs[ppb * c + t][...] for t in range(ppb)], axis=1).astype(BF16)
        s = st * c2
        m = jnp.max(s, axis=-1, keepdims=True)
        p = jnp.exp2(s - m)
        jb = step * (pg // ppb) + c
        col = lane_b == jb
        bs_new = jnp.where(col, jnp.mean(st, axis=-1, keepdims=True), bs_new)
        m_new = jnp.where(col, m, m_new)
        l_new = jnp.where(col, jnp.sum(p, axis=-1, keepdims=True), l_new)
        acc_ref[jb] = _dot_nt(p.astype(BF16), vblk)
    bs_ref[...] = bs_new
    m_ref[...] = m_new
    l_ref[...] = l_new

    @pl.when(step == n_steps - 1)
    def _():
        sel = _select_topk(bs_ref[...], jnp.ones((r, nb), jnp.bool_), lane_b.astype(F32), nb)
        key_t = lax.broadcasted_iota(jnp.int32, (r, ds), 1)
        qry_t = lax.broadcasted_iota(jnp.int32, (r, ds), 0) % ds
        s_own = jnp.where(key_t <= qry_t, _dot_nt(qbd, kn_ref[...]) * c2, NEG_INF)
        m_all = m_ref[...]
        m_tot = jnp.maximum(jnp.max(jnp.where(sel, m_all, NEG_INF), axis=-1, keepdims=True),
                            jnp.max(s_own, axis=-1, keepdims=True))
        w = jnp.where(sel, jnp.exp2(m_all - m_tot), 0.0)
        p_own = jnp.exp2(s_own - m_tot)
        den = jnp.sum(w * l_ref[...], axis=-1, keepdims=True) + jnp.sum(p_own, axis=-1, keepdims=True)
        num = _dot(p_own.astype(BF16), vn_ref[...])

        def body(j, num):
            wj = jnp.sum(jnp.where(lane_b == j, w, 0.0), axis=-1, keepdims=True)
            return num + wj * acc_ref[j]

        out = jnp.where(hmask, lax.fori_loop(0, nb, body, num) / den, 0.0)
        o = out[0:ds, :]
        for h in range(1, n_heads):
            o = o + out[h * ds:(h + 1) * ds, :]
        o_ref[...] = o.astype(o_ref.dtype)


def _moba_sample(q, kb, vb, cache_k, cache_v, page_table, ds):
    n, width = q.shape
    dbsz, n_pages = page_table.shape
    n_heads = width // HEAD_DIM
    assert (n_pages * PAGE_SIZE) % MOBA_BLOCK == 0, "cached length must fill whole MoBA blocks"
    nb = n_pages * PAGE_SIZE // MOBA_BLOCK
    assert nb >= MOBA_TOPK
    pg = PAGES_PER_STEP
    ppb = MOBA_BLOCK // PAGE_SIZE
    assert n_pages % pg == 0 and pg % ppb == 0 and ds % 8 == 0
    assert cache_k.shape[1:] == (PAGE_SIZE, n_heads, HEAD_DIM)
    r = n_heads * ds
    ck = jnp.transpose(cache_k, (0, 2, 3, 1)).reshape(cache_k.shape[0], width, PAGE_SIZE)
    cv = jnp.transpose(cache_v, (0, 2, 3, 1)).reshape(cache_v.shape[0], width, PAGE_SIZE)

    tok = pl.BlockSpec((ds, width), lambda b, s, pt: (b, 0))
    hbm = pl.BlockSpec(memory_space=pl.ANY)
    grid_spec = pltpu.PrefetchScalarGridSpec(
        num_scalar_prefetch=1, grid=(dbsz, n_pages // pg),
        in_specs=[tok, tok, tok, hbm, hbm],
        out_specs=tok,
        scratch_shapes=[pltpu.VMEM((r, nb), F32), pltpu.VMEM((r, nb), F32), pltpu.VMEM((r, nb), F32),
                        pltpu.VMEM((nb, r, width), F32),
                        pltpu.VMEM((PAGE_SLOTS, pg, width, PAGE_SIZE), F32),
                        pltpu.VMEM((PAGE_SLOTS, pg, width, PAGE_SIZE), F32),
                        pltpu.SemaphoreType.DMA((PAGE_SLOTS, 2))],
    )
    kern = functools.partial(_moba_sample_kernel, pg=pg, n_heads=n_heads, nb=nb, n_steps=n_pages // pg,
                             n_total=dbsz * (n_pages // pg))
    return pl.pallas_call(
        kern, grid_spec=grid_spec, out_shape=jax.ShapeDtypeStruct((n, width), BF16),
        name='moba_sample',
        compiler_params=pltpu.CompilerParams(dimension_semantics=('arbitrary', 'arbitrary'),
                                             vmem_limit_bytes=VMEM_LIMIT),
    )(page_table, q, kb, vb, ck, cv)


def _pack_bf16_pairs(x):
    w = x.shape[1] // 2
    xb = x.astype(BF16).astype(F32)
    hi = lax.bitcast_convert_type(xb[:, :w], jnp.uint32) & jnp.uint32(0xFFFF0000)
    lo = lax.bitcast_convert_type(xb[:, w:], jnp.uint32) >> 16
    return hi | lo


def _unpack_bf16_pairs(u):
    hi = lax.bitcast_convert_type(u & jnp.uint32(0xFFFF0000), F32)
    lo = lax.bitcast_convert_type(u << 16, F32)
    return jnp.concatenate([hi, lo], axis=1).astype(BF16)


def _post_attn_kernel(x_ref, aterm_ref, sgb_ref, ob_ref, wb_ref, wo_ref, gffn_ref, wr_ref, br_ref,
                      x1_ref, hf_ref, route_ref, route_t_ref):
    merged = aterm_ref[...] + sgb_ref[...] * _dot(ob_ref[...], wb_ref[...])
    x1 = x_ref[...] + _dot(merged.astype(BF16), wo_ref[...])
    x1_ref[...] = x1
    hf = _rms(x1, gffn_ref[...])
    hf_ref[...] = _pack_bf16_pairs(hf)
    logits = _dot(hf.astype(BF16), wr_ref[...]) + br_ref[...]
    tm, lanes = logits.shape
    lane = lax.broadcasted_iota(jnp.int32, (tm, lanes), 1).astype(F32)
    n_exp = N_GROUPS * EXPERTS_PER_GROUP
    gmask = lane < N_GROUPS
    gl = jnp.where(gmask, logits, -jnp.inf)
    gmax = jnp.max(gl, axis=-1, keepdims=True)
    grp = jnp.min(jnp.where(gl == gmax, lane, float(lanes)), axis=-1, keepdims=True)
    p_grp = 1.0 / jnp.sum(jnp.where(gmask, jnp.exp(gl - gmax), 0.0), axis=-1, keepdims=True)
    lo = N_GROUPS + grp * EXPERTS_PER_GROUP
    emask = jnp.logical_and(jnp.logical_and(lane >= lo, lane < lo + EXPERTS_PER_GROUP),
                            lane < N_GROUPS + n_exp)
    e1 = jnp.where(emask, logits, -jnp.inf)
    v1 = jnp.max(e1, axis=-1, keepdims=True)
    j1 = jnp.min(jnp.where(e1 == v1, lane, float(lanes)), axis=-1, keepdims=True)
    e2 = jnp.where(lane == j1, -jnp.inf, e1)
    v2 = jnp.max(e2, axis=-1, keepdims=True)
    j2 = jnp.min(jnp.where(e2 == v2, lane, float(lanes)), axis=-1, keepdims=True)
    t = jnp.exp(v2 - v1)
    p1 = 1.0 / (1.0 + t)
    p2 = t / (1.0 + t)
    rec = jnp.where(lane == 0, j1 - N_GROUPS, 0.0)
    rec = jnp.where(lane == 1, j2 - N_GROUPS, rec)
    rec = jnp.where(lane == 2, p_grp * p1, rec)
    rec = jnp.where(lane == 3, p_grp * p2, rec)
    route_ref[...] = rec
    route_t_ref[...] = rec.T[:ROUTE_ROWS, :]


def _post_attn(x, aterm, sgb, out_b, wts):
    n, d_model = x.shape
    tm = min(TOKEN_TILE, n)
    row = lambda i: (i, 0)
    return pl.pallas_call(
        _post_attn_kernel, grid=(n // tm,),
        in_specs=[pl.BlockSpec((tm, d_model), row), pl.BlockSpec((tm, d_model), row),
                  pl.BlockSpec((tm, d_model), row), pl.BlockSpec((tm, out_b.shape[1]), row),
                  _const_spec(wts['w_b'].shape), _const_spec(wts['w_o'].shape),
                  _const_spec((1, d_model)), _const_spec(wts['w_r'].shape), _const_spec((1, PLE_LANES))],
        out_specs=[pl.BlockSpec((tm, d_model), row), pl.BlockSpec((tm, d_model // 2), row),
                   pl.BlockSpec((tm, PLE_LANES), row), pl.BlockSpec((ROUTE_ROWS, tm), lambda i: (0, i))],
        out_shape=[jax.ShapeDtypeStruct((n, d_model), F32),
                   jax.ShapeDtypeStruct((n, d_model // 2), jnp.uint32),
                   jax.ShapeDtypeStruct((n, PLE_LANES), F32),
                   jax.ShapeDtypeStruct((ROUTE_ROWS, n), F32)],
        name='post_attn',
        compiler_params=pltpu.CompilerParams(dimension_semantics=('arbitrary',),
                                             vmem_limit_bytes=VMEM_LIMIT),
    )(x, aterm, sgb, out_b, wts['w_b'], wts['w_o'], wts['g_ffn'], wts['w_r'], wts['b_r'])


def _dispatch_kernel(dest_ref, hf_ref, xd_in_ref, xd_ref, sem, *, fanout):
    del xd_in_ref
    tm = hf_ref.shape[0]

    for rr in range(tm):
        for kk in range(fanout):
            pltpu.make_async_copy(hf_ref.at[pl.ds(rr, 1)],
                                  xd_ref.at[pl.ds(dest_ref[0, kk, rr], 1)], sem).start()
    for kk in range(fanout):
        pltpu.make_async_copy(hf_ref, xd_ref.at[pl.ds(0, tm)], sem).wait()


def _dest_tiles(dest, tm):
    fanout, n = dest.shape
    return dest.reshape(fanout, n // tm, tm).transpose(1, 0, 2)


def _dispatch(hf, dest, n_rows):
    n, w = hf.shape
    fanout = dest.shape[0]
    tm = min(n, 512)
    steps = n // tm
    dest3 = _dest_tiles(dest, tm)
    xd0 = jnp.zeros((n_rows, w), hf.dtype)
    kern = functools.partial(_dispatch_kernel, fanout=fanout)
    return pl.pallas_call(
        kern, grid=(steps,),
        in_specs=[pl.BlockSpec((1, fanout, tm), lambda i: (i, 0, 0), memory_space=pltpu.SMEM),
                  pl.BlockSpec((tm, w), lambda i: (i, 0)), pl.BlockSpec(memory_space=pl.ANY)],
        out_specs=pl.BlockSpec(memory_space=pl.ANY),
        out_shape=jax.ShapeDtypeStruct((n_rows, w), hf.dtype),
        scratch_shapes=[pltpu.SemaphoreType.DMA(())],
        input_output_aliases={2: 0},
        name='dispatch',
        compiler_params=pltpu.CompilerParams(dimension_semantics=('arbitrary',)),
    )(dest3, hf, xd0)


def _moe_ffn_kernel(be_ref, nreal_ref, xd_ref, w1_ref, w3_ref, w2_ref, yd_ref, w1b_ref, w3b_ref, w2b_ref):
    i = pl.program_id(0)
    live = i < nreal_ref[0]
    new_expert = jnp.logical_or(i == 0, be_ref[i] != be_ref[jnp.maximum(i - 1, 0)])

    @pl.when(jnp.logical_and(live, new_expert))
    def _():
        w1b_ref[...] = w1_ref[...].astype(BF16)
        w3b_ref[...] = w3_ref[...].astype(BF16)
        w2b_ref[...] = w2_ref[...].astype(BF16)

    @pl.when(live)
    def _():
        xb = _unpack_bf16_pairs(xd_ref[...])
        a = _dot(xb, w1b_ref[...])
        b = _dot(xb, w3b_ref[...])
        yd_ref[...] = _dot((jax.nn.silu(a) * b).astype(BF16), w2b_ref[...])

    @pl.when(i >= nreal_ref[0])
    def _():
        yd_ref[...] = jnp.zeros_like(yd_ref)


def _moe_ffn(xd, blk_e, n_real, w1, w3, w2):
    n_rows, half = xd.shape
    d_model = 2 * half
    ff = w1.shape[-1]
    blk = MOE_BLOCK
    grid_spec = pltpu.PrefetchScalarGridSpec(
        num_scalar_prefetch=2, grid=(n_rows // blk,),
        in_specs=[pl.BlockSpec((blk, half), lambda i, be, nr: (i, 0)),
                  pl.BlockSpec((None, d_model, ff), lambda i, be, nr: (be[i], 0, 0)),
                  pl.BlockSpec((None, d_model, ff), lambda i, be, nr: (be[i], 0, 0)),
                  pl.BlockSpec((None, ff, d_model), lambda i, be, nr: (be[i], 0, 0))],
        out_specs=pl.BlockSpec((blk, d_model), lambda i, be, nr: (i, 0)),
        scratch_shapes=[pltpu.VMEM((d_model, ff), BF16), pltpu.VMEM((d_model, ff), BF16),
                        pltpu.VMEM((ff, d_model), BF16)],
    )
    return pl.pallas_call(
        _moe_ffn_kernel, grid_spec=grid_spec,
        out_shape=jax.ShapeDtypeStruct((n_rows, d_model), F32),
        name='moe_ffn',
        compiler_params=pltpu.CompilerParams(dimension_semantics=('arbitrary',),
                                             vmem_limit_bytes=VMEM_LIMIT),
    )(blk_e, n_real, xd, w1, w3, w2)


def _final_kernel(d0_ref, dn_ref, yd_ref, x1_ref, route_ref, p_ref, gple_ref, wpg_ref, wple_ref,
                  y_ref, ybuf, sems, *, fanout):
    tm = x1_ref.shape[0]
    step = pl.program_id(0)
    slot = step % 2

    def start_rows(idx_ref, s, static_rows):
        def issue(rr, carry):
            for kk in range(fanout):
                pltpu.make_async_copy(yd_ref.at[pl.ds(idx_ref[0, kk, rr], 1)],
                                      ybuf.at[s, kk, pl.ds(rr, 1)], sems.at[s]).start()
            return carry

        if static_rows:
            for rr in range(tm):
                issue(rr, 0)
        else:
            lax.fori_loop(0, tm, issue, 0, unroll=DMA_ISSUE_UNROLL)

    def wait_rows(s):
        for kk in range(fanout):
            pltpu.make_async_copy(yd_ref.at[pl.ds(0, tm)], ybuf.at[s, kk], sems.at[s]).wait()

    @pl.when(step == 0)
    def _():
        start_rows(d0_ref, 0, False)

    start_rows(dn_ref, 1 - slot, True)
    wait_rows(slot)

    route = route_ref[...]
    moe = route[:, 2:3] * ybuf[slot, 0]
    for kk in range(1, fanout):
        moe = moe + route[:, 2 + kk:3 + kk] * ybuf[slot, kk]
    x2 = x1_ref[...] + moe
    gate = jax.nn.sigmoid(_dot(_rms(x2, gple_ref[...]).astype(BF16), wpg_ref[...]))
    y_ref[...] = x2 + gate * _dot(p_ref[...].astype(BF16), wple_ref[...])

    @pl.when(step == pl.num_programs(0) - 1)
    def _():
        wait_rows(1 - slot)


def _final(x1, yd, dest, route, p, wts):
    n, d_model = x1.shape
    fanout = dest.shape[0]
    tm = min(TOKEN_TILE, n)
    steps = n // tm
    dest3 = _dest_tiles(dest, tm)
    row = lambda i: (i, 0)
    kern = functools.partial(_final_kernel, fanout=fanout)
    return pl.pallas_call(
        kern, grid=(steps,),
        in_specs=[pl.BlockSpec((1, fanout, tm), lambda i: (0, 0, 0), memory_space=pltpu.SMEM),
                  pl.BlockSpec((1, fanout, tm), lambda i: (jnp.minimum(i + 1, steps - 1), 0, 0),
                               memory_space=pltpu.SMEM),
                  pl.BlockSpec(memory_space=pl.ANY),
                  pl.BlockSpec((tm, d_model), row), pl.BlockSpec((tm, PLE_LANES), row),
                  pl.BlockSpec((tm, p.shape[1]), row),
                  _const_spec((1, d_model)), _const_spec(wts['w_pg'].shape), _const_spec(wts['w_ple'].shape)],
        out_specs=pl.BlockSpec((tm, d_model), row),
        out_shape=jax.ShapeDtypeStruct((n, d_model), F32),
        scratch_shapes=[pltpu.VMEM((2, fanout, tm, d_model), F32), pltpu.SemaphoreType.DMA((2,))],
        name='final',
        compiler_params=pltpu.CompilerParams(dimension_semantics=('arbitrary',),
                                             vmem_limit_bytes=VMEM_LIMIT),
    )(dest3, dest3, yd, x1, route, p, wts['g_ple'], wts['w_pg'], wts['w_ple'])


def _routing_plan(eid, blk):
    n_exp = N_GROUPS * EXPERTS_PER_GROUP
    e = eid.reshape(-1)
    n_assign = e.shape[0]
    onehot = (jnp.arange(n_exp, dtype=jnp.int32)[:, None] == e[None, :]).astype(jnp.int32)
    counts = jnp.sum(onehot, axis=1)
    pcounts = ((counts + blk - 1) // blk) * blk
    pend = jnp.cumsum(pcounts)
    pstart = pend - pcounts
    dest = jnp.sum(onehot * (jnp.cumsum(onehot, axis=1) - 1 + pstart[:, None]), axis=0).astype(jnp.int32)
    n_blocks = -(-(n_assign + n_exp * (blk - 1)) // blk)
    blk_start = jnp.arange(n_blocks, dtype=jnp.int32) * blk
    blk_e = jnp.minimum(jnp.sum(pend[None, :] <= blk_start[:, None], axis=1), n_exp - 1).astype(jnp.int32)
    n_real = (pend[-1:] // blk).astype(jnp.int32)
    return dest.reshape(eid.shape), blk_e, n_real, n_blocks * blk


def _rope_tables(pos, n_heads):
    half = HEAD_DIM // 2
    inv = ROPE_THETA ** (-jnp.arange(half, dtype=F32) / half)
    ang = pos.astype(F32)[:, None] * inv[None, :]
    cos = jnp.cos(ang)
    sin = jnp.sin(ang)
    return (jnp.tile(jnp.concatenate([cos, cos], axis=-1), (1, n_heads)),
            jnp.tile(jnp.concatenate([-sin, sin], axis=-1), (1, n_heads)))


def _mix_tables(w_s_l, b_s_l, t_mix, a_width):
    reps = CHUNK // t_mix
    tri = jnp.tril(jnp.ones((t_mix, t_mix), F32))
    wt = w_s_l[:, :t_mix, :t_mix] * tri[None]
    eye = jnp.eye(reps, dtype=F32)
    wbig = jnp.einsum('ab,gts->gatbs', eye, wt).reshape(A_GROUPS, CHUNK, CHUNK)
    wmix = jnp.transpose(wbig, (1, 0, 2)).reshape(CHUNK, A_GROUPS * CHUNK)
    bs = jnp.tile(b_s_l[:, :t_mix], (1, reps))
    bs_tab = jnp.repeat(bs.T, a_width // A_GROUPS, axis=1)
    return dict(wmix=wmix.astype(BF16), bs_tab=bs_tab)


def _layer_weights(l, g_mix, w_in, g_v, g_q, g_k, w_a, w_b, w_o, g_ffn, w_rg, b_rg, w_re, b_re,
                   w1, w3, w2, g_ple, w_pg, w_ple):
    b_width = w_b.shape[1]
    n_heads = b_width // HEAD_DIM
    d_model = w_o.shape[-1]
    hid = jnp.arange(b_width) // HEAD_DIM
    hind = jnp.where(hid[:, None] == hid[None, :], 1.0 / HEAD_DIM, 0.0)
    n_exp = N_GROUPS * EXPERTS_PER_GROUP
    w_r = jnp.zeros((d_model, PLE_LANES), F32)
    w_r = w_r.at[:, :N_GROUPS].set(w_rg[l]).at[:, N_GROUPS:N_GROUPS + n_exp].set(w_re[l])
    b_r = jnp.zeros((1, PLE_LANES), F32)
    b_r = b_r.at[0, :N_GROUPS].set(b_rg[l]).at[0, N_GROUPS:N_GROUPS + n_exp].set(b_re[l])
    return dict(
        g_mix=g_mix[l][None], w_in=w_in[l].astype(BF16), g_v=g_v[l][None], hind=hind.astype(BF16),
        g_q=jnp.tile(g_q[l], n_heads)[None], g_k=jnp.tile(g_k[l], n_heads)[None],
        w_a=w_a[l].astype(BF16), w_b=w_b[l].astype(BF16), w_o=w_o[l].astype(BF16),
        g_ffn=g_ffn[l][None], w_r=w_r.astype(BF16), b_r=b_r,
        w1=w1[l], w3=w3[l], w2=w2[l],
        g_ple=g_ple[l][None], w_pg=w_pg[l].astype(BF16), w_ple=w_ple[l].astype(BF16),
    )


def _finish(x, aterm, sgb, out_b, p, wts):
    x1, hf, route, route_t = _post_attn(x, aterm, sgb, out_b, wts)
    eid = route_t[:2].astype(jnp.int32)
    dest, blk_e, n_real, n_rows = _routing_plan(eid, MOE_BLOCK)
    xd = _dispatch(hf, dest, n_rows)
    yd = _moe_ffn(xd, blk_e, n_real, wts['w1'], wts['w3'], wts['w2'])
    return _final(x1, yd, dest, route, p, wts)


def kernel(x_prompt, x_sample, cache_k, cache_v, page_table, p_prompt, p_sample, g_mix, w_in, g_v, w_s, b_s, g_q, g_k, w_a, w_b, w_o, g_ffn, w_router_group, b_router_group, w_router_expert, b_router_expert, w1, w3, w2, g_ple, w_ple_gate, w_ple):
    bsz, seq, d_model = x_prompt.shape
    dbsz, dseq, _ = x_sample.shape
    depth = g_mix.shape[0]
    b_width = w_b.shape[1]
    n_heads = b_width // HEAD_DIM
    past_len = page_table.shape[1] * PAGE_SIZE
    assert seq % MOBA_BLOCK == 0 and (bsz * seq) % ROW_TILE == 0 and (dbsz * dseq) % ROW_TILE == 0
    assert CHUNK % dseq == 0 and ROW_TILE % CHUNK == 0 and ROW_TILE % MOBA_BLOCK == 0
    assert seq // MOBA_BLOCK >= MOBA_TOPK
    params = (g_mix, w_in, g_v, g_q, g_k, w_a, w_b, w_o, g_ffn, w_router_group, b_router_group,
              w_router_expert, b_router_expert, w1, w3, w2, g_ple, w_ple_gate, w_ple)
    a_width = g_v.shape[-1]
    tab_p = _rope_tables(jnp.arange(seq, dtype=jnp.int32), n_heads)
    pos_s = past_len + (jnp.arange(ROW_TILE, dtype=jnp.int32) % dseq)
    tab_s = _rope_tables(pos_s, n_heads)
    xp = x_prompt.reshape(bsz * seq, d_model)
    xs = x_sample.reshape(dbsz * dseq, d_model)
    kp_rows, vp_rows, ks_rows, vs_rows, chunk_rows = [], [], [], [], []
    for l in range(depth):
        wts = _layer_weights(l, *params)
        wts_p = dict(wts, **_mix_tables(w_s[l], b_s[l], CHUNK, a_width))
        wts_s = dict(wts, **_mix_tables(w_s[l], b_s[l], dseq, a_width))
        aterm, sgb, qt, k, v, kb, vt, kmean = _inproj(
            xp, tab_p, wts_p, emit_kmean=True, emit_vchunk=False, pos_blocks=seq // ROW_TILE)
        nbt = bsz * seq // MOBA_BLOCK
        out_b = _moba_prompt(qt, kb.reshape(nbt, MOBA_BLOCK, b_width), vt,
                             kmean.reshape(bsz, seq // MOBA_BLOCK, b_width), bsz, seq)
        xp = _finish(xp, aterm, sgb, out_b, p_prompt[l].reshape(bsz * seq, -1), wts_p)
        kp_rows.append(jnp.transpose(k.reshape(bsz, n_heads, HEAD_DIM, seq), (0, 3, 1, 2)))
        vp_rows.append(jnp.transpose(v.reshape(bsz, n_heads, HEAD_DIM, seq), (0, 3, 1, 2)))
        aterm, sgb, q, k, v, kb, vb, vchunk = _inproj(
            xs, tab_s, wts_s, emit_kmean=False, emit_vchunk=True, pos_blocks=1)
        out_b = _moba_sample(q, kb, vb, cache_k[l], cache_v[l], page_table, dseq)
        xs = _finish(xs, aterm, sgb, out_b, p_sample[l].reshape(dbsz * dseq, -1), wts_s)
        ks_rows.append(k.reshape(dbsz, dseq, n_heads, HEAD_DIM))
        vs_rows.append(v.reshape(dbsz, dseq, n_heads, HEAD_DIM))
        chunk_rows.append(vchunk.reshape(dbsz, dseq, -1))
    return (xp.reshape(bsz, seq, d_model), xs.reshape(dbsz, dseq, d_model),
            jnp.stack(kp_rows), jnp.stack(vp_rows), jnp.stack(ks_rows), jnp.stack(vs_rows),
            jnp.stack(chunk_rows))
```
